```python
import math
import jax
import jax.numpy as jnp
from jax import lax
import numpy as np

D_MODEL = 1024
BATCH = 8
SEQ = 2048
DEPTH = 4
DEC_BATCH = 32
DEC_SEQ = 8
PAST_LEN = 16384
PAGE_SIZE = 128

F32 = jnp.float32
N_MIXERS = 4
N_LAYERS_A = (DEPTH + 3) // 4
N_LAYERS_B = (DEPTH + 2) // 4
N_LAYERS_C = (DEPTH + 1) // 4
N_LAYERS_D = DEPTH // 4
N_DENSE = (DEPTH + 1) // 2
N_MOE = DEPTH // 2
NORM_EPS = 1e-6
RES_SCALE = (2 * DEPTH) ** -0.5

GLA_HEADS = 4
GLA_DK = D_MODEL // 2
GLA_DV = D_MODEL
GLA_HK = GLA_DK // GLA_HEADS
GLA_HV = GLA_DV // GLA_HEADS
GLA_GATE_RANK = 16
GLA_GATE_NORMALIZER = 16.0
GLA_CHUNK = 64
GLA_IN = 2 * GLA_DK + 2 * GLA_DV + GLA_GATE_RANK

MLA_HEADS = 16
MLA_Q_LORA = 384
MLA_KV_LORA = 256
MLA_NOPE = 64
MLA_ROPE = 32
MLA_V = 64
MLA_SCALE = (MLA_NOPE + MLA_ROPE) ** -0.5
MLA_IN = MLA_Q_LORA + MLA_KV_LORA + MLA_ROPE
ROPE_THETA = 10000.0
Q_BLOCK = 128

SWA_HEADS = 16
SWA_KV_HEADS = 4
SWA_GROUPS = SWA_HEADS // SWA_KV_HEADS
SWA_HD = 64
SWA_SCALE = SWA_HD ** -0.5
SWA_IN = (SWA_HEADS + 2 * SWA_KV_HEADS) * SWA_HD
WINDOW = 128

DN_HEADS = 8
DN_HK = 128
DN_HV = 128
DN_CONV = 4
DN_CHUNK = 64
DN_CONV_DIM = 2 * DN_HEADS * DN_HK + DN_HEADS * DN_HV
DN_IN = DN_CONV_DIM + DN_HEADS * DN_HV + 2 * DN_HEADS

D_FF = 3584
N_EXPERTS = 8
TOP_K = 2

kernel_name = 'hybrid_gla_mla_swa_deltanet_step'


def rmsnorm(x, w):
    xf = x.astype(F32)
    y = xf * lax.rsqrt(jnp.mean(xf * xf, axis=-1, keepdims=True) + NORM_EPS)
    return (y * w.astype(F32)).astype(x.dtype)


def l2norm(x):
    xf = x.astype(F32)
    return xf * lax.rsqrt(jnp.sum(xf * xf, axis=-1, keepdims=True) + 1e-6)


def rope(x, pos):
    half = x.shape[-1] // 2
    freqs = jnp.exp(-math.log(ROPE_THETA) * jnp.arange(half, dtype=F32) / half)
    ang = pos.astype(F32)[:, None] * freqs[None, :]
    cos = jnp.cos(ang)[None, :, None, :]
    sin = jnp.sin(ang)[None, :, None, :]
    xf = x.astype(F32)
    x1, x2 = xf[..., :half], xf[..., half:]
    return jnp.concatenate([x1 * cos - x2 * sin, x1 * sin + x2 * cos], axis=-1).astype(x.dtype)


def pad_time(x, n_pad):
    if n_pad == 0:
        return x
    return jnp.pad(x, [(0, 0), (0, n_pad)] + [(0, 0)] * (x.ndim - 2))


def to_chunks(x, c):
    b, t, h = x.shape[:3]
    x = x.reshape((b, t // c, c, h) + x.shape[3:])
    return jnp.moveaxis(x, (1, 3), (0, 2))


def from_chunks(x, t):
    x = jnp.moveaxis(x, (0, 2), (1, 3))
    return x.reshape((x.shape[0], x.shape[1] * x.shape[2]) + x.shape[3:])[:, :t]


def gla_chunked(q, k, v, log_a, s0):
    t = q.shape[1]
    c = min(GLA_CHUNK, t)
    n = -(-t // c)
    pad = n * c - t
    q, k, v, log_a = [to_chunks(pad_time(z.astype(F32), pad), c) for z in (q, k, v, log_a)]
    gc = jnp.cumsum(log_a, axis=3)
    g_last = gc[:, :, :, -1:]
    q_in = q * jnp.exp(gc)
    k_in = k * jnp.exp(-gc)
    k_out = k * jnp.exp(g_last - gc)
    causal = jnp.tril(jnp.ones((c, c), bool))
    intra = jnp.where(causal, jnp.einsum('nbhid,nbhjd->nbhij', q_in, k_in), 0.0)
    o_intra = jnp.einsum('nbhij,nbhjv->nbhiv', intra, v)

    def step(s, xs):
        q_c, k_c, v_c, gl_c = xs
        o = jnp.einsum('bhid,bhdv->bhiv', q_c, s)
        s = s * jnp.exp(gl_c[:, :, 0, :, None]) + jnp.einsum('bhjd,bhjv->bhdv', k_c, v_c)
        return s, o

    s, o_inter = lax.scan(step, s0.astype(F32), (q_in, k_out, v, g_last))
    return from_chunks(o_inter + o_intra, t), s


def gated_delta_chunked(q, k, v, beta, g, s0):
    t = q.shape[1]
    dv = v.shape[-1]
    c = min(DN_CHUNK, t)
    n = -(-t // c)
    pad = n * c - t
    q, k, v, beta, g = [to_chunks(pad_time(z.astype(F32), pad), c) for z in (q, k, v, beta, g)]
    gc = jnp.cumsum(g, axis=-1)
    causal = jnp.tril(jnp.ones((c, c), bool))
    strict = jnp.tril(jnp.ones((c, c), bool), -1)
    diff = gc[..., :, None] - gc[..., None, :]
    decay = jnp.where(causal, jnp.exp(jnp.where(causal, diff, 0.0)), 0.0)
    kb = k * beta[..., None]
    l_mat = jnp.where(strict, jnp.einsum('nbhid,nbhjd->nbhij', kb, k) * decay, 0.0)
    rhs = jnp.concatenate([v * beta[..., None], kb * jnp.exp(gc)[..., None]], axis=-1)
    sol = lax.linalg.triangular_solve(l_mat, rhs, left_side=True, lower=True, unit_diagonal=True)
    u, w = sol[..., :dv], sol[..., dv:]
    attn = jnp.einsum('nbhid,nbhjd->nbhij', q, k) * decay
    q_dec = q * jnp.exp(gc)[..., None]
    k_out = k * jnp.exp(gc[..., -1:] - gc)[..., None]
    g_last = jnp.exp(gc[..., -1])[..., None, None]

    def step(s, xs):
        u_c, w_c, attn_c, q_c, k_c, gl_c = xs
        v_new = u_c - jnp.einsum('bhid,bhdv->bhiv', w_c, s)
        o = jnp.einsum('bhid,bhdv->bhiv', q_c, s) + jnp.einsum('bhij,bhjv->bhiv', attn_c, v_new)
        s = s * gl_c + jnp.einsum('bhjd,bhjv->bhdv', k_c, v_new)
        return s, o

    s, o = lax.scan(step, s0.astype(F32), (u, w, attn, q_dec, k_out, g_last))
    return from_chunks(o, t), s


def gla_mixer(h, s0, w_in, w_gk2, b_gk2, norm_w, w_out):
    b, t, _ = h.shape
    q, k, v, g, gk_low = jnp.split(h @ w_in, [GLA_DK, 2 * GLA_DK, 2 * GLA_DK + GLA_DV, 2 * GLA_DK + 2 * GLA_DV], axis=-1)
    log_a = jax.nn.log_sigmoid((gk_low @ w_gk2 + b_gk2).astype(F32)) / GLA_GATE_NORMALIZER
    heads = lambda z, d: z.reshape(b, t, GLA_HEADS, d)
    o, s = gla_chunked(heads(q, GLA_HK) * GLA_HK ** -0.5, heads(k, GLA_HK), heads(v, GLA_HV), heads(log_a, GLA_HK), s0)
    o = rmsnorm(o.astype(h.dtype), norm_w) * jax.nn.silu(heads(g, GLA_HV))
    return o.reshape(b, t, GLA_DV) @ w_out, s.astype(h.dtype)


def delta_mixer(h, s0, conv0, w_in, conv_w, a_log, dt_bias, norm_w, w_out):
    b, t, _ = h.shape
    qkv, z, a, beta_logit = jnp.split(h @ w_in, [DN_CONV_DIM, DN_CONV_DIM + DN_HEADS * DN_HV, DN_CONV_DIM + DN_HEADS * DN_HV + DN_HEADS], axis=-1)
    xpad = jnp.concatenate([conv0.astype(qkv.dtype), qkv], axis=1)
    conv = xpad[:, 0:t] * conv_w[0]
    for w in range(1, DN_CONV):
        conv = conv + xpad[:, w:w + t] * conv_w[w]
    qkv_c = jax.nn.silu(conv)
    new_conv = xpad[:, -(DN_CONV - 1):]
    q, k, v = jnp.split(qkv_c, [DN_HEADS * DN_HK, 2 * DN_HEADS * DN_HK], axis=-1)
    q = l2norm(q.reshape(b, t, DN_HEADS, DN_HK)) * DN_HK ** -0.5
    k = l2norm(k.reshape(b, t, DN_HEADS, DN_HK))
    v = v.reshape(b, t, DN_HEADS, DN_HV)
    beta = jax.nn.sigmoid(beta_logit.astype(F32))
    g = -jnp.exp(a_log.astype(F32)) * jax.nn.softplus(a.astype(F32) + dt_bias.astype(F32))
    o, s = gated_delta_chunked(q, k, v, beta, g, s0)
    o = rmsnorm(o.astype(h.dtype), norm_w) * jax.nn.silu(z.reshape(b, t, DN_HEADS, DN_HV))
    return o.reshape(b, t, DN_HEADS * DN_HV) @ w_out, s.astype(h.dtype), new_conv


def mla_project(h, pos, w_in, q_norm_w, w_uq, kv_norm_w, w_uk):
    b, t, _ = h.shape
    c_q, c_kv, k_r = jnp.split(h @ w_in, [MLA_Q_LORA, MLA_Q_LORA + MLA_KV_LORA], axis=-1)
    q = (rmsnorm(c_q, q_norm_w) @ w_uq).reshape(b, t, MLA_HEADS, MLA_NOPE + MLA_ROPE)
    q_lat = jnp.einsum('bthn,chn->bthc', q[..., :MLA_NOPE], w_uk)
    q_rope = rope(q[..., MLA_NOPE:], pos)
    c_kv = rmsnorm(c_kv, kv_norm_w)
    k_r = rope(k_r[:, :, None, :], pos)[:, :, 0]
    return q_lat, q_rope, c_kv, k_r


def mla_attend(q_lat, q_rope, c_keys, r_keys, q_pos, k_pos):
    s = (jnp.einsum('bqhc,bkc->bhqk', q_lat, c_keys) + jnp.einsum('bqhr,bkr->bhqk', q_rope, r_keys)).astype(F32) * MLA_SCALE
    s = jnp.where(k_pos[None, :] <= q_pos[:, None], s, -jnp.inf)
    p = jax.nn.softmax(s, axis=-1).astype(c_keys.dtype)
    return jnp.einsum('bhqk,bkc->bqhc', p, c_keys)


def mla_prompt_attend(q_lat, q_rope, c_kv, k_r, pos):
    b, t = q_lat.shape[:2]
    nq = t // Q_BLOCK
    blk = lambda z: jnp.moveaxis(z.reshape((b, nq, Q_BLOCK) + z.shape[2:]), 1, 0)
    o = lax.map(lambda xs: mla_attend(xs[0], xs[1], c_kv, k_r, xs[2], pos), (blk(q_lat), blk(q_rope), pos.reshape(nq, Q_BLOCK)))
    return jnp.moveaxis(o, 0, 1).reshape(b, t, MLA_HEADS, MLA_KV_LORA)


def mla_output(o_lat, w_uv, w_out):
    b, t = o_lat.shape[:2]
    return jnp.einsum('bthc,chv->bthv', o_lat, w_uv).reshape(b, t, MLA_HEADS * MLA_V) @ w_out


def swa_project(h, w_qkv, b_qkv):
    b, t, _ = h.shape
    q, k, v = jnp.split(h @ w_qkv + b_qkv, [SWA_HEADS * SWA_HD, (SWA_HEADS + SWA_KV_HEADS) * SWA_HD], axis=-1)
    return (q.reshape(b, t, SWA_KV_HEADS, SWA_GROUPS, SWA_HD), k.reshape(b, t, SWA_KV_HEADS, SWA_HD), v.reshape(b, t, SWA_KV_HEADS, SWA_HD))


def sink_softmax(s, sink, mask):
    s = jnp.where(mask, s, -jnp.inf)
    m = jnp.maximum(jnp.max(s, axis=-1, keepdims=True), sink)
    e = jnp.exp(s - m)
    return e / (jnp.sum(e, axis=-1, keepdims=True) + jnp.exp(sink - m))


def swa_prompt_attend(q, k, v, sinks):
    b, t = q.shape[:2]
    nb = t // WINDOW
    qb = q.reshape(b, nb, WINDOW, SWA_KV_HEADS, SWA_GROUPS, SWA_HD)
    kb = k.reshape(b, nb, WINDOW, SWA_KV_HEADS, SWA_HD)
    vb = v.reshape(b, nb, WINDOW, SWA_KV_HEADS, SWA_HD)
    shift = lambda z: jnp.pad(z, ((0, 0), (1, 0), (0, 0), (0, 0), (0, 0)))[:, :-1]
    k2 = jnp.concatenate([shift(kb), kb], axis=2)
    v2 = jnp.concatenate([shift(vb), vb], axis=2)
    s = jnp.einsum('bnqkgd,bnckd->bnkgqc', qb, k2).astype(F32) * SWA_SCALE
    r = jnp.arange(WINDOW)[:, None]
    c = jnp.arange(2 * WINDOW)[None, :]
    band = (c >= r) & (c <= r + WINDOW)
    not_first = jnp.arange(nb)[:, None, None] > 0
    mask = (band[None] & (not_first | (c >= WINDOW)[None]))[None, :, None, None]
    p = sink_softmax(s, sinks.astype(F32).reshape(1, 1, SWA_KV_HEADS, SWA_GROUPS, 1, 1), mask).astype(v.dtype)
    o = jnp.einsum('bnkgqc,bnckd->bnqkgd', p, v2)
    return o.reshape(b, t, SWA_HEADS * SWA_HD)


def swa_sample_attend(q, k_all, v_all, sinks):
    b, tn = q.shape[:2]
    n_buf = k_all.shape[1] - tn
    s = jnp.einsum('bqkgd,bckd->bkgqc', q, k_all).astype(F32) * SWA_SCALE
    q_pos = n_buf + jnp.arange(tn)[:, None]
    k_pos = jnp.arange(k_all.shape[1])[None, :]
    mask = (k_pos <= q_pos) & (k_pos >= q_pos - WINDOW)
    p = sink_softmax(s, sinks.astype(F32).reshape(1, SWA_KV_HEADS, SWA_GROUPS, 1, 1), mask).astype(v_all.dtype)
    o = jnp.einsum('bkgqc,bckd->bqkgd', p, v_all)
    return o.reshape(b, tn, SWA_HEADS * SWA_HD)


def swiglu(h, wg, wu, wd):
    return (jax.nn.silu(h @ wg) * (h @ wu)) @ wd


def moe_ffn(h, w_router, w_gate, w_up, w_down):
    logits = (h @ w_router).astype(F32)
    top_v, top_i = lax.top_k(logits, TOP_K)
    gates = jax.nn.softmax(top_v, axis=-1)
    combine = jnp.sum(jax.nn.one_hot(top_i, N_EXPERTS, dtype=F32) * gates[..., None], axis=-2).astype(h.dtype)
    out = jnp.zeros_like(h)
    for e in range(N_EXPERTS):
        out = out + combine[..., e:e + 1] * swiglu(h, w_gate[e], w_up[e], w_down[e])
    return out


def setup_inputs(seed: int = 0) -> dict:
    key = jax.random.key(seed)
    ks = iter(jax.random.split(key, 64))

    def nrm(shape, scale=1.0):
        return scale * jax.random.normal(next(ks), shape, F32)

    def gain(shape):
        return 1.0 + 0.01 * jax.random.normal(next(ks), shape, F32)

    n_pages = PAST_LEN // PAGE_SIZE
    n_pool = (DEC_BATCH * n_pages * 5) // 4
    d = D_MODEL
    inputs = {
        'x_prompt': nrm((BATCH, SEQ, d)),
        'x_sample': nrm((DEC_BATCH, DEC_SEQ, d)),
        'state_gla': nrm((N_LAYERS_A, DEC_BATCH, GLA_HEADS, GLA_HK, GLA_HV), 0.1),
        'cache_mla_latent': nrm((N_LAYERS_B, n_pool, PAGE_SIZE, MLA_KV_LORA)),
        'cache_mla_krope': nrm((N_LAYERS_B, n_pool, PAGE_SIZE, MLA_ROPE)),
        'cache_swa_k': nrm((N_LAYERS_C, DEC_BATCH, WINDOW, SWA_KV_HEADS, SWA_HD)),
        'cache_swa_v': nrm((N_LAYERS_C, DEC_BATCH, WINDOW, SWA_KV_HEADS, SWA_HD)),
        'state_delta': nrm((N_LAYERS_D, DEC_BATCH, DN_HEADS, DN_HK, DN_HV), 0.1),
        'state_delta_conv': nrm((N_LAYERS_D, DEC_BATCH, DN_CONV - 1, DN_CONV_DIM)),
        'page_table': jax.random.permutation(next(ks), n_pool)[:DEC_BATCH * n_pages].reshape(DEC_BATCH, n_pages).astype(jnp.int32),
        'norm_w': gain((DEPTH, 2, d)),
        'final_norm_w': gain((d,)),
        'gla_w_in': nrm((N_LAYERS_A, d, GLA_IN), d ** -0.5),
        'gla_w_gk2': nrm((N_LAYERS_A, GLA_GATE_RANK, GLA_DK), GLA_GATE_RANK ** -0.5),
        'gla_b_gk2': nrm((N_LAYERS_A, GLA_DK), 0.1),
        'gla_norm_w': gain((N_LAYERS_A, GLA_HV)),
        'gla_w_out': nrm((N_LAYERS_A, GLA_DV, d), GLA_DV ** -0.5 * RES_SCALE),
        'mla_w_in': nrm((N_LAYERS_B, d, MLA_IN), d ** -0.5),
        'mla_q_norm_w': gain((N_LAYERS_B, MLA_Q_LORA)),
        'mla_w_uq': nrm((N_LAYERS_B, MLA_Q_LORA, MLA_HEADS * (MLA_NOPE + MLA_ROPE)), MLA_Q_LORA ** -0.5),
        'mla_kv_norm_w': gain((N_LAYERS_B, MLA_KV_LORA)),
        'mla_w_uk': nrm((N_LAYERS_B, MLA_KV_LORA, MLA_HEADS, MLA_NOPE), MLA_KV_LORA ** -0.5),
        'mla_w_uv': nrm((N_LAYERS_B, MLA_KV_LORA, MLA_HEADS, MLA_V), MLA_KV_LORA ** -0.5),
        'mla_w_out': nrm((N_LAYERS_B, MLA_HEADS * MLA_V, d), (MLA_HEADS * MLA_V) ** -0.5 * RES_SCALE),
        'swa_w_qkv': nrm((N_LAYERS_C, d, SWA_IN), d ** -0.5),
        'swa_b_qkv': nrm((N_LAYERS_C, SWA_IN), 0.02),
        'swa_sinks': nrm((N_LAYERS_C, SWA_HEADS), 0.5),
        'swa_w_out': nrm((N_LAYERS_C, SWA_HEADS * SWA_HD, d), (SWA_HEADS * SWA_HD) ** -0.5 * RES_SCALE),
        'swa_b_out': nrm((N_LAYERS_C, d), 0.02),
        'dn_w_in': nrm((N_LAYERS_D, d, DN_IN), d ** -0.5),
        'dn_conv_w': nrm((N_LAYERS_D, DN_CONV, DN_CONV_DIM), DN_CONV ** -0.5),
        'dn_a_log': jnp.log(jax.random.uniform(next(ks), (N_LAYERS_D, DN_HEADS), F32, 1.0, 16.0)),
        'dn_dt_bias': None,
        'dn_norm_w': gain((N_LAYERS_D, DN_HV)),
        'dn_w_out': nrm((N_LAYERS_D, DN_HEADS * DN_HV, d), (DN_HEADS * DN_HV) ** -0.5 * RES_SCALE),
        'ffn_w_gate': nrm((N_DENSE, d, D_FF), d ** -0.5),
        'ffn_w_up': nrm((N_DENSE, d, D_FF), d ** -0.5),
        'ffn_w_down': nrm((N_DENSE, D_FF, d), D_FF ** -0.5 * RES_SCALE),
        'moe_w_router': nrm((N_MOE, d, N_EXPERTS), d ** -0.5),
        'moe_w_gate': nrm((N_MOE, N_EXPERTS, d, D_FF), d ** -0.5),
        'moe_w_up': nrm((N_MOE, N_EXPERTS, d, D_FF), d ** -0.5),
        'moe_w_down': nrm((N_MOE, N_EXPERTS, D_FF, d), D_FF ** -0.5 * RES_SCALE),
    }
    dt = jnp.exp(jax.random.uniform(next(ks), (N_LAYERS_D, DN_HEADS), F32, math.log(1e-3), math.log(1e-1)))
    inputs['dn_dt_bias'] = dt + jnp.log(-jnp.expm1(-dt))
    return inputs


def reference(x_prompt, x_sample, state_gla, cache_mla_latent, cache_mla_krope, cache_swa_k, cache_swa_v, state_delta, state_delta_conv, page_table, norm_w, final_norm_w, gla_w_in, gla_w_gk2, gla_b_gk2, gla_norm_w, gla_w_out, mla_w_in, mla_q_norm_w, mla_w_uq, mla_kv_norm_w, mla_w_uk, mla_w_uv, mla_w_out, swa_w_qkv, swa_b_qkv, swa_sinks, swa_w_out, swa_b_out, dn_w_in, dn_conv_w, dn_a_log, dn_dt_bias, dn_norm_w, dn_w_out, ffn_w_gate, ffn_w_up, ffn_w_down, moe_w_router, moe_w_gate, moe_w_up, moe_w_down):
    xp, xs = x_prompt, x_sample
    b_p, t_p = xp.shape[:2]
    b_s, t_s = xs.shape[:2]
    past_len = page_table.shape[1] * cache_mla_latent.shape[2]
    pos_p = jnp.arange(t_p)
    pos_s = past_len + jnp.arange(t_s)
    gla_p, gla_s, lat_p, lat_s, kr_p, kr_s = [], [], [], [], [], []
    swk_p, swk_s, swv_p, swv_s, dn_p, dn_s, cv_p, cv_s = [], [], [], [], [], [], [], []
    for i in range(DEPTH):
        kind, j = i % N_MIXERS, i // N_MIXERS
        hp = rmsnorm(xp, norm_w[i, 0])
        hs = rmsnorm(xs, norm_w[i, 0])
        if kind == 0:
            w = (gla_w_in[j], gla_w_gk2[j], gla_b_gk2[j], gla_norm_w[j], gla_w_out[j])
            mp, sp = gla_mixer(hp, jnp.zeros((b_p, GLA_HEADS, GLA_HK, GLA_HV), hp.dtype), *w)
            ms, ss = gla_mixer(hs, state_gla[j], *w)
            gla_p.append(sp)
            gla_s.append(ss)
        elif kind == 1:
            w = (mla_w_in[j], mla_q_norm_w[j], mla_w_uq[j], mla_kv_norm_w[j], mla_w_uk[j])
            q_lat, q_rope, c_kv, k_r = mla_project(hp, pos_p, *w)
            mp = mla_output(mla_prompt_attend(q_lat, q_rope, c_kv, k_r, pos_p), mla_w_uv[j], mla_w_out[j])
            lat_p.append(c_kv)
            kr_p.append(k_r)
            q_lat, q_rope, c_kv, k_r = mla_project(hs, pos_s, *w)
            past_c = cache_mla_latent[j, page_table].reshape(b_s, past_len, MLA_KV_LORA).astype(c_kv.dtype)
            past_r = cache_mla_krope[j, page_table].reshape(b_s, past_len, MLA_ROPE).astype(k_r.dtype)
            keys_c = jnp.concatenate([past_c, c_kv], axis=1)
            keys_r = jnp.concatenate([past_r, k_r], axis=1)
            o_lat = mla_attend(q_lat, q_rope, keys_c, keys_r, pos_s, jnp.arange(past_len + t_s))
            ms = mla_output(o_lat, mla_w_uv[j], mla_w_out[j])
            lat_s.append(c_kv)
            kr_s.append(k_r)
        elif kind == 2:
            q, k, v = swa_project(hp, swa_w_qkv[j], swa_b_qkv[j])
            mp = swa_prompt_attend(q, k, v, swa_sinks[j]) @ swa_w_out[j] + swa_b_out[j]
            swk_p.append(k[:, -WINDOW:])
            swv_p.append(v[:, -WINDOW:])
            q, k, v = swa_project(hs, swa_w_qkv[j], swa_b_qkv[j])
            k_all = jnp.concatenate([cache_swa_k[j].astype(k.dtype), k], axis=1)
            v_all = jnp.concatenate([cache_swa_v[j].astype(v.dtype), v], axis=1)
            ms = swa_sample_attend(q, k_all, v_all, swa_sinks[j]) @ swa_w_out[j] + swa_b_out[j]
            swk_s.append(k_all[:, -WINDOW:])
            swv_s.append(v_all[:, -WINDOW:])
        else:
            w = (dn_w_in[j], dn_conv_w[j], dn_a_log[j], dn_dt_bias[j], dn_norm_w[j], dn_w_out[j])
            mp, sp, cp = delta_mixer(hp, jnp.zeros((b_p, DN_HEADS, DN_HK, DN_HV), hp.dtype), jnp.zeros((b_p, DN_CONV - 1, DN_CONV_DIM), hp.dtype), *w)
            ms, ss, cs = delta_mixer(hs, state_delta[j], state_delta_conv[j], *w)
            dn_p.append(sp)
            dn_s.append(ss)
            cv_p.append(cp)
            cv_s.append(cs)
        xp = xp + mp
        xs = xs + ms
        hp = rmsnorm(xp, norm_w[i, 1])
        hs = rmsnorm(xs, norm_w[i, 1])
        jf = i // 2
        if i % 2 == 0:
            xp = xp + swiglu(hp, ffn_w_gate[jf], ffn_w_up[jf], ffn_w_down[jf])
            xs = xs + swiglu(hs, ffn_w_gate[jf], ffn_w_up[jf], ffn_w_down[jf])
        else:
            xp = xp + moe_ffn(hp, moe_w_router[jf], moe_w_gate[jf], moe_w_up[jf], moe_w_down[jf])
            xs = xs + moe_ffn(hs, moe_w_router[jf], moe_w_gate[jf], moe_w_up[jf], moe_w_down[jf])
    y_prompt = rmsnorm(xp, final_norm_w)
    y_sample = rmsnorm(xs, final_norm_w)
    return (y_prompt, y_sample, jnp.stack(gla_p), jnp.stack(gla_s), jnp.stack(lat_p), jnp.stack(lat_s), jnp.stack(kr_p), jnp.stack(kr_s), jnp.stack(swk_p), jnp.stack(swk_s), jnp.stack(swv_p), jnp.stack(swv_s), jnp.stack(dn_p), jnp.stack(dn_s), jnp.stack(cv_p), jnp.stack(cv_s))
```

```python
import functools
import math

import jax
import jax.numpy as jnp
from jax import lax
from jax.experimental import pallas as pl
from jax.experimental.pallas import tpu as pltpu

F32 = jnp.float32
BF16 = jnp.bfloat16

NORM_EPS = 1e-6
GLA_HEADS = 4
GLA_GATE_RANK = 16
GLA_GATE_NORMALIZER = 16.0
CHUNK = 64
MLA_HEADS = 16
MLA_Q_LORA = 384
MLA_KV_LORA = 256
MLA_NOPE = 64
MLA_ROPE = 32
MLA_V = 64
ROPE_THETA = 10000.0
SWA_HEADS = 16
SWA_KV_HEADS = 4
SWA_HD = 64
WINDOW = 128
DN_HEADS = 8
DN_HK = 128
DN_HV = 128
DN_CONV = 4
N_EXPERTS = 8

LANES = 128
VMEM_LIMIT = 56 * 1024 * 1024
NEG_INF = float("-inf")


def _cparams(*sem):
    return pltpu.CompilerParams(dimension_semantics=sem, vmem_limit_bytes=VMEM_LIMIT)


def _bdot(a, b):
    return jnp.dot(a.astype(BF16), b.astype(BF16), preferred_element_type=F32)


def _bdot_nt(a, b):
    return lax.dot_general(a.astype(BF16), b.astype(BF16), (((1,), (1,)), ((), ())),
                           preferred_element_type=F32)


def _bdot_tn(a, b):
    return lax.dot_general(a.astype(BF16), b.astype(BF16), (((0,), (0,)), ((), ())),
                           preferred_element_type=F32)


def _split3(x):
    h1 = x.astype(BF16)
    r1 = x - h1.astype(F32)
    h2 = r1.astype(BF16)
    h3 = (r1 - h2.astype(F32)).astype(BF16)
    return h1, h2, h3


def _dot_exact_lhs(m, x):
    mb = m.astype(BF16)
    h1, h2, h3 = _split3(x)
    return (jnp.dot(mb, h1, preferred_element_type=F32) + jnp.dot(mb, h2, preferred_element_type=F32)
            + jnp.dot(mb, h3, preferred_element_type=F32))


def _rms(x, w):
    return x * lax.rsqrt(jnp.mean(x * x, axis=-1, keepdims=True) + NORM_EPS) * w


def _silu(x):
    return x / (1.0 + jnp.exp(-x))


def _log_sigmoid(x):
    return jnp.minimum(x, 0.0) - jnp.log(1.0 + jnp.exp(-jnp.abs(x)))


def _softplus(x):
    return jnp.maximum(x, 0.0) + jnp.log(1.0 + jnp.exp(-jnp.abs(x)))


def _linear_kernel(*refs, has_norm, has_bias, has_res):
    it = iter(refs)
    x_ref = next(it)
    nw_ref = next(it) if has_norm else None
    w_ref = next(it)
    b_ref = next(it) if has_bias else None
    r_ref = next(it) if has_res else None
    o_ref = next(it)
    h_ref = next(it)

    @pl.when(pl.program_id(1) == 0)
    def _():
        xv = x_ref[...].astype(F32)
        if has_norm:
            xv = _rms(xv, nw_ref[...])
        h_ref[...] = xv.astype(BF16)

    acc = jnp.dot(h_ref[...], w_ref[...].astype(BF16), preferred_element_type=F32)
    if has_bias:
        acc = acc + b_ref[...]
    if has_res:
        acc = acc + r_ref[...]
    o_ref[...] = acc.astype(o_ref.dtype)


def _linear(x, w, *, norm_w=None, bias=None, res=None, tm, tn, out_dtype=F32, name="linear"):
    t, k = x.shape
    n = w.shape[1]
    assert t % tm == 0 and n % tn == 0, (t, tm, n, tn)
    in_specs = [pl.BlockSpec((tm, k), lambda i, j: (i, 0))]
    args = [x]
    if norm_w is not None:
        in_specs.append(pl.BlockSpec((1, k), lambda i, j: (0, 0)))
        args.append(norm_w.reshape(1, k))
    in_specs.append(pl.BlockSpec((k, tn), lambda i, j: (0, j)))
    args.append(w)
    if bias is not None:
        in_specs.append(pl.BlockSpec((1, tn), lambda i, j: (0, j)))
        args.append(bias.reshape(1, n))
    if res is not None:
        in_specs.append(pl.BlockSpec((tm, tn), lambda i, j: (i, j)))
        args.append(res)
    kern = functools.partial(_linear_kernel, has_norm=norm_w is not None, has_bias=bias is not None,
                             has_res=res is not None)
    return pl.pallas_call(
        kern,
        grid=(t // tm, n // tn),
        in_specs=in_specs,
        out_specs=pl.BlockSpec((tm, tn), lambda i, j: (i, j)),
        out_shape=jax.ShapeDtypeStruct((t, n), out_dtype),
        scratch_shapes=[pltpu.VMEM((tm, k), BF16)],
        compiler_params=_cparams("parallel", "arbitrary"),
        name=name,
    )(*args)


def _ffn_kernel(*refs, n_exp, final_norm):
    it = iter(refs)
    x_ref = next(it)
    nw_ref = next(it)
    c_ref = next(it) if n_exp > 1 else None
    wg_ref = next(it)
    wu_ref = next(it)
    wd_ref = next(it)
    fw_ref = next(it) if final_norm else None
    o_ref = next(it)
    h_ref = next(it)
    acc_ref = next(it)
    e = pl.program_id(1)
    f = pl.program_id(2)

    @pl.when((e == 0) & (f == 0))
    def _():
        h_ref[...] = _rms(x_ref[...], nw_ref[...]).astype(BF16)
        acc_ref[...] = jnp.zeros_like(acc_ref)

    h = h_ref[...]
    g = jnp.dot(h, wg_ref[...].astype(BF16), preferred_element_type=F32)
    u = jnp.dot(h, wu_ref[...].astype(BF16), preferred_element_type=F32)
    a = _silu(g) * u
    if n_exp > 1:
        lane = lax.broadcasted_iota(jnp.int32, c_ref.shape, 1)
        a = a * jnp.sum(jnp.where(lane == e, c_ref[...], 0.0), axis=-1, keepdims=True)
    acc_ref[...] += jnp.dot(a.astype(BF16), wd_ref[...].astype(BF16), preferred_element_type=F32)

    @pl.when((e == n_exp - 1) & (f == pl.num_programs(2) - 1))
    def _():
        y = x_ref[...] + acc_ref[...]
        if final_norm:
            y = _rms(y, fw_ref[...])
        o_ref[...] = y


def _ffn(x, norm_w, wg, wu, wd, *, combine=None, final_w=None, tm, tf):
    t, d = x.shape
    n_exp, _, ff = wg.shape
    assert t % tm == 0 and ff % tf == 0
    in_specs = [pl.BlockSpec((tm, d), lambda i, e, f: (i, 0)),
                pl.BlockSpec((1, d), lambda i, e, f: (0, 0))]
    args = [x, norm_w.reshape(1, d)]
    if n_exp > 1:
        in_specs.append(pl.BlockSpec((tm, LANES), lambda i, e, f: (i, 0)))
        args.append(combine)
    in_specs += [pl.BlockSpec((None, d, tf), lambda i, e, f: (e, 0, f)),
                 pl.BlockSpec((None, d, tf), lambda i, e, f: (e, 0, f)),
                 pl.BlockSpec((None, tf, d), lambda i, e, f: (e, f, 0))]
    args += [wg, wu, wd]
    if final_w is not None:
        in_specs.append(pl.BlockSpec((1, d), lambda i, e, f: (0, 0)))
        args.append(final_w.reshape(1, d))
    kern = functools.partial(_ffn_kernel, n_exp=n_exp, final_norm=final_w is not None)
    return pl.pallas_call(
        kern,
        grid=(t // tm, n_exp, ff // tf),
        in_specs=in_specs,
        out_specs=pl.BlockSpec((tm, d), lambda i, e, f: (i, 0)),
        out_shape=jax.ShapeDtypeStruct((t, d), F32),
        scratch_shapes=[pltpu.VMEM((tm, d), BF16), pltpu.VMEM((tm, d), F32)],
        compiler_params=_cparams("parallel", "arbitrary", "arbitrary"),
        name="ffn" if n_exp == 1 else "moe_dense",
    )(*args)


def _router_kernel(x_ref, nw_ref, wr_ref, c_ref, *, n_exp):
    h = _rms(x_ref[...], nw_ref[...])
    logits = jnp.dot(h, wr_ref[...], preferred_element_type=F32, precision=lax.Precision.HIGHEST)
    lane = lax.broadcasted_iota(jnp.int32, logits.shape, 1)
    logits = jnp.where(lane < n_exp, logits, NEG_INF)
    m1 = jnp.max(logits, axis=-1, keepdims=True)
    i1 = jnp.min(jnp.where(logits == m1, lane, LANES), axis=-1, keepdims=True)
    rest = jnp.where(lane == i1, NEG_INF, logits)
    m2 = jnp.max(rest, axis=-1, keepdims=True)
    i2 = jnp.min(jnp.where(rest == m2, lane, LANES), axis=-1, keepdims=True)
    e2 = jnp.exp(m2 - m1)
    g1 = 1.0 / (1.0 + e2)
    g2 = e2 / (1.0 + e2)
    c_ref[...] = jnp.where(lane == i1, g1, 0.0) + jnp.where(lane == i2, g2, 0.0)


def _router(x, norm_w, w_router, *, tm):
    t, d = x.shape
    n_exp = w_router.shape[1]
    wr = jnp.pad(w_router, ((0, 0), (0, LANES - n_exp)))
    return pl.pallas_call(
        functools.partial(_router_kernel, n_exp=n_exp),
        grid=(t // tm,),
        in_specs=[pl.BlockSpec((tm, d), lambda i: (i, 0)),
                  pl.BlockSpec((1, d), lambda i: (0, 0)),
                  pl.BlockSpec((d, LANES), lambda i: (0, 0))],
        out_specs=pl.BlockSpec((tm, LANES), lambda i: (i, 0)),
        out_shape=jax.ShapeDtypeStruct((t, LANES), F32),
        compiler_params=_cparams("parallel"),
        name="router",
    )(x, norm_w.reshape(1, d), wr)


def _pad_rows(x, rows):
    if x.shape[0] == rows:
        return x
    return jnp.concatenate([x, jnp.zeros((rows - x.shape[0], x.shape[1]), x.dtype)], axis=0)


def _chunk_masks(rows):
    ri = lax.broadcasted_iota(jnp.int32, (rows, rows), 0)
    ci = lax.broadcasted_iota(jnp.int32, (rows, rows), 1)
    same = (ri // CHUNK) == (ci // CHUNK)
    return same & (ci <= ri), same


def _gla_kernel(*refs, rb, zero_init, has_alias):
    it = iter(refs)
    q_ref, k_ref, v_ref, g_ref, gk_ref = next(it), next(it), next(it), next(it), next(it)
    s0_ref = None if zero_init else next(it)
    wgk_ref, bgk_ref, nw_ref = next(it), next(it), next(it)
    if has_alias:
        next(it)
    o_ref, s_ref, st_ref = next(it), next(it), next(it)
    r = pl.program_id(1)
    rows = max(rb, CHUNK)
    n_chunks = rows // CHUNK
    hk = q_ref.shape[1] // GLA_HEADS
    hv = v_ref.shape[1] // GLA_HEADS

    @pl.when(r == 0)
    def _():
        for h in range(GLA_HEADS):
            if zero_init:
                st_ref[h] = jnp.zeros(st_ref.shape[1:], F32)
            else:
                st_ref[h] = s0_ref[0, h].T

    q = _pad_rows(q_ref[...] * hk ** -0.5, rows)
    k = _pad_rows(k_ref[...], rows)
    v = _pad_rows(v_ref[...], rows)
    la = _log_sigmoid(_bdot(gk_ref[...], wgk_ref[...]) + bgk_ref[...]) * (1.0 / GLA_GATE_NORMALIZER)
    la = _pad_rows(la, rows)
    causal, same = _chunk_masks(rows)
    gc = _dot_exact_lhs(causal, la)
    gt = _dot_exact_lhs(same, la)
    q_in = q * jnp.exp(gc)
    k_in = k * jnp.exp(-gc)
    k_out = k * jnp.exp(gt - gc)
    e_tot = jnp.exp(gt)
    tri = causal[:CHUNK, :CHUNK]
    for h in range(GLA_HEADS):
        st = st_ref[h]
        ks = slice(h * hk, (h + 1) * hk)
        vs = slice(h * hv, (h + 1) * hv)
        for c in range(n_chunks):
            rs = slice(c * CHUNK, (c + 1) * CHUNK)
            qi, ki, ko, vh = q_in[rs, ks], k_in[rs, ks], k_out[rs, ks], v[rs, vs]
            intra = jnp.where(tri, _bdot_nt(qi, ki), 0.0)
            o = _bdot(intra, vh) + _bdot_nt(qi, st)
            st = st * e_tot[c * CHUNK:c * CHUNK + 1, ks] + _bdot_tn(vh, ko)
            n_out = min(rb, CHUNK)
            og = _rms(o[:n_out], nw_ref[...]) * _silu(g_ref[c * CHUNK:c * CHUNK + n_out, vs])
            o_ref[c * CHUNK:c * CHUNK + n_out, vs] = og
        st_ref[h] = st

    @pl.when(r == pl.num_programs(1) - 1)
    def _():
        for h in range(GLA_HEADS):
            s_ref[0, h] = st_ref[h].T


def _gla_core(proj, s0, w_gk2, b_gk2, norm_w, o_full, *, n_seq, seq_len, row0, rb):
    t = proj.shape[0]
    dk = w_gk2.shape[1]
    dv = 2 * dk
    hk, hv = dk // GLA_HEADS, dv // GLA_HEADS
    nblk = seq_len // rb
    base = row0 // rb
    assert row0 % rb == 0 and seq_len % rb == 0

    def rowmap(col):
        return lambda b, r: (base + b * nblk + r, col)

    in_specs = [pl.BlockSpec((rb, dk), rowmap(0)), pl.BlockSpec((rb, dk), rowmap(1)),
                pl.BlockSpec((rb, dv), rowmap(1)), pl.BlockSpec((rb, dv), rowmap(2)),
                pl.BlockSpec((rb, LANES), rowmap((2 * dk + 2 * dv) // LANES))]
    args = [proj, proj, proj, proj, proj]
    if s0 is not None:
        in_specs.append(pl.BlockSpec((1, GLA_HEADS, hk, hv), lambda b, r: (b, 0, 0, 0)))
        args.append(s0)
    wgk = jnp.pad(w_gk2, ((0, LANES - w_gk2.shape[0]), (0, 0)))
    in_specs += [pl.BlockSpec((LANES, dk), lambda b, r: (0, 0)),
                 pl.BlockSpec((1, dk), lambda b, r: (0, 0)),
                 pl.BlockSpec((1, hv), lambda b, r: (0, 0))]
    args += [wgk, b_gk2.reshape(1, dk), norm_w.reshape(1, hv)]
    aliases = {}
    if o_full is not None:
        in_specs.append(pl.BlockSpec(memory_space=pl.ANY))
        aliases = {len(args): 0}
        args.append(o_full)
    kern = functools.partial(_gla_kernel, rb=rb, zero_init=s0 is None, has_alias=o_full is not None)
    return pl.pallas_call(
        kern,
        grid=(n_seq, nblk),
        in_specs=in_specs,
        out_specs=[pl.BlockSpec((rb, dv), rowmap(0)),
                   pl.BlockSpec((1, GLA_HEADS, hk, hv), lambda b, r: (b, 0, 0, 0))],
        out_shape=[jax.ShapeDtypeStruct((t, dv), F32),
                   jax.ShapeDtypeStruct((n_seq, GLA_HEADS, hk, hv), F32)],
        scratch_shapes=[pltpu.VMEM((GLA_HEADS, hv, hk), F32)],
        input_output_aliases=aliases,
        compiler_params=_cparams("parallel", "arbitrary"),
        name="gla_core",
    )(*args)


def _tile(n, pref):
    if n <= pref:
        return n
    for c in range(pref, 7, -8):
        if n % c == 0:
            return c
    return n


def _pad_cols(w, n):
    return jnp.pad(w, ((0, 0), (0, n - w.shape[1])))


def _gla_layer(x, nw, s0_s, w_in, w_gk2, b_gk2, norm_w, w_out, dims):
    n_p, t_p, n_s, t_s = dims
    t = x.shape[0]
    tm = _tile(t, 1280)
    dk = w_gk2.shape[1]
    width = 6 * dk + LANES
    proj = _linear(x, _pad_cols(w_in, width), norm_w=nw, tm=tm, tn=_tile(width, 640), name="gla_in")
    o, sp = _gla_core(proj, None, w_gk2, b_gk2, norm_w, None, n_seq=n_p, seq_len=t_p, row0=0,
                      rb=min(t_p, 256))
    o, ss = _gla_core(proj, s0_s, w_gk2, b_gk2, norm_w, o, n_seq=n_s, seq_len=t_s, row0=n_p * t_p,
                      rb=t_s)
    x = _linear(o, w_out, res=x, tm=tm, tn=512, name="gla_out")
    return x, sp, ss


def _rope_tables(pos, half):
    freqs = jnp.exp(-math.log(ROPE_THETA) * jnp.arange(half, dtype=F32) / half)
    ang = pos.astype(F32)[:, None] * freqs[None, :]
    cos, sin = jnp.cos(ang), jnp.sin(ang)
    return jnp.concatenate([cos, cos], axis=-1), jnp.concatenate([-sin, sin], axis=-1)


def _swap_halves(w, axis=-1):
    a, b = jnp.split(w, 2, axis=axis)
    return jnp.concatenate([b, a], axis=axis)


def _mla_in_kernel(x_ref, nw_ref, w_ref, kvw_ref, cos_ref, sin_ref, cq_ref, ckv_ref, kr_ref):
    h = _rms(x_ref[...], nw_ref[...])
    y = _bdot(h, w_ref[...])
    cq_ref[...] = y[:, :MLA_Q_LORA]
    ckv_ref[...] = _rms(y[:, MLA_Q_LORA:MLA_Q_LORA + MLA_KV_LORA], kvw_ref[...])
    o = MLA_Q_LORA + MLA_KV_LORA
    kr_ref[...] = (y[:, o:o + MLA_ROPE] * cos_ref[...]
                   + y[:, o + LANES:o + LANES + MLA_ROPE] * sin_ref[...])


def _mla_in(x, nw, w_in, kv_norm_w, cos, sin, *, tm):
    t, d = x.shape
    o = MLA_Q_LORA + MLA_KV_LORA
    kr_w = w_in[:, o:o + MLA_ROPE]
    w_aug = jnp.concatenate([w_in[:, :o], _pad_cols(kr_w, LANES), _pad_cols(_swap_halves(kr_w), LANES)],
                            axis=1)
    wid = w_aug.shape[1]
    row = lambda i: (i, 0)
    fix = lambda i: (0, 0)
    return pl.pallas_call(
        _mla_in_kernel,
        grid=(t // tm,),
        in_specs=[pl.BlockSpec((tm, d), row), pl.BlockSpec((1, d), fix), pl.BlockSpec((d, wid), fix),
                  pl.BlockSpec((1, MLA_KV_LORA), fix), pl.BlockSpec((tm, MLA_ROPE), row),
                  pl.BlockSpec((tm, MLA_ROPE), row)],
        out_specs=[pl.BlockSpec((tm, MLA_Q_LORA), row), pl.BlockSpec((tm, MLA_KV_LORA), row),
                   pl.BlockSpec((tm, MLA_ROPE), row)],
        out_shape=[jax.ShapeDtypeStruct((t, MLA_Q_LORA), F32), jax.ShapeDtypeStruct((t, MLA_KV_LORA), F32),
                   jax.ShapeDtypeStruct((t, MLA_ROPE), F32)],
        compiler_params=_cparams("parallel"),
        name="mla_in",
    )(x, nw.reshape(1, d), w_aug, kv_norm_w.reshape(1, -1), cos, sin)


def _mla_q_kernel(cq_ref, qw_ref, wn_ref, wr_ref, ws_ref, wk_ref, cos_ref, sin_ref, ql_ref, qr_ref):
    cq = _rms(cq_ref[...], qw_ref[...]).astype(BF16)
    qn = jnp.dot(cq, wn_ref[...].astype(BF16), preferred_element_type=F32).astype(BF16)
    for j in range(MLA_HEADS // 2):
        ql = jnp.dot(qn[:, j * LANES:(j + 1) * LANES], wk_ref[j].astype(BF16), preferred_element_type=F32)
        ql_ref[2 * j] = ql[:, :MLA_KV_LORA]
        ql_ref[2 * j + 1] = ql[:, MLA_KV_LORA:]
    qr = jnp.dot(cq, wr_ref[...].astype(BF16), preferred_element_type=F32)
    qs = jnp.dot(cq, ws_ref[...].astype(BF16), preferred_element_type=F32)
    per = LANES // MLA_ROPE
    cos = jnp.concatenate([cos_ref[...]] * (MLA_HEADS // per), axis=-1)
    sin = jnp.concatenate([sin_ref[...]] * (MLA_HEADS // per), axis=-1)
    rot = qr * cos + qs * sin
    for h in range(MLA_HEADS):
        qr_ref[h] = rot[:, h * MLA_ROPE:(h + 1) * MLA_ROPE]


def _mla_q(cq, q_norm_w, w_uq, w_uk, cos, sin, *, tm):
    t = cq.shape[0]
    per = LANES // MLA_ROPE
    w3 = w_uq.reshape(MLA_Q_LORA, MLA_HEADS, MLA_NOPE + MLA_ROPE)
    w_nope = w3[:, :, :MLA_NOPE].reshape(MLA_Q_LORA, MLA_HEADS * MLA_NOPE)
    w_rope = w3[:, :, MLA_NOPE:].reshape(MLA_Q_LORA, MLA_HEADS * MLA_ROPE)
    w_swap = _swap_halves(w3[:, :, MLA_NOPE:]).reshape(MLA_Q_LORA, MLA_HEADS * MLA_ROPE)
    a = jnp.transpose(w_uk, (1, 2, 0))
    z = jnp.zeros_like(a[0::2])
    w_bd = jnp.concatenate([jnp.concatenate([a[0::2], z], axis=2),
                            jnp.concatenate([z, a[1::2]], axis=2)], axis=1)
    cos4 = jnp.tile(cos, (1, per))
    sin4 = jnp.tile(sin, (1, per))
    row = lambda i: (i, 0)
    fix2 = lambda i: (0, 0)
    return pl.pallas_call(
        _mla_q_kernel,
        grid=(t // tm,),
        in_specs=[pl.BlockSpec((tm, MLA_Q_LORA), row), pl.BlockSpec((1, MLA_Q_LORA), fix2),
                  pl.BlockSpec(w_nope.shape, fix2), pl.BlockSpec(w_rope.shape, fix2),
                  pl.BlockSpec(w_swap.shape, fix2), pl.BlockSpec(w_bd.shape, lambda i: (0, 0, 0)),
                  pl.BlockSpec((tm, LANES), row), pl.BlockSpec((tm, LANES), row)],
        out_specs=[pl.BlockSpec((MLA_HEADS, tm, MLA_KV_LORA), lambda i: (0, i, 0)),
                   pl.BlockSpec((MLA_HEADS, tm, MLA_ROPE), lambda i: (0, i, 0))],
        out_shape=[jax.ShapeDtypeStruct((MLA_HEADS, t, MLA_KV_LORA), F32),
                   jax.ShapeDtypeStruct((MLA_HEADS, t, MLA_ROPE), F32)],
        compiler_params=_cparams("parallel"),
        name="mla_q",
    )(cq, q_norm_w.reshape(1, -1), w_nope, w_rope, w_swap, w_bd, cos4, sin4)


def _flash_update(s, c, m_ref, l_ref, acc_ref):
    m_prev = m_ref[...]
    m_new = jnp.maximum(m_prev, jnp.max(s, axis=-1, keepdims=True))
    alpha = jnp.exp(m_prev - m_new)
    p = jnp.exp(s - m_new)
    l_ref[...] = alpha * l_ref[...] + jnp.sum(p, axis=-1, keepdims=True)
    acc_ref[...] = alpha * acc_ref[...] + _bdot(p, c)
    m_ref[...] = m_new


def _mla_finish(acc_ref, l_ref, wv_ref, o_ref, rows_per_head):
    o = acc_ref[...] / l_ref[...]
    outs = []
    for j in range(MLA_HEADS // 2):
        pair = None
        for h in (2 * j, 2 * j + 1):
            part = _bdot(o[h * rows_per_head:(h + 1) * rows_per_head], wv_ref[h])
            pair = part if pair is None else pair + part
        outs.append(pair)
    o_ref[...] = jnp.concatenate(outs, axis=-1)


def _mla_prompt_kernel(ql_ref, qr_ref, c_ref, r_ref, wv_ref, o_ref, m_ref, l_ref, acc_ref, *, tq, tk):
    qi = pl.program_id(1)
    kj = pl.program_id(2)
    last = (qi * tq + tq - 1) // tk
    rows = MLA_HEADS * tq

    @pl.when(kj == 0)
    def _():
        m_ref[...] = jnp.full(m_ref.shape, NEG_INF, F32)
        l_ref[...] = jnp.zeros(l_ref.shape, F32)
        acc_ref[...] = jnp.zeros(acc_ref.shape, F32)

    @pl.when(kj <= last)
    def _():
        ql = ql_ref[...].reshape(rows, MLA_KV_LORA)
        qr = qr_ref[...].reshape(rows, MLA_ROPE)
        c = c_ref[...]
        s = (_bdot_nt(ql, c) + _bdot_nt(qr, r_ref[...])) * ((MLA_NOPE + MLA_ROPE) ** -0.5)
        q_pos = qi * tq + lax.broadcasted_iota(jnp.int32, (rows, tk), 0) % tq
        k_pos = kj * tk + lax.broadcasted_iota(jnp.int32, (rows, tk), 1)
        s = jnp.where(k_pos <= q_pos, s, NEG_INF)
        _flash_update(s, c, m_ref, l_ref, acc_ref)

    @pl.when(kj == last)
    def _():
        _mla_finish(acc_ref, l_ref, wv_ref, o_ref, tq)


def _pad_uv(w_uv):
    a = jnp.transpose(w_uv, (1, 0, 2))
    z = jnp.zeros_like(a)
    even = (jnp.arange(a.shape[0]) % 2 == 0)[:, None, None]
    return jnp.concatenate([jnp.where(even, a, z), jnp.where(even, z, a)], axis=2)


def _mla_prompt_attn(q_lat, q_rope, c_kv, k_r, w_uv_pad, *, n_seq, seq_len, tq, tk):
    t = c_kv.shape[0]
    nq, nk = seq_len // tq, seq_len // tk

    def qmap(b, i, j):
        return (0, b * nq + i, 0)

    def kmap(b, i, j):
        return (b * nk + jnp.minimum(j, (i * tq + tq - 1) // tk), 0)

    rows = MLA_HEADS * tq
    return pl.pallas_call(
        functools.partial(_mla_prompt_kernel, tq=tq, tk=tk),
        grid=(n_seq, nq, nk),
        in_specs=[pl.BlockSpec((MLA_HEADS, tq, MLA_KV_LORA), qmap),
                  pl.BlockSpec((MLA_HEADS, tq, MLA_ROPE), qmap),
                  pl.BlockSpec((tk, MLA_KV_LORA), kmap), pl.BlockSpec((tk, MLA_ROPE), kmap),
                  pl.BlockSpec(w_uv_pad.shape, lambda b, i, j: (0, 0, 0))],
        out_specs=pl.BlockSpec((tq, MLA_HEADS * MLA_V), lambda b, i, j: (b * nq + i, 0)),
        out_shape=jax.ShapeDtypeStruct((t, MLA_HEADS * MLA_V), F32),
        scratch_shapes=[pltpu.VMEM((rows, 1), F32), pltpu.VMEM((rows, 1), F32),
                        pltpu.VMEM((rows, MLA_KV_LORA), F32)],
        compiler_params=_cparams("parallel", "parallel", "arbitrary"),
        name="mla_prompt_attn",
    )(q_lat, q_rope, c_kv, k_r, w_uv_pad)


def _mla_sample_kernel(*refs, n_pg, t_s):
    pt_ref = refs[0]
    ql_ref, qr_ref = refs[1], refs[2]
    lat_refs = refs[3:3 + n_pg]
    kr_refs = refs[3 + n_pg:3 + 2 * n_pg]
    cn_ref, rn_ref, wv_ref = refs[3 + 2 * n_pg:6 + 2 * n_pg]
    o_ref, m_ref, l_ref, acc_ref = refs[7 + 2 * n_pg:]
    del pt_ref
    kj = pl.program_id(1)
    n_steps = pl.num_programs(1)
    rows = MLA_HEADS * t_s
    scale = (MLA_NOPE + MLA_ROPE) ** -0.5

    @pl.when(kj == 0)
    def _():
        m_ref[...] = jnp.full(m_ref.shape, NEG_INF, F32)
        l_ref[...] = jnp.zeros(l_ref.shape, F32)
        acc_ref[...] = jnp.zeros(acc_ref.shape, F32)

    ql = ql_ref[...].reshape(rows, MLA_KV_LORA).astype(BF16)
    qr = qr_ref[...].reshape(rows, MLA_ROPE).astype(BF16)

    @pl.when(kj < n_steps - 1)
    def _():
        for p in range(n_pg):
            c = lat_refs[p][...]
            s = (_bdot_nt(ql, c) + _bdot_nt(qr, kr_refs[p][...])) * scale
            _flash_update(s, c, m_ref, l_ref, acc_ref)

    @pl.when(kj == n_steps - 1)
    def _():
        c = cn_ref[...]
        s = (_bdot_nt(ql, c) + _bdot_nt(qr, rn_ref[...])) * scale
        q_t = lax.broadcasted_iota(jnp.int32, (rows, t_s), 0) % t_s
        k_t = lax.broadcasted_iota(jnp.int32, (rows, t_s), 1)
        s = jnp.where(k_t <= q_t, s, NEG_INF)
        _flash_update(s, c, m_ref, l_ref, acc_ref)
        _mla_finish(acc_ref, l_ref, wv_ref, o_ref, t_s)


def _mla_sample_attn(q_lat, q_rope, c_kv, k_r, cache_lat, cache_kr, page_table, w_uv_pad, o_full, *,
                     n_seq, t_s, row0, n_pg):
    n_pages = page_table.shape[1]
    page = cache_lat.shape[1]
    assert n_pages % n_pg == 0 and row0 % t_s == 0
    n_steps = n_pages // n_pg + 1
    base = row0 // t_s

    def qmap(b, j, pt):
        return (0, base + b, 0)

    def newmap(b, j, pt):
        return (base + b, 0)

    def pagemap(p):
        return lambda b, j, pt: (pt[b, jnp.minimum(j * n_pg + p, n_pages - 1)], 0, 0)

    in_specs = [pl.BlockSpec((MLA_HEADS, t_s, MLA_KV_LORA), qmap),
                pl.BlockSpec((MLA_HEADS, t_s, MLA_ROPE), qmap)]
    in_specs += [pl.BlockSpec((None, page, MLA_KV_LORA), pagemap(p)) for p in range(n_pg)]
    in_specs += [pl.BlockSpec((None, page, MLA_ROPE), pagemap(p)) for p in range(n_pg)]
    in_specs += [pl.BlockSpec((t_s, MLA_KV_LORA), newmap), pl.BlockSpec((t_s, MLA_ROPE), newmap),
                 pl.BlockSpec(w_uv_pad.shape, lambda b, j, pt: (0, 0, 0)),
                 pl.BlockSpec(memory_space=pl.ANY)]
    rows = MLA_HEADS * t_s
    n_in = len(in_specs)
    return pl.pallas_call(
        functools.partial(_mla_sample_kernel, n_pg=n_pg, t_s=t_s),
        grid_spec=pltpu.PrefetchScalarGridSpec(
            num_scalar_prefetch=1,
            grid=(n_seq, n_steps),
            in_specs=in_specs,
            out_specs=pl.BlockSpec((t_s, MLA_HEADS * MLA_V), newmap),
            scratch_shapes=[pltpu.VMEM((rows, 1), F32), pltpu.VMEM((rows, 1), F32),
                            pltpu.VMEM((rows, MLA_KV_LORA), F32)]),
        out_shape=jax.ShapeDtypeStruct(o_full.shape, F32),
        input_output_aliases={n_in: 0},
        compiler_params=_cparams("parallel", "arbitrary"),
        name="mla_sample_attn",
    )(page_table, q_lat, q_rope, *([cache_lat] * n_pg), *([cache_kr] * n_pg), c_kv, k_r, w_uv_pad, o_full)


def _positions(dims, past_len):
    n_p, t_p, n_s, t_s = dims
    return jnp.concatenate([jnp.tile(jnp.arange(t_p), n_p), jnp.tile(past_len + jnp.arange(t_s), n_s)])


def _mla_layer(x, nw, cache_lat, cache_kr, page_table, w_in, q_norm_w, w_uq, kv_norm_w, w_uk, w_uv, w_out,
               dims):
    n_p, t_p, n_s, t_s = dims
    t = x.shape[0]
    past_len = page_table.shape[1] * cache_lat.shape[1]
    cos, sin = _rope_tables(_positions(dims, past_len), MLA_ROPE // 2)
    cq, ckv, kr = _mla_in(x, nw, w_in, kv_norm_w, cos, sin, tm=_tile(t, 640))
    q_lat, q_rope = _mla_q(cq, q_norm_w, w_uq, w_uk, cos, sin, tm=_tile(t, 256))
    wv = _pad_uv(w_uv)
    o = _mla_prompt_attn(q_lat, q_rope, ckv, kr, wv, n_seq=n_p, seq_len=t_p, tq=min(t_p, 128),
                         tk=min(t_p, 256))
    o = _mla_sample_attn(q_lat, q_rope, ckv, kr, cache_lat, cache_kr, page_table, wv, o, n_seq=n_s,
                         t_s=t_s, row0=n_p * t_p, n_pg=8)
    x = _linear(o, w_out, res=x, tm=_tile(t, 1280), tn=512, name="mla_out")
    return x, ckv, kr


def _lane_halves(a):
    half = LANES // 2
    low = lax.broadcasted_iota(jnp.int32, a.shape, 1) < half
    rolled = pltpu.roll(a, half, axis=1)
    head0 = (jnp.where(low, a, 0.0), jnp.where(low, 0.0, rolled))
    head1 = (jnp.where(low, rolled, 0.0), jnp.where(low, 0.0, a))
    return head0, head1


def _swa_heads(q, k_all, v_all, sink_ref, mask, o_ref):
    rq = q.shape[0]
    scale = SWA_HD ** -0.5
    top = lax.broadcasted_iota(jnp.int32, (2 * rq, 1), 0) < rq
    for cg in range(SWA_KV_HEADS // 2):
        k_heads = _lane_halves(k_all[:, cg * LANES:(cg + 1) * LANES])
        v_heads = _lane_halves(v_all[:, cg * LANES:(cg + 1) * LANES])
        for sub in range(2):
            kh = 2 * cg + sub
            (k_lo, k_hi), (v_lo, v_hi) = k_heads[sub], v_heads[sub]
            qs = jnp.concatenate([q[:, (2 * kh) * LANES:(2 * kh + 1) * LANES],
                                  q[:, (2 * kh + 1) * LANES:(2 * kh + 2) * LANES]], axis=0)
            acc = None
            for which, (kk, vv) in enumerate(((k_lo, v_lo), (k_hi, v_hi))):
                s = jnp.where(mask, _bdot_nt(qs, kk) * scale, NEG_INF)
                sink = jnp.where(top, sink_ref[4 * kh + which], sink_ref[4 * kh + 2 + which])
                m = jnp.maximum(jnp.max(s, axis=-1, keepdims=True), sink)
                e = jnp.exp(s - m)
                p = e / (jnp.sum(e, axis=-1, keepdims=True) + jnp.exp(sink - m))
                part = _bdot(p, vv)
                acc = part if acc is None else acc + part
            o_ref[:, (2 * kh) * LANES:(2 * kh + 1) * LANES] = acc[:rq]
            o_ref[:, (2 * kh + 1) * LANES:(2 * kh + 2) * LANES] = acc[rq:]


def _swa_prompt_kernel(sink_ref, q_ref, kp_ref, kc_ref, vp_ref, vc_ref, o_ref):
    n = pl.program_id(1)
    w = q_ref.shape[0]
    k_all = jnp.concatenate([kp_ref[...], kc_ref[...]], axis=0)
    v_all = jnp.concatenate([vp_ref[...], vc_ref[...]], axis=0)
    r = lax.broadcasted_iota(jnp.int32, (2 * w, 2 * w), 0) % w
    c = lax.broadcasted_iota(jnp.int32, (2 * w, 2 * w), 1)
    mask = (c >= r) & (c <= r + w) & ((n > 0) | (c >= w))
    _swa_heads(q_ref[...], k_all, v_all, sink_ref, mask, o_ref)


def _swa_prompt_attn(qkv, sinks, *, n_seq, seq_len):
    t = qkv.shape[0]
    w = WINDOW
    nb = seq_len // w
    dq = SWA_HEADS * SWA_HD
    dkv = SWA_KV_HEADS * SWA_HD
    kcol = dq // dkv
    cur = lambda col: (lambda b, n: (b * nb + n, col))
    prev = lambda col: (lambda b, n: (b * nb + jnp.maximum(n - 1, 0), col))
    return pl.pallas_call(
        _swa_prompt_kernel,
        grid=(n_seq, nb),
        in_specs=[pl.BlockSpec(memory_space=pltpu.SMEM),
                  pl.BlockSpec((w, dq), cur(0)),
                  pl.BlockSpec((w, dkv), prev(kcol)), pl.BlockSpec((w, dkv), cur(kcol)),
                  pl.BlockSpec((w, dkv), prev(kcol + 1)), pl.BlockSpec((w, dkv), cur(kcol + 1))],
        out_specs=pl.BlockSpec((w, dq), cur(0)),
        out_shape=jax.ShapeDtypeStruct((t, dq), F32),
        compiler_params=_cparams("parallel", "parallel"),
        name="swa_prompt_attn",
    )(sinks, qkv, qkv, qkv, qkv, qkv)


def _swa_sample_kernel(sink_ref, q_ref, kn_ref, vn_ref, kc_ref, vc_ref, alias_ref, o_ref, ko_ref, vo_ref):
    del alias_ref
    t_s = q_ref.shape[0]
    w = kc_ref.shape[0]
    k_all = jnp.concatenate([kc_ref[...], kn_ref[...]], axis=0)
    v_all = jnp.concatenate([vc_ref[...], vn_ref[...]], axis=0)
    r = lax.broadcasted_iota(jnp.int32, (2 * t_s, w + t_s), 0) % t_s
    c = lax.broadcasted_iota(jnp.int32, (2 * t_s, w + t_s), 1)
    mask = (c <= w + r) & (c >= r)
    _swa_heads(q_ref[...], k_all, v_all, sink_ref, mask, o_ref)
    ko_ref[...] = k_all[t_s:]
    vo_ref[...] = v_all[t_s:]


def _swa_sample_attn(qkv, cache_k, cache_v, sinks, o_full, *, n_seq, t_s, row0):
    w = cache_k.shape[1]
    dq = SWA_HEADS * SWA_HD
    dkv = SWA_KV_HEADS * SWA_HD
    kcol = dq // dkv
    base = row0 // t_s
    new = lambda col: (lambda b: (base + b, col))
    seq = lambda b: (b, 0, 0)
    return pl.pallas_call(
        _swa_sample_kernel,
        grid=(n_seq,),
        in_specs=[pl.BlockSpec(memory_space=pltpu.SMEM),
                  pl.BlockSpec((t_s, dq), new(0)),
                  pl.BlockSpec((t_s, dkv), new(kcol)), pl.BlockSpec((t_s, dkv), new(kcol + 1)),
                  pl.BlockSpec((None, w, dkv), seq), pl.BlockSpec((None, w, dkv), seq),
                  pl.BlockSpec(memory_space=pl.ANY)],
        out_specs=[pl.BlockSpec((t_s, dq), new(0)),
                   pl.BlockSpec((None, w, dkv), seq), pl.BlockSpec((None, w, dkv), seq)],
        out_shape=[jax.ShapeDtypeStruct(o_full.shape, F32),
                   jax.ShapeDtypeStruct(cache_k.shape, F32), jax.ShapeDtypeStruct(cache_v.shape, F32)],
        input_output_aliases={6: 0},
        compiler_params=_cparams("parallel"),
        name="swa_sample_attn",
    )(sinks, qkv, qkv, qkv, cache_k, cache_v, o_full)


def _swa_layer(x, nw, cache_k, cache_v, w_qkv, b_qkv, sinks, w_out, b_out, dims):
    n_p, t_p, n_s, t_s = dims
    t = x.shape[0]
    tm = _tile(t, 1280)
    dq = SWA_HEADS * SWA_HD
    dkv = SWA_KV_HEADS * SWA_HD
    qkv = _linear(x, w_qkv, norm_w=nw, bias=b_qkv, tm=tm, tn=512, name="swa_in")
    o = _swa_prompt_attn(qkv, sinks, n_seq=n_p, seq_len=t_p)
    o, k_s, v_s = _swa_sample_attn(qkv, cache_k.reshape(n_s, WINDOW, dkv), cache_v.reshape(n_s, WINDOW, dkv),
                                   sinks, o, n_seq=n_s, t_s=t_s, row0=n_p * t_p)
    x = _linear(o, w_out, bias=b_out, res=x, tm=tm, tn=512, name="swa_out")
    kv_p = qkv[:n_p * t_p].reshape(n_p, t_p, -1)[:, t_p - WINDOW:, dq:]
    kv_shape = (n_p, WINDOW, SWA_KV_HEADS, SWA_HD)
    k_p = kv_p[:, :, :dkv].reshape(kv_shape)
    v_p = kv_p[:, :, dkv:].reshape(kv_shape)
    return x, k_p, v_p, k_s.reshape(cache_k.shape), v_s.reshape(cache_v.shape)


def _l2norm(x):
    return x * lax.rsqrt(jnp.sum(x * x, axis=-1, keepdims=True) + 1e-6)


def _split2(x):
    h1 = x.astype(BF16)
    return h1, (x - h1.astype(F32)).astype(BF16)


def _bmm3(a, b):
    a1, a2 = _split2(a)
    b1, b2 = _split2(b)
    dot = lambda x, y: jnp.einsum("bij,bjk->bik", x, y, preferred_element_type=F32)
    return dot(a1, b1) + dot(a1, b2) + dot(a2, b1)


def _unit_lower_inverse(low):
    n = low.shape[-1]
    eye = (lax.broadcasted_iota(jnp.int32, (n, n), 0) == lax.broadcasted_iota(jnp.int32, (n, n), 1))
    eye = eye.astype(F32)[None]
    power = -low
    inv = eye + power
    steps = int(math.log2(n)) - 1
    for _ in range(steps):
        power = _bmm3(power, power)
        inv = inv + _bmm3(inv, power)
    return inv


def _dn_kernel(*refs, rb, zero_init, has_alias):
    it = iter(refs)
    x_ref, z_ref, ab_ref = next(it), next(it), next(it)
    s0_ref, c0_ref = (None, None) if zero_init else (next(it), next(it))
    cw_ref, al_ref, dt_ref, nw_ref = next(it), next(it), next(it), next(it)
    if has_alias:
        next(it)
    o_ref, s_ref, co_ref = next(it), next(it), next(it)
    st_ref, xp_ref = next(it), next(it)
    r = pl.program_id(1)
    rows = max(rb, CHUNK)
    n_chunks = rows // CHUNK
    halo = 8
    dqk = DN_HEADS * DN_HK

    @pl.when(r == 0)
    def _():
        if zero_init:
            st_ref[...] = jnp.zeros(st_ref.shape, F32)
            xp_ref[0:halo] = jnp.zeros((halo, xp_ref.shape[1]), F32)
        else:
            st_ref[...] = s0_ref[0]
            xp_ref[0:halo] = c0_ref[0]

    xp_ref[halo:halo + rb] = x_ref[...]
    conv = xp_ref[halo - DN_CONV + 1:halo - DN_CONV + 1 + rb] * cw_ref[0:1]
    for w in range(1, DN_CONV):
        conv = conv + xp_ref[halo - DN_CONV + 1 + w:halo - DN_CONV + 1 + w + rb] * cw_ref[w:w + 1]
    tail = xp_ref[rb:rb + halo]
    co_ref[0] = tail
    xp_ref[0:halo] = tail
    qkv = _pad_rows(_silu(conv), rows)
    ab = ab_ref[...]
    g_all = _pad_rows(-jnp.exp(al_ref[...]) * _softplus(ab + dt_ref[...]), rows)
    beta_all = _pad_rows(1.0 / (1.0 + jnp.exp(-ab)), rows)

    ri = lax.broadcasted_iota(jnp.int32, (rows, rows), 0)
    ci = lax.broadcasted_iota(jnp.int32, (rows, rows), 1)
    same = (ri // CHUNK) == (ci // CHUNK)
    causal = same & (ci <= ri)
    strict = same & (ci < ri)
    upper = (same & (ri <= ci)).astype(BF16)
    gc_col = _dot_exact_lhs(causal, g_all)
    gt_col = _dot_exact_lhs(same, g_all)
    g1, g2, g3 = _split3(g_all)
    tn = lambda a: lax.dot_general(a, upper, (((0,), (0,)), ((), ())), preferred_element_type=F32)
    gc_row = tn(g1) + tn(g2) + tn(g3)

    lows, rhss, attns, qds, kos, gls = [], [], [], [], [], []
    for h in range(DN_HEADS):
        hs = slice(h * DN_HK, (h + 1) * DN_HK)
        q = _l2norm(qkv[:, hs]) * DN_HK ** -0.5
        k = _l2norm(qkv[:, dqk + h * DN_HK:dqk + (h + 1) * DN_HK])
        v = qkv[:, 2 * dqk + h * DN_HV:2 * dqk + (h + 1) * DN_HV]
        beta = beta_all[:, DN_HEADS + h:DN_HEADS + h + 1]
        gc = gc_col[:, h:h + 1]
        gt = gt_col[:, h:h + 1]
        decay = jnp.where(causal, jnp.exp(jnp.where(causal, gc - gc_row[h:h + 1, :], 0.0)), 0.0)
        kb = k * beta
        low = jnp.where(strict, _bdot_nt(kb, k) * decay, 0.0)
        attn = _bdot_nt(q, k) * decay
        rhs = jnp.concatenate([v * beta, kb * jnp.exp(gc)], axis=-1)
        q_dec = q * jnp.exp(gc)
        k_out = k * jnp.exp(gt - gc)
        g_last = jnp.exp(gt)
        for c in range(n_chunks):
            rs = slice(c * CHUNK, (c + 1) * CHUNK)
            lows.append(low[rs, rs])
            attns.append(attn[rs, rs])
            rhss.append(rhs[rs])
            qds.append(q_dec[rs])
            kos.append(k_out[rs])
            gls.append(g_last[c * CHUNK:c * CHUNK + 1])
    sol = _bmm3(_unit_lower_inverse(jnp.stack(lows)), jnp.stack(rhss))
    n_out = min(rb, CHUNK)
    for h in range(DN_HEADS):
        s = st_ref[h]
        for c in range(n_chunks):
            i = h * n_chunks + c
            u, wm = sol[i, :, :DN_HV], sol[i, :, DN_HV:]
            v_new = u - _bdot(wm, s)
            o = _bdot(qds[i], s) + _bdot(attns[i], v_new)
            s = s * gls[i] + _bdot_tn(kos[i], v_new)
            zs = z_ref[c * CHUNK:c * CHUNK + n_out, h * DN_HV:(h + 1) * DN_HV]
            o_ref[c * CHUNK:c * CHUNK + n_out, h * DN_HV:(h + 1) * DN_HV] = (
                _rms(o[:n_out], nw_ref[...]) * _silu(zs))
        st_ref[h] = s

    @pl.when(r == pl.num_programs(1) - 1)
    def _():
        s_ref[0] = st_ref[...]


def _dn_core(proj, s0, conv0, conv_w, a_log, dt_bias, norm_w, o_full, *, n_seq, seq_len, row0, rb):
    t = proj.shape[0]
    dconv = conv_w.shape[1]
    dz = DN_HEADS * DN_HV
    nblk = seq_len // rb
    base = row0 // rb
    assert row0 % rb == 0 and seq_len % rb == 0 and rb % 8 == 0

    def rowmap(col):
        return lambda b, r: (base + b * nblk + r, col)

    seq4 = lambda b, r: (b, 0, 0, 0)
    seq3 = lambda b, r: (b, 0, 0)
    fix = lambda b, r: (0, 0)
    in_specs = [pl.BlockSpec((rb, dconv), rowmap(0)), pl.BlockSpec((rb, dz), rowmap(dconv // dz)),
                pl.BlockSpec((rb, LANES), rowmap((dconv + dz) // LANES))]
    args = [proj, proj, proj]
    if s0 is not None:
        in_specs += [pl.BlockSpec((1, DN_HEADS, DN_HK, DN_HV), seq4), pl.BlockSpec((1, 8, dconv), seq3)]
        args += [s0, conv0]
    in_specs += [pl.BlockSpec((DN_CONV, dconv), fix), pl.BlockSpec((1, LANES), fix),
                 pl.BlockSpec((1, LANES), fix), pl.BlockSpec((1, DN_HV), fix)]
    args += [conv_w, _pad_cols(a_log.reshape(1, -1), LANES), _pad_cols(dt_bias.reshape(1, -1), LANES),
             norm_w.reshape(1, -1)]
    aliases = {}
    if o_full is not None:
        in_specs.append(pl.BlockSpec(memory_space=pl.ANY))
        aliases = {len(args): 0}
        args.append(o_full)
    kern = functools.partial(_dn_kernel, rb=rb, zero_init=s0 is None, has_alias=o_full is not None)
    return pl.pallas_call(
        kern,
        grid=(n_seq, nblk),
        in_specs=in_specs,
        out_specs=[pl.BlockSpec((rb, dz), rowmap(0)),
                   pl.BlockSpec((1, DN_HEADS, DN_HK, DN_HV), seq4),
                   pl.BlockSpec((1, 8, dconv), seq3)],
        out_shape=[jax.ShapeDtypeStruct((t, dz), F32),
                   jax.ShapeDtypeStruct((n_seq, DN_HEADS, DN_HK, DN_HV), F32),
                   jax.ShapeDtypeStruct((n_seq, 8, dconv), F32)],
        scratch_shapes=[pltpu.VMEM((DN_HEADS, DN_HK, DN_HV), F32), pltpu.VMEM((rb + 8, dconv), F32)],
        input_output_aliases=aliases,
        compiler_params=_cparams("parallel", "arbitrary"),
        name="dn_core",
    )(*args)


def _dn_layer(x, nw, s0_s, conv0_s, w_in, conv_w, a_log, dt_bias, norm_w, w_out, dims):
    n_p, t_p, n_s, t_s = dims
    t = x.shape[0]
    tm = _tile(t, 1280)
    dconv = conv_w.shape[1]
    dz = DN_HEADS * DN_HV
    width = dconv + dz + LANES
    proj = _linear(x, _pad_cols(w_in, width), norm_w=nw, tm=tm, tn=_tile(width, 384), name="dn_in")
    o, sp, cp = _dn_core(proj, None, None, conv_w, a_log, dt_bias, norm_w, None, n_seq=n_p, seq_len=t_p,
                         row0=0, rb=min(t_p, 128))
    conv0 = jnp.pad(conv0_s, ((0, 0), (8 - conv0_s.shape[1], 0), (0, 0)))
    o, ss, cs = _dn_core(proj, s0_s, conv0, conv_w, a_log, dt_bias, norm_w, o, n_seq=n_s, seq_len=t_s,
                         row0=n_p * t_p, rb=t_s)
    x = _linear(o, w_out, res=x, tm=tm, tn=512, name="dn_out")
    keep = DN_CONV - 1
    return x, sp, ss, cp[:, 8 - keep:], cs[:, 8 - keep:]


def kernel(x_prompt, x_sample, state_gla, cache_mla_latent, cache_mla_krope, cache_swa_k, cache_swa_v, state_delta, state_delta_conv, page_table, norm_w, final_norm_w, gla_w_in, gla_w_gk2, gla_b_gk2, gla_norm_w, gla_w_out, mla_w_in, mla_q_norm_w, mla_w_uq, mla_kv_norm_w, mla_w_uk, mla_w_uv, mla_w_out, swa_w_qkv, swa_b_qkv, swa_sinks, swa_w_out, swa_b_out, dn_w_in, dn_conv_w, dn_a_log, dn_dt_bias, dn_norm_w, dn_w_out, ffn_w_gate, ffn_w_up, ffn_w_down, moe_w_router, moe_w_gate, moe_w_up, moe_w_down):
    n_p, t_p, d = x_prompt.shape
    n_s, t_s, _ = x_sample.shape
    dims = (n_p, t_p, n_s, t_s)
    x = jnp.concatenate([x_prompt.reshape(n_p * t_p, d), x_sample.reshape(n_s * t_s, d)], axis=0)
    t = x.shape[0]
    tm = _tile(t, 1280)
    n_tp = n_p * t_p

    x, gla_p, gla_s = _gla_layer(x, norm_w[0, 0], state_gla[0], gla_w_in[0], gla_w_gk2[0], gla_b_gk2[0],
                                 gla_norm_w[0], gla_w_out[0], dims)
    x = _ffn(x, norm_w[0, 1], ffn_w_gate[0:1], ffn_w_up[0:1], ffn_w_down[0:1], tm=tm, tf=256)

    x, ckv, kr = _mla_layer(x, norm_w[1, 0], cache_mla_latent[0], cache_mla_krope[0], page_table,
                            mla_w_in[0], mla_q_norm_w[0], mla_w_uq[0], mla_kv_norm_w[0], mla_w_uk[0],
                            mla_w_uv[0], mla_w_out[0], dims)
    comb = _router(x, norm_w[1, 1], moe_w_router[0], tm=tm)
    x = _ffn(x, norm_w[1, 1], moe_w_gate[0], moe_w_up[0], moe_w_down[0], combine=comb, tm=tm, tf=256)

    x, swk_p, swv_p, swk_s, swv_s = _swa_layer(x, norm_w[2, 0], cache_swa_k[0], cache_swa_v[0],
                                               swa_w_qkv[0], swa_b_qkv[0], swa_sinks[0], swa_w_out[0],
                                               swa_b_out[0], dims)
    x = _ffn(x, norm_w[2, 1], ffn_w_gate[1:2], ffn_w_up[1:2], ffn_w_down[1:2], tm=tm, tf=256)

    x, dn_p, dn_s, cv_p, cv_s = _dn_layer(x, norm_w[3, 0], state_delta[0], state_delta_conv[0], dn_w_in[0],
                                          dn_conv_w[0], dn_a_log[0], dn_dt_bias[0], dn_norm_w[0],
                                          dn_w_out[0], dims)
    comb = _router(x, norm_w[3, 1], moe_w_router[1], tm=tm)
    y = _ffn(x, norm_w[3, 1], moe_w_gate[1], moe_w_up[1], moe_w_down[1], combine=comb,
             final_w=final_norm_w, tm=tm, tf=256)

    lead = lambda a: a[None]
    return (y[:n_tp].reshape(n_p, t_p, d), y[n_tp:].reshape(n_s, t_s, d),
            lead(gla_p), lead(gla_s),
            lead(ckv[:n_tp].reshape(n_p, t_p, -1)), lead(ckv[n_tp:].reshape(n_s, t_s, -1)),
            lead(kr[:n_tp].reshape(n_p, t_p, -1)), lead(kr[n_tp:].reshape(n_s, t_s, -1)),
            lead(swk_p), lead(swk_s), lead(swv_p), lead(swv_s),
            lead(dn_p), lead(dn_s), lead(cv_p), lead(cv_s))
```

```python
import functools
import math

import jax
import jax.numpy as jnp
import numpy as np
from jax import lax
from jax.experimental import pallas as pl
from jax.experimental.pallas import tpu as pltpu

F32 = jnp.float32
BF16 = jnp.bfloat16

NORM_EPS = 1e-6
GLA_HEADS = 4
GLA_GATE_RANK = 16
GLA_GATE_NORMALIZER = 16.0
CHUNK = 64
MLA_HEADS = 16
MLA_Q_LORA = 384
MLA_KV_LORA = 256
MLA_NOPE = 64
MLA_ROPE = 32
MLA_V = 64
ROPE_THETA = 10000.0
SWA_HEADS = 16
SWA_KV_HEADS = 4
SWA_HD = 64
WINDOW = 128
DN_HEADS = 8
DN_HK = 128
DN_HV = 128
DN_CONV = 4
N_EXPERTS = 8

LANES = 128
VMEM_LIMIT = 56 * 1024 * 1024
NEG_INF = float("-inf")


def _cparams(*sem):
    return pltpu.CompilerParams(dimension_semantics=sem, vmem_limit_bytes=VMEM_LIMIT)


def _bdot(a, b):
    return jnp.dot(a.astype(BF16), b.astype(BF16), preferred_element_type=F32)


def _bdot_nt(a, b):
    return lax.dot_general(a.astype(BF16), b.astype(BF16), (((1,), (1,)), ((), ())),
                           preferred_element_type=F32)


def _bdot_tn(a, b):
    return lax.dot_general(a.astype(BF16), b.astype(BF16), (((0,), (0,)), ((), ())),
                           preferred_element_type=F32)


def _split3(x):
    h1 = x.astype(BF16)
    r1 = x - h1.astype(F32)
    h2 = r1.astype(BF16)
    h3 = (r1 - h2.astype(F32)).astype(BF16)
    return h1, h2, h3


def _dot_exact_lhs(m, x):
    mb = m.astype(BF16)
    h1, h2, h3 = _split3(x)
    return (jnp.dot(mb, h1, preferred_element_type=F32) + jnp.dot(mb, h2, preferred_element_type=F32)
            + jnp.dot(mb, h3, preferred_element_type=F32))


def _rms(x, w):
    return x * lax.rsqrt(jnp.mean(x * x, axis=-1, keepdims=True) + NORM_EPS) * w


def _silu(x):
    return x / (1.0 + jnp.exp(-x))


def _log_sigmoid(x):
    return jnp.minimum(x, 0.0) - jnp.log(1.0 + jnp.exp(-jnp.abs(x)))


def _softplus(x):
    return jnp.maximum(x, 0.0) + jnp.log(1.0 + jnp.exp(-jnp.abs(x)))


def _linear_kernel(*refs, has_norm, has_bias, has_res):
    it = iter(refs)
    x_ref = next(it)
    nw_ref = next(it) if has_norm else None
    w_ref = next(it)
    b_ref = next(it) if has_bias else None
    r_ref = next(it) if has_res else None
    o_ref = next(it)
    h_ref = next(it)

    @pl.when(pl.program_id(1) == 0)
    def _():
        xv = x_ref[...].astype(F32)
        if has_norm:
            xv = _rms(xv, nw_ref[...])
        h_ref[...] = xv.astype(BF16)

    acc = jnp.dot(h_ref[...], w_ref[...].astype(BF16), preferred_element_type=F32)
    if has_bias:
        acc = acc + b_ref[...]
    if has_res:
        acc = acc + r_ref[...]
    o_ref[...] = acc.astype(o_ref.dtype)


def _linear(x, w, *, norm_w=None, bias=None, res=None, tm, tn, out_dtype=F32, name="linear"):
    t, k = x.shape
    n = w.shape[1]
    assert t % tm == 0 and n % tn == 0, (t, tm, n, tn)
    in_specs = [pl.BlockSpec((tm, k), lambda i, j: (i, 0))]
    args = [x]
    if norm_w is not None:
        in_specs.append(pl.BlockSpec((1, k), lambda i, j: (0, 0)))
        args.append(norm_w.reshape(1, k))
    in_specs.append(pl.BlockSpec((k, tn), lambda i, j: (0, j)))
    args.append(w)
    if bias is not None:
        in_specs.append(pl.BlockSpec((1, tn), lambda i, j: (0, j)))
        args.append(bias.reshape(1, n))
    if res is not None:
        in_specs.append(pl.BlockSpec((tm, tn), lambda i, j: (i, j)))
        args.append(res)
    kern = functools.partial(_linear_kernel, has_norm=norm_w is not None, has_bias=bias is not None,
                             has_res=res is not None)
    return pl.pallas_call(
        kern,
        grid=(t // tm, n // tn),
        in_specs=in_specs,
        out_specs=pl.BlockSpec((tm, tn), lambda i, j: (i, j)),
        out_shape=jax.ShapeDtypeStruct((t, n), out_dtype),
        scratch_shapes=[pltpu.VMEM((tm, k), BF16)],
        compiler_params=_cparams("parallel", "arbitrary"),
        name=name,
    )(*args)


def _swiglu_step(h, wg_ref, wu_ref, wd_ref):
    g = jnp.dot(h, wg_ref[...].astype(BF16), preferred_element_type=F32)
    u = jnp.dot(h, wu_ref[...].astype(BF16), preferred_element_type=F32)
    a = (_silu(g) * u).astype(BF16)
    return jnp.dot(a, wd_ref[...].astype(BF16), preferred_element_type=F32)


def _ffn_kernel(x_ref, nw_ref, wg_ref, wu_ref, wd_ref, o_ref, h_ref, acc_ref):
    f = pl.program_id(1)

    @pl.when(f == 0)
    def _():
        h_ref[...] = _rms(x_ref[...], nw_ref[...]).astype(BF16)
        acc_ref[...] = jnp.zeros_like(acc_ref)

    acc_ref[...] += _swiglu_step(h_ref[...], wg_ref, wu_ref, wd_ref)

    @pl.when(f == pl.num_programs(1) - 1)
    def _():
        o_ref[...] = x_ref[...] + acc_ref[...]


def _ffn(x, norm_w, wg, wu, wd, layer, *, tm, tf):
    t, d = x.shape
    ff = wg.shape[2]
    assert t % tm == 0 and ff % tf == 0
    return pl.pallas_call(
        _ffn_kernel,
        grid=(t // tm, ff // tf),
        in_specs=[pl.BlockSpec((tm, d), lambda i, f: (i, 0)),
                  pl.BlockSpec((1, d), lambda i, f: (0, 0)),
                  pl.BlockSpec((None, d, tf), lambda i, f: (layer, 0, f)),
                  pl.BlockSpec((None, d, tf), lambda i, f: (layer, 0, f)),
                  pl.BlockSpec((None, tf, d), lambda i, f: (layer, f, 0))],
        out_specs=pl.BlockSpec((tm, d), lambda i, f: (i, 0)),
        out_shape=jax.ShapeDtypeStruct((t, d), F32),
        scratch_shapes=[pltpu.VMEM((tm, d), BF16), pltpu.VMEM((tm, d), F32)],
        compiler_params=_cparams("parallel", "arbitrary"),
        name="ffn",
    )(x, norm_w.reshape(1, d), wg, wu, wd)


MOE_CHUNK = 256
MOE_SLOTS = 4


def _route_kernel(x_ref, nw_ref, wrt_ref, h_ref, sel_ref, gate_ref, rank_ref, cnt_ref):
    hn = _rms(x_ref[...], nw_ref[...])
    h_ref[...] = hn.astype(BF16)
    logits = lax.dot_general(wrt_ref[...], hn, (((1,), (1,)), ((), ())), preferred_element_type=F32,
                             precision=lax.Precision.HIGHEST)
    n_exp, tm = logits.shape
    sub = lax.broadcasted_iota(jnp.int32, logits.shape, 0)
    m1 = jnp.max(logits, axis=0, keepdims=True)
    i1 = jnp.min(jnp.where(logits == m1, sub, n_exp), axis=0, keepdims=True)
    rest = jnp.where(sub == i1, NEG_INF, logits)
    m2 = jnp.max(rest, axis=0, keepdims=True)
    i2 = jnp.min(jnp.where(rest == m2, sub, n_exp), axis=0, keepdims=True)
    e2 = jnp.exp(m2 - m1)
    first, second = sub == i1, sub == i2
    sel = jnp.where(first | second, 1.0, 0.0)
    upper = jnp.where(lax.broadcasted_iota(jnp.int32, (tm, tm), 0)
                      <= lax.broadcasted_iota(jnp.int32, (tm, tm), 1), 1.0, 0.0)
    cum = _bdot(sel, upper)
    sel_ref[...] = sel
    gate_ref[...] = jnp.where(first, 1.0 / (1.0 + e2), 0.0) + jnp.where(second, e2 / (1.0 + e2), 0.0)
    rank_ref[...] = cum - sel
    cnt_ref[0] = jnp.broadcast_to(cum[:, tm - 1:tm], (n_exp, LANES))


def _route(x, norm_w, w_router, *, tm):
    t, d = x.shape
    n_exp = w_router.shape[1]
    et = lambda i: (0, i)
    return pl.pallas_call(
        _route_kernel,
        grid=(t // tm,),
        in_specs=[pl.BlockSpec((tm, d), lambda i: (i, 0)), pl.BlockSpec((1, d), lambda i: (0, 0)),
                  pl.BlockSpec((n_exp, d), lambda i: (0, 0))],
        out_specs=[pl.BlockSpec((tm, d), lambda i: (i, 0)), pl.BlockSpec((n_exp, tm), et),
                   pl.BlockSpec((n_exp, tm), et), pl.BlockSpec((n_exp, tm), et),
                   pl.BlockSpec((1, n_exp, LANES), lambda i: (i, 0, 0))],
        out_shape=[jax.ShapeDtypeStruct((t, d), BF16), jax.ShapeDtypeStruct((n_exp, t), F32),
                   jax.ShapeDtypeStruct((n_exp, t), F32), jax.ShapeDtypeStruct((n_exp, t), F32),
                   jax.ShapeDtypeStruct((t // tm, n_exp, LANES), F32)],
        compiler_params=_cparams("parallel"),
        name="moe_route",
    )(x, norm_w.reshape(1, d), w_router.T)


def _moe_plan(cnt, *, tm, tp, p_rows):
    n_tiles, n_exp = cnt.shape
    seg = (cnt + 7) // 8 * 8
    total = jnp.sum(seg, axis=0)
    gsize = (total + MOE_CHUNK + tp - 1) // tp * tp
    gstart = jnp.cumsum(gsize) - gsize
    seg_start = gstart[None, :] + jnp.cumsum(seg, axis=0) - seg
    nch = (cnt + MOE_CHUNK - 1) // MOE_CHUNK
    cum_e = jnp.cumsum(nch, axis=1)
    q_max = n_exp + 2 * tm // MOE_CHUNK
    q = jnp.arange(q_max, dtype=jnp.int32)[None, :]
    flat_e = jnp.minimum(jnp.sum(q[:, :, None] >= cum_e[:, None, :], axis=-1), n_exp - 1).astype(jnp.int32)
    flat_c = q - jnp.take_along_axis(cum_e - nch, flat_e, axis=1)
    flat_row = jnp.take_along_axis(seg_start, flat_e, axis=1) + flat_c * MOE_CHUNK
    row_j = jnp.arange(p_rows // tp, dtype=jnp.int32) * tp
    tile_e = jnp.minimum(jnp.sum(row_j[:, None] >= (gstart + gsize)[None, :], axis=-1), n_exp - 1)
    tile_valid = row_j < jnp.take(gstart + total, tile_e)
    i32 = lambda a: a.astype(jnp.int32)
    return (i32(cum_e[:, -1]), i32(flat_e.reshape(-1)), i32(flat_c.reshape(-1)), i32(flat_row.reshape(-1)),
            i32(tile_e), i32(tile_valid))


def _dispatch_kernel(nq_ref, fe_ref, fc_ref, fr_ref, h_ref, sel_ref, rank_ref, xs_in, xs_ref, stage, sem,
                     *, q_max):
    del xs_in
    i = pl.program_id(0)
    n = nq_ref[i]
    h = h_ref[...]

    def chunk_copy(slot, row):
        return pltpu.make_async_copy(stage.at[slot], xs_ref.at[pl.ds(pl.multiple_of(row, 8), MOE_CHUNK)],
                                     sem.at[slot])

    def body(q, carry):
        slot = q % MOE_SLOTS
        e = fe_ref[i * q_max + q]
        c = fc_ref[i * q_max + q]

        @pl.when(q >= MOE_SLOTS)
        def _():
            chunk_copy(slot, 0).wait()

        pos = jnp.where(sel_ref[pl.ds(e, 1), :] > 0.0, rank_ref[pl.ds(e, 1), :], -1.0)
        want = (c * MOE_CHUNK + lax.broadcasted_iota(jnp.int32, (MOE_CHUNK, 1), 0)).astype(F32)
        pick = jnp.where(pos == want, 1.0, 0.0).astype(BF16)
        stage[slot] = jnp.dot(pick, h, preferred_element_type=F32)
        chunk_copy(slot, fr_ref[i * q_max + q]).start()
        return carry

    lax.fori_loop(0, n, body, 0)
    for s in range(MOE_SLOTS):
        @pl.when(n > s)
        def _():
            chunk_copy(s, 0).wait()


def _dispatch(h, sel, rank, plan, *, tm, p_rows):
    t, d = h.shape
    n_exp = sel.shape[0]
    n_flat, flat_e, flat_c, flat_row = plan[:4]
    q_max = flat_e.shape[0] // (t // tm)
    et = lambda i, *_: (0, i)
    return pl.pallas_call(
        functools.partial(_dispatch_kernel, q_max=q_max),
        grid_spec=pltpu.PrefetchScalarGridSpec(
            num_scalar_prefetch=4,
            grid=(t // tm,),
            in_specs=[pl.BlockSpec((tm, d), lambda i, *_: (i, 0)), pl.BlockSpec((n_exp, tm), et),
                      pl.BlockSpec((n_exp, tm), et), pl.BlockSpec(memory_space=pl.ANY)],
            out_specs=pl.BlockSpec(memory_space=pl.ANY),
            scratch_shapes=[pltpu.VMEM((MOE_SLOTS, MOE_CHUNK, d), F32),
                            pltpu.SemaphoreType.DMA((MOE_SLOTS,))]),
        out_shape=jax.ShapeDtypeStruct((p_rows, d), F32),
        input_output_aliases={7: 0},
        compiler_params=_cparams("arbitrary"),
        name="moe_dispatch",
    )(n_flat, flat_e, flat_c, flat_row, h, sel, rank, jnp.zeros((p_rows, d), F32))


def _gffn_kernel(te_ref, tv_ref, x_ref, wg_ref, wu_ref, wd_ref, o_ref, acc_ref):
    del te_ref
    j = pl.program_id(0)
    f = pl.program_id(1)
    valid = tv_ref[j] > 0

    @pl.when(f == 0)
    def _():
        acc_ref[...] = jnp.zeros_like(acc_ref)

    @pl.when(valid)
    def _():
        acc_ref[...] += _swiglu_step(x_ref[...].astype(BF16), wg_ref, wu_ref, wd_ref)

    @pl.when(f == pl.num_programs(1) - 1)
    def _():
        o_ref[...] = acc_ref[...]


def _gffn(xs, wg, wu, wd, layer, tile_e, tile_valid, *, tp, tf):
    p_rows, d = xs.shape
    ff = wg.shape[3]

    def wmap(is_down):
        def index(j, f, te, tv):
            fi = jnp.where(tv[j] > 0, f, 0)
            return (layer, te[j], fi, 0) if is_down else (layer, te[j], 0, fi)
        return index

    return pl.pallas_call(
        _gffn_kernel,
        grid_spec=pltpu.PrefetchScalarGridSpec(
            num_scalar_prefetch=2,
            grid=(p_rows // tp, ff // tf),
            in_specs=[pl.BlockSpec((tp, d), lambda j, f, te, tv: (j, 0)),
                      pl.BlockSpec((None, None, d, tf), wmap(False)),
                      pl.BlockSpec((None, None, d, tf), wmap(False)),
                      pl.BlockSpec((None, None, tf, d), wmap(True))],
            out_specs=pl.BlockSpec((tp, d), lambda j, f, te, tv: (j, 0)),
            scratch_shapes=[pltpu.VMEM((tp, d), F32)]),
        out_shape=jax.ShapeDtypeStruct((p_rows, d), F32),
        compiler_params=_cparams("parallel", "arbitrary"),
        name="moe_ffn",
    )(tile_e, tile_valid, xs, wg, wu, wd)


def _combine_kernel(*refs, q_max, final_norm):
    nq_ref, fe_ref, fc_ref, fr_ref = refs[:4]
    x_ref, pos_ref, gate_ref = refs[4:7]
    fw_ref = refs[7] if final_norm else None
    ys_ref, o_ref, acc_ref, buf, sem = refs[7 + int(final_norm):]
    i = pl.program_id(0)
    n = nq_ref[i]
    acc_ref[...] = jnp.zeros_like(acc_ref)
    lane = lax.broadcasted_iota(jnp.int32, pos_ref.shape, 1)

    def chunk_copy(slot, row):
        return pltpu.make_async_copy(ys_ref.at[pl.ds(pl.multiple_of(row, 8), MOE_CHUNK)], buf.at[slot],
                                     sem.at[slot])

    @pl.when(n > 0)
    def _():
        chunk_copy(0, fr_ref[i * q_max]).start()

    def body(q, carry):
        slot = q % 2
        e = fe_ref[i * q_max + q]
        c = fc_ref[i * q_max + q]

        @pl.when(q + 1 < n)
        def _():
            chunk_copy(1 - slot, fr_ref[i * q_max + q + 1]).start()

        chunk_copy(slot, 0).wait()
        pos = jnp.sum(jnp.where(lane == e, pos_ref[...], 0.0), axis=1, keepdims=True)
        gate = jnp.sum(jnp.where(lane == e, gate_ref[...], 0.0), axis=1, keepdims=True)
        want = (c * MOE_CHUNK + lax.broadcasted_iota(jnp.int32, (1, MOE_CHUNK), 1)).astype(F32)
        pick = jnp.where(pos == want, 1.0, 0.0).astype(BF16)
        hi, lo = _split2(buf[slot])
        rows = jnp.dot(pick, hi, preferred_element_type=F32) + jnp.dot(pick, lo, preferred_element_type=F32)
        acc_ref[...] += gate * rows
        return carry

    lax.fori_loop(0, n, body, 0)
    y = x_ref[...] + acc_ref[...]
    if final_norm:
        y = _rms(y, fw_ref[...])
    o_ref[...] = y


def _combine(x, ys, pos_tok, gate_tok, plan, final_w, *, tm):
    t, d = x.shape
    n_exp = pos_tok.shape[1]
    n_flat, flat_e, flat_c, flat_row = plan[:4]
    q_max = flat_e.shape[0] // (t // tm)
    row = lambda i, *_: (i, 0)
    in_specs = [pl.BlockSpec((tm, d), row), pl.BlockSpec((tm, n_exp), row), pl.BlockSpec((tm, n_exp), row)]
    args = [x, pos_tok, gate_tok]
    if final_w is not None:
        in_specs.append(pl.BlockSpec((1, d), lambda i, *_: (0, 0)))
        args.append(final_w.reshape(1, d))
    in_specs.append(pl.BlockSpec(memory_space=pl.ANY))
    args.append(ys)
    return pl.pallas_call(
        functools.partial(_combine_kernel, q_max=q_max, final_norm=final_w is not None),
        grid_spec=pltpu.PrefetchScalarGridSpec(
            num_scalar_prefetch=4,
            grid=(t // tm,),
            in_specs=in_specs,
            out_specs=pl.BlockSpec((tm, d), row),
            scratch_shapes=[pltpu.VMEM((tm, d), F32), pltpu.VMEM((2, MOE_CHUNK, d), F32),
                            pltpu.SemaphoreType.DMA((2,))]),
        out_shape=jax.ShapeDtypeStruct((t, d), F32),
        compiler_params=_cparams("arbitrary"),
        name="moe_combine",
    )(n_flat, flat_e, flat_c, flat_row, *args)


def _moe(x, norm_w, w_router, wg, wu, wd, layer, final_w=None, *, tm, tp, tf):
    t, d = x.shape
    n_exp = w_router.shape[1]
    n_tiles = t // tm
    worst = 2 * t + n_tiles * n_exp * 7 + n_exp * (MOE_CHUNK + tp - 1)
    p_rows = (worst + tp - 1) // tp * tp
    h, sel, gate, rank, cnt = _route(x, norm_w, w_router, tm=tm)
    plan = _moe_plan(cnt[:, :, 0].astype(jnp.int32), tm=tm, tp=tp, p_rows=p_rows)
    xs = _dispatch(h, sel, rank, plan, tm=tm, p_rows=p_rows)
    ys = _gffn(xs, wg, wu, wd, layer, plan[4], plan[5], tp=tp, tf=tf)
    pos_tok = jnp.where(sel > 0.0, rank, -1.0).T
    return _combine(x, ys, pos_tok, gate.T, plan, final_w, tm=tm)


def _pad_rows(x, rows):
    if x.shape[0] == rows:
        return x
    return jnp.concatenate([x, jnp.zeros((rows - x.shape[0], x.shape[1]), x.dtype)], axis=0)


def _chunk_masks(rows):
    ri = lax.broadcasted_iota(jnp.int32, (rows, rows), 0)
    ci = lax.broadcasted_iota(jnp.int32, (rows, rows), 1)
    same = (ri // CHUNK) == (ci // CHUNK)
    return same & (ci <= ri), same


def _gla_kernel(*refs, rb, zero_init, has_alias):
    it = iter(refs)
    q_ref, k_ref, v_ref, g_ref, gk_ref = next(it), next(it), next(it), next(it), next(it)
    s0_ref = None if zero_init else next(it)
    wgk_ref, bgk_ref, nw_ref = next(it), next(it), next(it)
    if has_alias:
        next(it)
    o_ref, s_ref, st_ref = next(it), next(it), next(it)
    r = pl.program_id(1)
    rows = max(rb, CHUNK)
    n_chunks = rows // CHUNK
    hk = q_ref.shape[1] // GLA_HEADS
    hv = v_ref.shape[1] // GLA_HEADS

    @pl.when(r == 0)
    def _():
        for h in range(GLA_HEADS):
            if zero_init:
                st_ref[h] = jnp.zeros(st_ref.shape[1:], F32)
            else:
                st_ref[h] = s0_ref[0, h].T

    q = _pad_rows(q_ref[...] * hk ** -0.5, rows)
    k = _pad_rows(k_ref[...], rows)
    v = _pad_rows(v_ref[...], rows)
    la = _log_sigmoid(_bdot(gk_ref[...], wgk_ref[...]) + bgk_ref[...]) * (1.0 / GLA_GATE_NORMALIZER)
    la = _pad_rows(la, rows)
    causal, same = _chunk_masks(rows)
    gc = _dot_exact_lhs(causal, la)
    gt = _dot_exact_lhs(same, la)
    q_in = q * jnp.exp(gc)
    k_in = k * jnp.exp(-gc)
    k_out = k * jnp.exp(gt - gc)
    e_tot = jnp.exp(gt)
    tri = causal[:CHUNK, :CHUNK]
    for h in range(GLA_HEADS):
        st = st_ref[h]
        ks = slice(h * hk, (h + 1) * hk)
        vs = slice(h * hv, (h + 1) * hv)
        for c in range(n_chunks):
            rs = slice(c * CHUNK, (c + 1) * CHUNK)
            qi, ki, ko, vh = q_in[rs, ks], k_in[rs, ks], k_out[rs, ks], v[rs, vs]
            intra = jnp.where(tri, _bdot_nt(qi, ki), 0.0)
            o = _bdot(intra, vh) + _bdot_nt(qi, st)
            st = st * e_tot[c * CHUNK:c * CHUNK + 1, ks] + _bdot_tn(vh, ko)
            n_out = min(rb, CHUNK)
            og = _rms(o[:n_out], nw_ref[...]) * _silu(g_ref[c * CHUNK:c * CHUNK + n_out, vs])
            o_ref[c * CHUNK:c * CHUNK + n_out, vs] = og
        st_ref[h] = st

    @pl.when(r == pl.num_programs(1) - 1)
    def _():
        for h in range(GLA_HEADS):
            s_ref[0, h] = st_ref[h].T


def _gla_core(proj, s0, w_gk2, b_gk2, norm_w, o_full, *, n_seq, seq_len, row0, rb):
    t = proj.shape[0]
    dk = w_gk2.shape[1]
    dv = 2 * dk
    hk, hv = dk // GLA_HEADS, dv // GLA_HEADS
    nblk = seq_len // rb
    base = row0 // rb
    assert row0 % rb == 0 and seq_len % rb == 0

    def rowmap(col):
        return lambda b, r: (base + b * nblk + r, col)

    in_specs = [pl.BlockSpec((rb, dk), rowmap(0)), pl.BlockSpec((rb, dk), rowmap(1)),
                pl.BlockSpec((rb, dv), rowmap(1)), pl.BlockSpec((rb, dv), rowmap(2)),
                pl.BlockSpec((rb, LANES), rowmap((2 * dk + 2 * dv) // LANES))]
    args = [proj, proj, proj, proj, proj]
    if s0 is not None:
        in_specs.append(pl.BlockSpec((1, GLA_HEADS, hk, hv), lambda b, r: (b, 0, 0, 0)))
        args.append(s0)
    wgk = jnp.pad(w_gk2, ((0, LANES - w_gk2.shape[0]), (0, 0)))
    in_specs += [pl.BlockSpec((LANES, dk), lambda b, r: (0, 0)),
                 pl.BlockSpec((1, dk), lambda b, r: (0, 0)),
                 pl.BlockSpec((1, hv), lambda b, r: (0, 0))]
    args += [wgk, b_gk2.reshape(1, dk), norm_w.reshape(1, hv)]
    aliases = {}
    if o_full is not None:
        in_specs.append(pl.BlockSpec(memory_space=pl.ANY))
        aliases = {len(args): 0}
        args.append(o_full)
    kern = functools.partial(_gla_kernel, rb=rb, zero_init=s0 is None, has_alias=o_full is not None)
    return pl.pallas_call(
        kern,
        grid=(n_seq, nblk),
        in_specs=in_specs,
        out_specs=[pl.BlockSpec((rb, dv), rowmap(0)),
                   pl.BlockSpec((1, GLA_HEADS, hk, hv), lambda b, r: (b, 0, 0, 0))],
        out_shape=[jax.ShapeDtypeStruct((t, dv), F32),
                   jax.ShapeDtypeStruct((n_seq, GLA_HEADS, hk, hv), F32)],
        scratch_shapes=[pltpu.VMEM((GLA_HEADS, hv, hk), F32)],
        input_output_aliases=aliases,
        compiler_params=_cparams("parallel", "arbitrary"),
        name="gla_core",
    )(*args)


def _tile(n, pref):
    if n <= pref:
        return n
    for c in range(pref, 7, -8):
        if n % c == 0:
            return c
    return n


def _pad_cols(w, n):
    return jnp.pad(w, ((0, 0), (0, n - w.shape[1])))


def _gla_layer(x, nw, s0_s, w_in, w_gk2, b_gk2, norm_w, w_out, dims):
    n_p, t_p, n_s, t_s = dims
    t = x.shape[0]
    tm = _tile(t, 1280)
    dk = w_gk2.shape[1]
    width = 6 * dk + LANES
    proj = _linear(x, _pad_cols(w_in, width), norm_w=nw, tm=tm, tn=_tile(width, 640), name="gla_in")
    o, sp = _gla_core(proj, None, w_gk2, b_gk2, norm_w, None, n_seq=n_p, seq_len=t_p, row0=0,
                      rb=min(t_p, 256))
    o, ss = _gla_core(proj, s0_s, w_gk2, b_gk2, norm_w, o, n_seq=n_s, seq_len=t_s, row0=n_p * t_p,
                      rb=t_s)
    x = _linear(o, w_out, res=x, tm=tm, tn=512, name="gla_out")
    return x, sp, ss


def _rope_tables(pos, half):
    freqs = np.exp(-math.log(ROPE_THETA) * np.arange(half, dtype=np.float64) / half)
    ang = np.asarray(pos, np.float64)[:, None] * freqs[None, :]
    cos, sin = np.cos(ang), np.sin(ang)
    return (jnp.asarray(np.concatenate([cos, cos], axis=-1), F32),
            jnp.asarray(np.concatenate([-sin, sin], axis=-1), F32))


def _swap_halves(w, axis=-1):
    a, b = jnp.split(w, 2, axis=axis)
    return jnp.concatenate([b, a], axis=axis)


def _mla_in_kernel(x_ref, nw_ref, w_ref, kvw_ref, cos_ref, sin_ref, cq_ref, ckv_ref, kr_ref):
    h = _rms(x_ref[...], nw_ref[...])
    y = _bdot(h, w_ref[...])
    cq_ref[...] = y[:, :MLA_Q_LORA]
    ckv_ref[...] = _rms(y[:, MLA_Q_LORA:MLA_Q_LORA + MLA_KV_LORA], kvw_ref[...])
    o = MLA_Q_LORA + MLA_KV_LORA
    kr_ref[...] = (y[:, o:o + MLA_ROPE] * cos_ref[...]
                   + y[:, o + LANES:o + LANES + MLA_ROPE] * sin_ref[...])


def _mla_in(x, nw, w_in, kv_norm_w, cos, sin, *, tm):
    t, d = x.shape
    o = MLA_Q_LORA + MLA_KV_LORA
    kr_w = w_in[:, o:o + MLA_ROPE]
    w_aug = jnp.concatenate([w_in[:, :o], _pad_cols(kr_w, LANES), _pad_cols(_swap_halves(kr_w), LANES)],
                            axis=1)
    wid = w_aug.shape[1]
    row = lambda i: (i, 0)
    fix = lambda i: (0, 0)
    return pl.pallas_call(
        _mla_in_kernel,
        grid=(t // tm,),
        in_specs=[pl.BlockSpec((tm, d), row), pl.BlockSpec((1, d), fix), pl.BlockSpec((d, wid), fix),
                  pl.BlockSpec((1, MLA_KV_LORA), fix), pl.BlockSpec((tm, MLA_ROPE), row),
                  pl.BlockSpec((tm, MLA_ROPE), row)],
        out_specs=[pl.BlockSpec((tm, MLA_Q_LORA), row), pl.BlockSpec((tm, MLA_KV_LORA), row),
                   pl.BlockSpec((tm, MLA_ROPE), row)],
        out_shape=[jax.ShapeDtypeStruct((t, MLA_Q_LORA), F32), jax.ShapeDtypeStruct((t, MLA_KV_LORA), F32),
                   jax.ShapeDtypeStruct((t, MLA_ROPE), F32)],
        compiler_params=_cparams("parallel"),
        name="mla_in",
    )(x, nw.reshape(1, d), w_aug, kv_norm_w.reshape(1, -1), cos, sin)


def _mla_q_kernel(cq_ref, qw_ref, wn_ref, wr_ref, ws_ref, wk_ref, cos_ref, sin_ref, ql_ref, qr_ref):
    cq = _rms(cq_ref[...], qw_ref[...]).astype(BF16)
    qn = jnp.dot(cq, wn_ref[...].astype(BF16), preferred_element_type=F32).astype(BF16)
    for j in range(MLA_HEADS // 2):
        ql = jnp.dot(qn[:, j * LANES:(j + 1) * LANES], wk_ref[j].astype(BF16), preferred_element_type=F32)
        ql_ref[2 * j] = ql[:, :MLA_KV_LORA]
        ql_ref[2 * j + 1] = ql[:, MLA_KV_LORA:]
    qr = jnp.dot(cq, wr_ref[...].astype(BF16), preferred_element_type=F32)
    qs = jnp.dot(cq, ws_ref[...].astype(BF16), preferred_element_type=F32)
    per = LANES // MLA_ROPE
    cos = jnp.concatenate([cos_ref[...]] * (MLA_HEADS // per), axis=-1)
    sin = jnp.concatenate([sin_ref[...]] * (MLA_HEADS // per), axis=-1)
    rot = qr * cos + qs * sin
    for h in range(MLA_HEADS):
        qr_ref[h] = rot[:, h * MLA_ROPE:(h + 1) * MLA_ROPE]


def _mla_q(cq, q_norm_w, w_uq, w_uk, cos, sin, *, tm):
    t = cq.shape[0]
    per = LANES // MLA_ROPE
    w3 = w_uq.reshape(MLA_Q_LORA, MLA_HEADS, MLA_NOPE + MLA_ROPE)
    w_nope = w3[:, :, :MLA_NOPE].reshape(MLA_Q_LORA, MLA_HEADS * MLA_NOPE)
    w_rope = w3[:, :, MLA_NOPE:].reshape(MLA_Q_LORA, MLA_HEADS * MLA_ROPE)
    w_swap = _swap_halves(w3[:, :, MLA_NOPE:]).reshape(MLA_Q_LORA, MLA_HEADS * MLA_ROPE)
    a = jnp.transpose(w_uk, (1, 2, 0))
    z = jnp.zeros_like(a[0::2])
    w_bd = jnp.concatenate([jnp.concatenate([a[0::2], z], axis=2),
                            jnp.concatenate([z, a[1::2]], axis=2)], axis=1)
    cos4 = jnp.tile(cos, (1, per))
    sin4 = jnp.tile(sin, (1, per))
    row = lambda i: (i, 0)
    fix2 = lambda i: (0, 0)
    return pl.pallas_call(
        _mla_q_kernel,
        grid=(t // tm,),
        in_specs=[pl.BlockSpec((tm, MLA_Q_LORA), row), pl.BlockSpec((1, MLA_Q_LORA), fix2),
                  pl.BlockSpec(w_nope.shape, fix2), pl.BlockSpec(w_rope.shape, fix2),
                  pl.BlockSpec(w_swap.shape, fix2), pl.BlockSpec(w_bd.shape, lambda i: (0, 0, 0)),
                  pl.BlockSpec((tm, LANES), row), pl.BlockSpec((tm, LANES), row)],
        out_specs=[pl.BlockSpec((MLA_HEADS, tm, MLA_KV_LORA), lambda i: (0, i, 0)),
                   pl.BlockSpec((MLA_HEADS, tm, MLA_ROPE), lambda i: (0, i, 0))],
        out_shape=[jax.ShapeDtypeStruct((MLA_HEADS, t, MLA_KV_LORA), F32),
                   jax.ShapeDtypeStruct((MLA_HEADS, t, MLA_ROPE), F32)],
        compiler_params=_cparams("parallel"),
        name="mla_q",
    )(cq, q_norm_w.reshape(1, -1), w_nope, w_rope, w_swap, w_bd, cos4, sin4)


MLA_QSCALE = (MLA_NOPE + MLA_ROPE) ** -0.5 * math.log2(math.e)


def _lane_repeat(x, width):
    return jnp.concatenate([x] * (width // LANES), axis=1)


def _flash_chunk(s, cb, m_ref, l_ref, acc_ref, rs):
    m_prev = m_ref[rs]
    m_new = jnp.maximum(m_prev, jnp.max(s, axis=-1, keepdims=True))
    alpha = jnp.exp2(m_prev - m_new)
    p = jnp.exp2(s - _lane_repeat(m_new, s.shape[1]))
    l_ref[rs] = alpha * l_ref[rs] + jnp.sum(p, axis=-1, keepdims=True)
    acc_ref[rs] = (_lane_repeat(alpha, acc_ref.shape[1]) * acc_ref[rs]
                   + jnp.dot(p.astype(BF16), cb, preferred_element_type=F32))
    m_ref[rs] = m_new


def _flash_init(ql_ref, qr_ref, qlb_ref, qrb_ref, m_ref, l_ref, acc_ref):
    rows = qlb_ref.shape[0]
    qlb_ref[...] = (ql_ref[...].reshape(rows, MLA_KV_LORA) * MLA_QSCALE).astype(BF16)
    qrb_ref[...] = (qr_ref[...].reshape(rows, MLA_ROPE) * MLA_QSCALE).astype(BF16)
    m_ref[...] = jnp.full(m_ref.shape, NEG_INF, F32)
    l_ref[...] = jnp.zeros(l_ref.shape, F32)
    acc_ref[...] = jnp.zeros(acc_ref.shape, F32)


def _mla_finish(acc_ref, l_ref, wv_ref, o_ref, rows_per_head):
    inv = _lane_repeat(1.0 / l_ref[...], acc_ref.shape[1])
    outs = []
    for j in range(MLA_HEADS // 2):
        pair = None
        for h in (2 * j, 2 * j + 1):
            rs = slice(h * rows_per_head, (h + 1) * rows_per_head)
            part = _bdot(acc_ref[rs] * inv[rs], wv_ref[h])
            pair = part if pair is None else pair + part
        outs.append(pair)
    o_ref[...] = jnp.concatenate(outs, axis=-1)


def _mla_prompt_kernel(ql_ref, qr_ref, c_ref, r_ref, wv_ref, o_ref, qlb_ref, qrb_ref, m_ref, l_ref, acc_ref,
                       *, tq, tk, rc):
    qi = pl.program_id(1)
    kj = pl.program_id(2)
    last = (qi * tq + tq - 1) // tk
    rows = MLA_HEADS * tq

    @pl.when(kj == 0)
    def _():
        _flash_init(ql_ref, qr_ref, qlb_ref, qrb_ref, m_ref, l_ref, acc_ref)

    def step(masked):
        cb = c_ref[...].astype(BF16)
        rb = r_ref[...].astype(BF16)
        chunks = [slice(ch * rc, (ch + 1) * rc) for ch in range(rows // rc)]
        scores = [_bdot_nt(qlb_ref[rs], cb) + _bdot_nt(qrb_ref[rs], rb) for rs in chunks]
        if masked:
            q_pos = qi * tq + lax.broadcasted_iota(jnp.int32, (rc, tk), 0) % tq
            k_pos = kj * tk + lax.broadcasted_iota(jnp.int32, (rc, tk), 1)
            scores = [jnp.where(k_pos <= q_pos, s, NEG_INF) for s in scores]
        probs, alphas = [], []
        for rs, s in zip(chunks, scores):
            m_prev = m_ref[rs]
            m_new = jnp.maximum(m_prev, jnp.max(s, axis=-1, keepdims=True))
            alpha = jnp.exp2(m_prev - m_new)
            p = jnp.exp2(s - _lane_repeat(m_new, tk))
            l_ref[rs] = alpha * l_ref[rs] + jnp.sum(p, axis=-1, keepdims=True)
            m_ref[rs] = m_new
            probs.append(p.astype(BF16))
            alphas.append(alpha)
        for rs, p, alpha in zip(chunks, probs, alphas):
            acc_ref[rs] = (_lane_repeat(alpha, MLA_KV_LORA) * acc_ref[rs]
                           + jnp.dot(p, cb, preferred_element_type=F32))

    @pl.when(kj < last)
    def _():
        step(False)

    @pl.when(kj == last)
    def _():
        step(True)
        _mla_finish(acc_ref, l_ref, wv_ref, o_ref, tq)


def _pad_uv(w_uv):
    a = jnp.transpose(w_uv, (1, 0, 2))
    z = jnp.zeros_like(a)
    even = (jnp.arange(a.shape[0]) % 2 == 0)[:, None, None]
    return jnp.concatenate([jnp.where(even, a, z), jnp.where(even, z, a)], axis=2)


def _mla_prompt_attn(q_lat, q_rope, c_kv, k_r, w_uv_pad, *, n_seq, seq_len, tq, tk):
    t = c_kv.shape[0]
    nq, nk = seq_len // tq, seq_len // tk

    def qmap(b, i, j):
        return (0, b * nq + i, 0)

    def kmap(b, i, j):
        return (b * nk + jnp.minimum(j, (i * tq + tq - 1) // tk), 0)

    rows = MLA_HEADS * tq
    assert tk % tq == 0
    return pl.pallas_call(
        functools.partial(_mla_prompt_kernel, tq=tq, tk=tk, rc=2 * tq),
        grid=(n_seq, nq, nk),
        in_specs=[pl.BlockSpec((MLA_HEADS, tq, MLA_KV_LORA), qmap),
                  pl.BlockSpec((MLA_HEADS, tq, MLA_ROPE), qmap),
                  pl.BlockSpec((tk, MLA_KV_LORA), kmap), pl.BlockSpec((tk, MLA_ROPE), kmap),
                  pl.BlockSpec(w_uv_pad.shape, lambda b, i, j: (0, 0, 0))],
        out_specs=pl.BlockSpec((tq, MLA_HEADS * MLA_V), lambda b, i, j: (b * nq + i, 0)),
        out_shape=jax.ShapeDtypeStruct((t, MLA_HEADS * MLA_V), F32),
        scratch_shapes=_flash_scratch(rows),
        compiler_params=_cparams("parallel", "parallel", "arbitrary"),
        name="mla_prompt_attn",
    )(q_lat, q_rope, c_kv, k_r, w_uv_pad)


def _flash_scratch(rows):
    return [pltpu.VMEM((rows, MLA_KV_LORA), BF16), pltpu.VMEM((rows, MLA_ROPE), BF16),
            pltpu.VMEM((rows, LANES), F32), pltpu.VMEM((rows, LANES), F32),
            pltpu.VMEM((rows, MLA_KV_LORA), F32)]


def _mla_sample_kernel(*refs, n_pg, t_s):
    pt_ref = refs[0]
    ql_ref, qr_ref = refs[1], refs[2]
    lat_refs = refs[3:3 + n_pg]
    kr_refs = refs[3 + n_pg:3 + 2 * n_pg]
    cn_ref, rn_ref, wv_ref = refs[3 + 2 * n_pg:6 + 2 * n_pg]
    o_ref, qlb_ref, qrb_ref, m_ref, l_ref, acc_ref = refs[7 + 2 * n_pg:]
    del pt_ref
    kj = pl.program_id(1)
    n_steps = pl.num_programs(1)
    rows = MLA_HEADS * t_s
    everything = slice(0, rows)

    @pl.when(kj == 0)
    def _():
        _flash_init(ql_ref, qr_ref, qlb_ref, qrb_ref, m_ref, l_ref, acc_ref)

    @pl.when(kj < n_steps - 1)
    def _():
        cb = jnp.concatenate([ref[...].astype(BF16) for ref in lat_refs], axis=0)
        rbt = jnp.concatenate([ref[...].astype(BF16) for ref in kr_refs], axis=1)
        s = _bdot_nt(qlb_ref[...], cb) + jnp.dot(qrb_ref[...], rbt, preferred_element_type=F32)
        _flash_chunk(s, cb, m_ref, l_ref, acc_ref, everything)

    @pl.when(kj == n_steps - 1)
    def _():
        cb = _pad_rows(cn_ref[...], LANES).astype(BF16)
        rb = _pad_rows(rn_ref[...], LANES).astype(BF16)
        s = _bdot_nt(qlb_ref[...], cb) + _bdot_nt(qrb_ref[...], rb)
        q_t = lax.broadcasted_iota(jnp.int32, (rows, LANES), 0) % t_s
        k_t = lax.broadcasted_iota(jnp.int32, (rows, LANES), 1)
        s = jnp.where(k_t <= q_t, s, NEG_INF)
        _flash_chunk(s, cb, m_ref, l_ref, acc_ref, everything)
        _mla_finish(acc_ref, l_ref, wv_ref, o_ref, t_s)


def _mla_sample_attn(q_lat, q_rope, c_kv, k_r, cache_lat, cache_kr_t, page_table, w_uv_pad, o_full, *,
                     n_seq, t_s, row0, n_pg):
    n_pages = page_table.shape[1]
    page = cache_lat.shape[2]
    assert n_pages % n_pg == 0 and row0 % t_s == 0
    n_steps = n_pages // n_pg + 1
    base = row0 // t_s

    def qmap(b, j, pt):
        return (0, base + b, 0)

    def newmap(b, j, pt):
        return (base + b, 0)

    def pagemap(p):
        return lambda b, j, pt: (0, pt[b, jnp.minimum(j * n_pg + p, n_pages - 1)], 0, 0)

    in_specs = [pl.BlockSpec((MLA_HEADS, t_s, MLA_KV_LORA), qmap),
                pl.BlockSpec((MLA_HEADS, t_s, MLA_ROPE), qmap)]
    in_specs += [pl.BlockSpec((None, None, page, MLA_KV_LORA), pagemap(p)) for p in range(n_pg)]
    in_specs += [pl.BlockSpec((None, None, MLA_ROPE, page), pagemap(p)) for p in range(n_pg)]
    in_specs += [pl.BlockSpec((t_s, MLA_KV_LORA), newmap), pl.BlockSpec((t_s, MLA_ROPE), newmap),
                 pl.BlockSpec(w_uv_pad.shape, lambda b, j, pt: (0, 0, 0)),
                 pl.BlockSpec(memory_space=pl.ANY)]
    n_in = len(in_specs)
    return pl.pallas_call(
        functools.partial(_mla_sample_kernel, n_pg=n_pg, t_s=t_s),
        grid_spec=pltpu.PrefetchScalarGridSpec(
            num_scalar_prefetch=1,
            grid=(n_seq, n_steps),
            in_specs=in_specs,
            out_specs=pl.BlockSpec((t_s, MLA_HEADS * MLA_V), newmap),
            scratch_shapes=_flash_scratch(MLA_HEADS * t_s)),
        out_shape=jax.ShapeDtypeStruct(o_full.shape, F32),
        input_output_aliases={n_in: 0},
        compiler_params=_cparams("parallel", "arbitrary"),
        name="mla_sample_attn",
    )(page_table, q_lat, q_rope, *([cache_lat] * n_pg), *([cache_kr_t] * n_pg), c_kv, k_r, w_uv_pad, o_full)


def _positions(dims, past_len):
    n_p, t_p, n_s, t_s = dims
    return np.concatenate([np.tile(np.arange(t_p), n_p), np.tile(past_len + np.arange(t_s), n_s)])


def _mla_layer(x, nw, cache_lat, cache_kr, page_table, w_in, q_norm_w, w_uq, kv_norm_w, w_uk, w_uv, w_out,
               dims):
    n_p, t_p, n_s, t_s = dims
    t = x.shape[0]
    n_pages = page_table.shape[1]
    past_len = n_pages * cache_lat.shape[2]
    cos, sin = _rope_tables(_positions(dims, past_len), MLA_ROPE // 2)
    cq, ckv, kr = _mla_in(x, nw, w_in, kv_norm_w, cos, sin, tm=_tile(t, 640))
    q_lat, q_rope = _mla_q(cq, q_norm_w, w_uq, w_uk, cos, sin, tm=_tile(t, 256))
    wv = _pad_uv(w_uv)
    o = _mla_prompt_attn(q_lat, q_rope, ckv, kr, wv, n_seq=n_p, seq_len=t_p, tq=min(t_p, 128),
                         tk=min(t_p, 256))
    o = _mla_sample_attn(q_lat, q_rope, ckv, kr, cache_lat, jnp.swapaxes(cache_kr, 2, 3), page_table, wv, o,
                         n_seq=n_s, t_s=t_s, row0=n_p * t_p, n_pg=math.gcd(n_pages, 16))
    x = _linear(o, w_out, res=x, tm=_tile(t, 1280), tn=512, name="mla_out")
    return x, ckv, kr


def _lane_halves(a):
    half = LANES // 2
    low = lax.broadcasted_iota(jnp.int32, a.shape, 1) < half
    rolled = pltpu.roll(a, half, axis=1)
    head0 = (jnp.where(low, a, 0.0), jnp.where(low, 0.0, rolled))
    head1 = (jnp.where(low, rolled, 0.0), jnp.where(low, 0.0, a))
    return head0, head1


def _swa_heads(q, k_all, v_all, sink_ref, mask, o_ref):
    rq = q.shape[0]
    scale = SWA_HD ** -0.5
    top = lax.broadcasted_iota(jnp.int32, (2 * rq, 1), 0) < rq
    for cg in range(SWA_KV_HEADS // 2):
        k_heads = _lane_halves(k_all[:, cg * LANES:(cg + 1) * LANES])
        v_heads = _lane_halves(v_all[:, cg * LANES:(cg + 1) * LANES])
        for sub in range(2):
            kh = 2 * cg + sub
            (k_lo, k_hi), (v_lo, v_hi) = k_heads[sub], v_heads[sub]
            qs = jnp.concatenate([q[:, (2 * kh) * LANES:(2 * kh + 1) * LANES],
                                  q[:, (2 * kh + 1) * LANES:(2 * kh + 2) * LANES]], axis=0)
            acc = None
            for which, (kk, vv) in enumerate(((k_lo, v_lo), (k_hi, v_hi))):
                s = jnp.where(mask, _bdot_nt(qs, kk) * scale, NEG_INF)
                sink = jnp.where(top, sink_ref[4 * kh + which], sink_ref[4 * kh + 2 + which])
                m = jnp.maximum(jnp.max(s, axis=-1, keepdims=True), sink)
                e = jnp.exp(s - m)
                p = e / (jnp.sum(e, axis=-1, keepdims=True) + jnp.exp(sink - m))
                part = _bdot(p, vv)
                acc = part if acc is None else acc + part
            o_ref[:, (2 * kh) * LANES:(2 * kh + 1) * LANES] = acc[:rq]
            o_ref[:, (2 * kh + 1) * LANES:(2 * kh + 2) * LANES] = acc[rq:]


def _swa_prompt_kernel(sink_ref, q_ref, kp_ref, kc_ref, vp_ref, vc_ref, o_ref):
    n = pl.program_id(1)
    w = q_ref.shape[0]
    k_all = jnp.concatenate([kp_ref[...], kc_ref[...]], axis=0)
    v_all = jnp.concatenate([vp_ref[...], vc_ref[...]], axis=0)
    r = lax.broadcasted_iota(jnp.int32, (2 * w, 2 * w), 0) % w
    c = lax.broadcasted_iota(jnp.int32, (2 * w, 2 * w), 1)
    mask = (c >= r) & (c <= r + w) & ((n > 0) | (c >= w))
    _swa_heads(q_ref[...], k_all, v_all, sink_ref, mask, o_ref)


def _swa_prompt_attn(qkv, sinks, *, n_seq, seq_len):
    t = qkv.shape[0]
    w = WINDOW
    nb = seq_len // w
    dq = SWA_HEADS * SWA_HD
    dkv = SWA_KV_HEADS * SWA_HD
    kcol = dq // dkv
    cur = lambda col: (lambda b, n: (b * nb + n, col))
    prev = lambda col: (lambda b, n: (b * nb + jnp.maximum(n - 1, 0), col))
    return pl.pallas_call(
        _swa_prompt_kernel,
        grid=(n_seq, nb),
        in_specs=[pl.BlockSpec(memory_space=pltpu.SMEM),
                  pl.BlockSpec((w, dq), cur(0)),
                  pl.BlockSpec((w, dkv), prev(kcol)), pl.BlockSpec((w, dkv), cur(kcol)),
                  pl.BlockSpec((w, dkv), prev(kcol + 1)), pl.BlockSpec((w, dkv), cur(kcol + 1))],
        out_specs=pl.BlockSpec((w, dq), cur(0)),
        out_shape=jax.ShapeDtypeStruct((t, dq), F32),
        compiler_params=_cparams("parallel", "parallel"),
        name="swa_prompt_attn",
    )(sinks, qkv, qkv, qkv, qkv, qkv)


def _swa_sample_kernel(sink_ref, q_ref, kn_ref, vn_ref, kc_ref, vc_ref, alias_ref, o_ref, ko_ref, vo_ref):
    del alias_ref
    t_s = q_ref.shape[0]
    w = kc_ref.shape[0]
    k_all = jnp.concatenate([kc_ref[...], kn_ref[...]], axis=0)
    v_all = jnp.concatenate([vc_ref[...], vn_ref[...]], axis=0)
    r = lax.broadcasted_iota(jnp.int32, (2 * t_s, w + t_s), 0) % t_s
    c = lax.broadcasted_iota(jnp.int32, (2 * t_s, w + t_s), 1)
    mask = (c <= w + r) & (c >= r)
    _swa_heads(q_ref[...], k_all, v_all, sink_ref, mask, o_ref)
    ko_ref[...] = k_all[t_s:]
    vo_ref[...] = v_all[t_s:]


def _swa_sample_attn(qkv, cache_k, cache_v, sinks, o_full, *, n_seq, t_s, row0):
    w = cache_k.shape[1]
    dq = SWA_HEADS * SWA_HD
    dkv = SWA_KV_HEADS * SWA_HD
    kcol = dq // dkv
    base = row0 // t_s
    new = lambda col: (lambda b: (base + b, col))
    seq = lambda b: (b, 0, 0)
    return pl.pallas_call(
        _swa_sample_kernel,
        grid=(n_seq,),
        in_specs=[pl.BlockSpec(memory_space=pltpu.SMEM),
                  pl.BlockSpec((t_s, dq), new(0)),
                  pl.BlockSpec((t_s, dkv), new(kcol)), pl.BlockSpec((t_s, dkv), new(kcol + 1)),
                  pl.BlockSpec((None, w, dkv), seq), pl.BlockSpec((None, w, dkv), seq),
                  pl.BlockSpec(memory_space=pl.ANY)],
        out_specs=[pl.BlockSpec((t_s, dq), new(0)),
                   pl.BlockSpec((None, w, dkv), seq), pl.BlockSpec((None, w, dkv), seq)],
        out_shape=[jax.ShapeDtypeStruct(o_full.shape, F32),
                   jax.ShapeDtypeStruct(cache_k.shape, F32), jax.ShapeDtypeStruct(cache_v.shape, F32)],
        input_output_aliases={6: 0},
        compiler_params=_cparams("parallel"),
        name="swa_sample_attn",
    )(sinks, qkv, qkv, qkv, cache_k, cache_v, o_full)


def _swa_layer(x, nw, cache_k, cache_v, w_qkv, b_qkv, sinks, w_out, b_out, dims):
    n_p, t_p, n_s, t_s = dims
    t = x.shape[0]
    tm = _tile(t, 1280)
    dq = SWA_HEADS * SWA_HD
    dkv = SWA_KV_HEADS * SWA_HD
    qkv = _linear(x, w_qkv, norm_w=nw, bias=b_qkv, tm=tm, tn=512, name="swa_in")
    o = _swa_prompt_attn(qkv, sinks, n_seq=n_p, seq_len=t_p)
    o, k_s, v_s = _swa_sample_attn(qkv, cache_k.reshape(n_s, WINDOW, dkv), cache_v.reshape(n_s, WINDOW, dkv),
                                   sinks, o, n_seq=n_s, t_s=t_s, row0=n_p * t_p)
    x = _linear(o, w_out, bias=b_out, res=x, tm=tm, tn=512, name="swa_out")
    kv_p = jnp.stack([lax.slice(qkv, ((b + 1) * t_p - WINDOW, dq), ((b + 1) * t_p, dq + 2 * dkv))
                      for b in range(n_p)])
    kv_shape = (n_p, WINDOW, SWA_KV_HEADS, SWA_HD)
    k_p = kv_p[:, :, :dkv].reshape(kv_shape)
    v_p = kv_p[:, :, dkv:].reshape(kv_shape)
    return x, k_p, v_p, k_s.reshape(cache_k.shape), v_s.reshape(cache_v.shape)


def _l2norm(x):
    return x * lax.rsqrt(jnp.sum(x * x, axis=-1, keepdims=True) + 1e-6)


def _split2(x):
    h1 = x.astype(BF16)
    return h1, (x - h1.astype(F32)).astype(BF16)


def _bmm3(a, b):
    a1, a2 = _split2(a)
    b1, b2 = _split2(b)
    dot = lambda x, y: jnp.einsum("bij,bjk->bik", x, y, preferred_element_type=F32)
    return dot(a1, b1) + dot(a1, b2) + dot(a2, b1)


def _unit_lower_inverse(low):
    n = low.shape[-1]
    eye = (lax.broadcasted_iota(jnp.int32, (n, n), 0) == lax.broadcasted_iota(jnp.int32, (n, n), 1))
    eye = eye.astype(F32)[None]
    power = -low
    inv = eye + power
    steps = int(math.log2(n)) - 1
    for _ in range(steps):
        power = _bmm3(power, power)
        inv = inv + _bmm3(inv, power)
    return inv


def _dn_kernel(*refs, rb, zero_init, has_alias):
    it = iter(refs)
    x_ref, z_ref, ab_ref = next(it), next(it), next(it)
    s0_ref, c0_ref = (None, None) if zero_init else (next(it), next(it))
    cw_ref, al_ref, dt_ref, nw_ref = next(it), next(it), next(it), next(it)
    if has_alias:
        next(it)
    o_ref, s_ref, co_ref = next(it), next(it), next(it)
    st_ref, xp_ref = next(it), next(it)
    r = pl.program_id(1)
    rows = max(rb, CHUNK)
    n_chunks = rows // CHUNK
    halo = 8
    dqk = DN_HEADS * DN_HK

    @pl.when(r == 0)
    def _():
        if zero_init:
            st_ref[...] = jnp.zeros(st_ref.shape, F32)
            xp_ref[0:halo] = jnp.zeros((halo, xp_ref.shape[1]), F32)
        else:
            st_ref[...] = s0_ref[0]
            xp_ref[0:halo] = c0_ref[0]

    xp_ref[halo:halo + rb] = x_ref[...]
    conv = xp_ref[halo - DN_CONV + 1:halo - DN_CONV + 1 + rb] * cw_ref[0:1]
    for w in range(1, DN_CONV):
        conv = conv + xp_ref[halo - DN_CONV + 1 + w:halo - DN_CONV + 1 + w + rb] * cw_ref[w:w + 1]
    tail = xp_ref[rb:rb + halo]
    co_ref[0] = tail
    xp_ref[0:halo] = tail
    qkv = _pad_rows(_silu(conv), rows)
    ab = ab_ref[...]
    g_all = _pad_rows(-jnp.exp(al_ref[...]) * _softplus(ab + dt_ref[...]), rows)
    beta_all = _pad_rows(1.0 / (1.0 + jnp.exp(-ab)), rows)

    ri = lax.broadcasted_iota(jnp.int32, (rows, rows), 0)
    ci = lax.broadcasted_iota(jnp.int32, (rows, rows), 1)
    same = (ri // CHUNK) == (ci // CHUNK)
    causal = same & (ci <= ri)
    strict = same & (ci < ri)
    upper = (same & (ri <= ci)).astype(BF16)
    gc_col = _dot_exact_lhs(causal, g_all)
    gt_col = _dot_exact_lhs(same, g_all)
    g1, g2, g3 = _split3(g_all)
    tn = lambda a: lax.dot_general(a, upper, (((0,), (0,)), ((), ())), preferred_element_type=F32)
    gc_row = tn(g1) + tn(g2) + tn(g3)

    lows, rhss, attns, qds, kos, gls = [], [], [], [], [], []
    for h in range(DN_HEADS):
        hs = slice(h * DN_HK, (h + 1) * DN_HK)
        q = _l2norm(qkv[:, hs]) * DN_HK ** -0.5
        k = _l2norm(qkv[:, dqk + h * DN_HK:dqk + (h + 1) * DN_HK])
        v = qkv[:, 2 * dqk + h * DN_HV:2 * dqk + (h + 1) * DN_HV]
        beta = beta_all[:, DN_HEADS + h:DN_HEADS + h + 1]
        gc = gc_col[:, h:h + 1]
        gt = gt_col[:, h:h + 1]
        decay = jnp.where(causal, jnp.exp(jnp.where(causal, gc - gc_row[h:h + 1, :], 0.0)), 0.0)
        kb = k * beta
        low = jnp.where(strict, _bdot_nt(kb, k) * decay, 0.0)
        attn = _bdot_nt(q, k) * decay
        rhs = jnp.concatenate([v * beta, kb * jnp.exp(gc)], axis=-1)
        q_dec = q * jnp.exp(gc)
        k_out = k * jnp.exp(gt - gc)
        g_last = jnp.exp(gt)
        for c in range(n_chunks):
            rs = slice(c * CHUNK, (c + 1) * CHUNK)
            lows.append(low[rs, rs])
            attns.append(attn[rs, rs])
            rhss.append(rhs[rs])
            qds.append(q_dec[rs])
            kos.append(k_out[rs])
            gls.append(g_last[c * CHUNK:c * CHUNK + 1])
    sol = _bmm3(_unit_lower_inverse(jnp.stack(lows)), jnp.stack(rhss))
    n_out = min(rb, CHUNK)
    for h in range(DN_HEADS):
        s = st_ref[h]
        for c in range(n_chunks):
            i = h * n_chunks + c
            u, wm = sol[i, :, :DN_HV], sol[i, :, DN_HV:]
            v_new = u - _bdot(wm, s)
            o = _bdot(qds[i], s) + _bdot(attns[i], v_new)
            s = s * gls[i] + _bdot_tn(kos[i], v_new)
            zs = z_ref[c * CHUNK:c * CHUNK + n_out, h * DN_HV:(h + 1) * DN_HV]
            o_ref[c * CHUNK:c * CHUNK + n_out, h * DN_HV:(h + 1) * DN_HV] = (
                _rms(o[:n_out], nw_ref[...]) * _silu(zs))
        st_ref[h] = s

    @pl.when(r == pl.num_programs(1) - 1)
    def _():
        s_ref[0] = st_ref[...]


def _dn_core(proj, s0, conv0, conv_w, a_log, dt_bias, norm_w, o_full, *, n_seq, seq_len, row0, rb):
    t = proj.shape[0]
    dconv = conv_w.shape[1]
    dz = DN_HEADS * DN_HV
    nblk = seq_len // rb
    base = row0 // rb
    assert row0 % rb == 0 and seq_len % rb == 0 and rb % 8 == 0

    def rowmap(col):
        return lambda b, r: (base + b * nblk + r, col)

    seq4 = lambda b, r: (b, 0, 0, 0)
    seq3 = lambda b, r: (b, 0, 0)
    fix = lambda b, r: (0, 0)
    in_specs = [pl.BlockSpec((rb, dconv), rowmap(0)), pl.BlockSpec((rb, dz), rowmap(dconv // dz)),
                pl.BlockSpec((rb, LANES), rowmap((dconv + dz) // LANES))]
    args = [proj, proj, proj]
    if s0 is not None:
        in_specs += [pl.BlockSpec((1, DN_HEADS, DN_HK, DN_HV), seq4), pl.BlockSpec((1, 8, dconv), seq3)]
        args += [s0, conv0]
    in_specs += [pl.BlockSpec((DN_CONV, dconv), fix), pl.BlockSpec((1, LANES), fix),
                 pl.BlockSpec((1, LANES), fix), pl.BlockSpec((1, DN_HV), fix)]
    args += [conv_w, _pad_cols(a_log.reshape(1, -1), LANES), _pad_cols(dt_bias.reshape(1, -1), LANES),
             norm_w.reshape(1, -1)]
    aliases = {}
    if o_full is not None:
        in_specs.append(pl.BlockSpec(memory_space=pl.ANY))
        aliases = {len(args): 0}
        args.append(o_full)
    kern = functools.partial(_dn_kernel, rb=rb, zero_init=s0 is None, has_alias=o_full is not None)
    return pl.pallas_call(
        kern,
        grid=(n_seq, nblk),
        in_specs=in_specs,
        out_specs=[pl.BlockSpec((rb, dz), rowmap(0)),
                   pl.BlockSpec((1, DN_HEADS, DN_HK, DN_HV), seq4),
                   pl.BlockSpec((1, 8, dconv), seq3)],
        out_shape=[jax.ShapeDtypeStruct((t, dz), F32),
                   jax.ShapeDtypeStruct((n_seq, DN_HEADS, DN_HK, DN_HV), F32),
                   jax.ShapeDtypeStruct((n_seq, 8, dconv), F32)],
        scratch_shapes=[pltpu.VMEM((DN_HEADS, DN_HK, DN_HV), F32), pltpu.VMEM((rb + 8, dconv), F32)],
        input_output_aliases=aliases,
        compiler_params=_cparams("parallel", "arbitrary"),
        name="dn_core",
    )(*args)


def _dn_layer(x, nw, s0_s, conv0_s, w_in, conv_w, a_log, dt_bias, norm_w, w_out, dims):
    n_p, t_p, n_s, t_s = dims
    t = x.shape[0]
    tm = _tile(t, 1280)
    dconv = conv_w.shape[1]
    dz = DN_HEADS * DN_HV
    width = dconv + dz + LANES
    proj = _linear(x, _pad_cols(w_in, width), norm_w=nw, tm=tm, tn=_tile(width, 384), name="dn_in")
    o, sp, cp = _dn_core(proj, None, None, conv_w, a_log, dt_bias, norm_w, None, n_seq=n_p, seq_len=t_p,
                         row0=0, rb=min(t_p, 128))
    conv0 = jnp.pad(conv0_s, ((0, 0), (8 - conv0_s.shape[1], 0), (0, 0)))
    o, ss, cs = _dn_core(proj, s0_s, conv0, conv_w, a_log, dt_bias, norm_w, o, n_seq=n_s, seq_len=t_s,
                         row0=n_p * t_p, rb=t_s)
    x = _linear(o, w_out, res=x, tm=tm, tn=512, name="dn_out")
    keep = DN_CONV - 1
    return x, sp, ss, cp[:, 8 - keep:], cs[:, 8 - keep:]


def kernel(x_prompt, x_sample, state_gla, cache_mla_latent, cache_mla_krope, cache_swa_k, cache_swa_v, state_delta, state_delta_conv, page_table, norm_w, final_norm_w, gla_w_in, gla_w_gk2, gla_b_gk2, gla_norm_w, gla_w_out, mla_w_in, mla_q_norm_w, mla_w_uq, mla_kv_norm_w, mla_w_uk, mla_w_uv, mla_w_out, swa_w_qkv, swa_b_qkv, swa_sinks, swa_w_out, swa_b_out, dn_w_in, dn_conv_w, dn_a_log, dn_dt_bias, dn_norm_w, dn_w_out, ffn_w_gate, ffn_w_up, ffn_w_down, moe_w_router, moe_w_gate, moe_w_up, moe_w_down):
    n_p, t_p, d = x_prompt.shape
    n_s, t_s, _ = x_sample.shape
    dims = (n_p, t_p, n_s, t_s)
    x = jnp.concatenate([x_prompt.reshape(n_p * t_p, d), x_sample.reshape(n_s * t_s, d)], axis=0)
    t = x.shape[0]
    tm = _tile(t, 1280)
    n_tp = n_p * t_p

    x, gla_p, gla_s = _gla_layer(x, norm_w[0, 0], state_gla[0], gla_w_in[0], gla_w_gk2[0], gla_b_gk2[0],
                                 gla_norm_w[0], gla_w_out[0], dims)
    x = _ffn(x, norm_w[0, 1], ffn_w_gate, ffn_w_up, ffn_w_down, 0, tm=tm, tf=256)
    moe_tiles = dict(tm=_tile(t, 640), tp=640, tf=512)

    x, ckv, kr = _mla_layer(x, norm_w[1, 0], cache_mla_latent[0:1], cache_mla_krope[0:1], page_table,
                            mla_w_in[0], mla_q_norm_w[0], mla_w_uq[0], mla_kv_norm_w[0], mla_w_uk[0],
                            mla_w_uv[0], mla_w_out[0], dims)
    x = _moe(x, norm_w[1, 1], moe_w_router[0], moe_w_gate, moe_w_up, moe_w_down, 0, **moe_tiles)

    x, swk_p, swv_p, swk_s, swv_s = _swa_layer(x, norm_w[2, 0], cache_swa_k[0], cache_swa_v[0],
                                               swa_w_qkv[0], swa_b_qkv[0], swa_sinks[0], swa_w_out[0],
                                               swa_b_out[0], dims)
    x = _ffn(x, norm_w[2, 1], ffn_w_gate, ffn_w_up, ffn_w_down, 1, tm=tm, tf=256)

    x, dn_p, dn_s, cv_p, cv_s = _dn_layer(x, norm_w[3, 0], state_delta[0], state_delta_conv[0], dn_w_in[0],
                                          dn_conv_w[0], dn_a_log[0], dn_dt_bias[0], dn_norm_w[0],
                                          dn_w_out[0], dims)
    y = _moe(x, norm_w[3, 1], moe_w_router[1], moe_w_gate, moe_w_up, moe_w_down, 1, final_norm_w,
             **moe_tiles)

    lead = lambda a: a[None]
    return (y[:n_tp].reshape(n_p, t_p, d), y[n_tp:].reshape(n_s, t_s, d),
            lead(gla_p), lead(gla_s),
            lead(ckv[:n_tp].reshape(n_p, t_p, -1)), lead(ckv[n_tp:].reshape(n_s, t_s, -1)),
            lead(kr[:n_tp].reshape(n_p, t_p, -1)), lead(kr[n_tp:].reshape(n_s, t_s, -1)),
            lead(swk_p), lead(swk_s), lead(swv_p), lead(swv_s),
            lead(dn_p), lead(dn_s), lead(cv_p), lead(cv_s))
```

```python
import functools
import math

import jax
import jax.numpy as jnp
import numpy as np
from jax import lax
from jax.experimental import pallas as pl
from jax.experimental.pallas import tpu as pltpu

F32 = jnp.float32
BF16 = jnp.bfloat16

NORM_EPS = 1e-6
GLA_HEADS = 4
GLA_GATE_RANK = 16
GLA_GATE_NORMALIZER = 16.0
CHUNK = 64
MLA_HEADS = 16
MLA_Q_LORA = 384
MLA_KV_LORA = 256
MLA_NOPE = 64
MLA_ROPE = 32
MLA_V = 64
ROPE_THETA = 10000.0
SWA_HEADS = 16
SWA_KV_HEADS = 4
SWA_HD = 64
WINDOW = 128
DN_HEADS = 8
DN_HK = 128
DN_HV = 128
DN_CONV = 4
N_EXPERTS = 8

LANES = 128
VMEM_LIMIT = 56 * 1024 * 1024
NEG_INF = float("-inf")


def _cparams(*sem):
    return pltpu.CompilerParams(dimension_semantics=sem, vmem_limit_bytes=VMEM_LIMIT)


def _bdot(a, b):
    return jnp.dot(a.astype(BF16), b.astype(BF16), preferred_element_type=F32)


def _bdot_nt(a, b):
    return lax.dot_general(a.astype(BF16), b.astype(BF16), (((1,), (1,)), ((), ())),
                           preferred_element_type=F32)


def _bdot_tn(a, b):
    return lax.dot_general(a.astype(BF16), b.astype(BF16), (((0,), (0,)), ((), ())),
                           preferred_element_type=F32)


def _split3(x):
    h1 = x.astype(BF16)
    r1 = x - h1.astype(F32)
    h2 = r1.astype(BF16)
    h3 = (r1 - h2.astype(F32)).astype(BF16)
    return h1, h2, h3


def _dot_exact_lhs(m, x):
    mb = m.astype(BF16)
    h1, h2, h3 = _split3(x)
    return (jnp.dot(mb, h1, preferred_element_type=F32) + jnp.dot(mb, h2, preferred_element_type=F32)
            + jnp.dot(mb, h3, preferred_element_type=F32))


def _rms(x, w):
    return x * lax.rsqrt(jnp.mean(x * x, axis=-1, keepdims=True) + NORM_EPS) * w


def _silu(x):
    return x / (1.0 + jnp.exp(-x))


def _log_sigmoid(x):
    return jnp.minimum(x, 0.0) - jnp.log(1.0 + jnp.exp(-jnp.abs(x)))


def _softplus(x):
    return jnp.maximum(x, 0.0) + jnp.log(1.0 + jnp.exp(-jnp.abs(x)))


def _linear_kernel(*refs, has_norm, has_bias, has_res):
    it = iter(refs)
    x_ref = next(it)
    nw_ref = next(it) if has_norm else None
    w_ref = next(it)
    b_ref = next(it) if has_bias else None
    r_ref = next(it) if has_res else None
    o_ref = next(it)
    h_ref = next(it)

    @pl.when(pl.program_id(1) == 0)
    def _():
        xv = x_ref[...].astype(F32)
        if has_norm:
            xv = _rms(xv, nw_ref[...])
        h_ref[...] = xv.astype(BF16)

    acc = jnp.dot(h_ref[...], w_ref[...].astype(BF16), preferred_element_type=F32)
    if has_bias:
        acc = acc + b_ref[...]
    if has_res:
        acc = acc + r_ref[...]
    o_ref[...] = acc.astype(o_ref.dtype)


def _linear(x, w, *, norm_w=None, bias=None, res=None, tm, tn, out_dtype=F32, name="linear"):
    t, k = x.shape
    n = w.shape[1]
    assert t % tm == 0 and n % tn == 0, (t, tm, n, tn)
    in_specs = [pl.BlockSpec((tm, k), lambda i, j: (i, 0))]
    args = [x]
    if norm_w is not None:
        in_specs.append(pl.BlockSpec((1, k), lambda i, j: (0, 0)))
        args.append(norm_w.reshape(1, k))
    in_specs.append(pl.BlockSpec((k, tn), lambda i, j: (0, j)))
    args.append(w)
    if bias is not None:
        in_specs.append(pl.BlockSpec((1, tn), lambda i, j: (0, j)))
        args.append(bias.reshape(1, n))
    if res is not None:
        in_specs.append(pl.BlockSpec((tm, tn), lambda i, j: (i, j)))
        args.append(res)
    kern = functools.partial(_linear_kernel, has_norm=norm_w is not None, has_bias=bias is not None,
                             has_res=res is not None)
    return pl.pallas_call(
        kern,
        grid=(t // tm, n // tn),
        in_specs=in_specs,
        out_specs=pl.BlockSpec((tm, tn), lambda i, j: (i, j)),
        out_shape=jax.ShapeDtypeStruct((t, n), out_dtype),
        scratch_shapes=[pltpu.VMEM((tm, k), BF16)],
        compiler_params=_cparams("parallel", "arbitrary"),
        name=name,
    )(*args)


def _swiglu_acc(h_ref, wg_ref, wu_ref, wd_ref, acc_ref):
    h = h_ref[...].astype(BF16)
    g = jnp.dot(h, wg_ref[...].astype(BF16), preferred_element_type=F32)
    u = jnp.dot(h, wu_ref[...].astype(BF16), preferred_element_type=F32)
    a = (_silu(g) * u).astype(BF16)
    acc_ref[...] += jnp.dot(a, wd_ref[...].astype(BF16), preferred_element_type=F32)


def _ffn_kernel(x_ref, nw_ref, wg_ref, wu_ref, wd_ref, o_ref, h_ref, acc_ref):
    f = pl.program_id(1)

    @pl.when(f == 0)
    def _():
        h_ref[...] = _rms(x_ref[...], nw_ref[...]).astype(BF16)
        acc_ref[...] = jnp.zeros_like(acc_ref)

    _swiglu_acc(h_ref, wg_ref, wu_ref, wd_ref, acc_ref)

    @pl.when(f == pl.num_programs(1) - 1)
    def _():
        o_ref[...] = x_ref[...] + acc_ref[...]


def _ffn(x, norm_w, wg, wu, wd, layer, *, tm, tf):
    t, d = x.shape
    ff = wg.shape[2]
    assert t % tm == 0 and ff % tf == 0
    return pl.pallas_call(
        _ffn_kernel,
        grid=(t // tm, ff // tf),
        in_specs=[pl.BlockSpec((tm, d), lambda i, f: (i, 0)),
                  pl.BlockSpec((1, d), lambda i, f: (0, 0)),
                  pl.BlockSpec((None, d, tf), lambda i, f: (layer, 0, f)),
                  pl.BlockSpec((None, d, tf), lambda i, f: (layer, 0, f)),
                  pl.BlockSpec((None, tf, d), lambda i, f: (layer, f, 0))],
        out_specs=pl.BlockSpec((tm, d), lambda i, f: (i, 0)),
        out_shape=jax.ShapeDtypeStruct((t, d), F32),
        scratch_shapes=[pltpu.VMEM((tm, d), BF16), pltpu.VMEM((tm, d), F32)],
        compiler_params=_cparams("parallel", "arbitrary"),
        name="ffn",
    )(x, norm_w.reshape(1, d), wg, wu, wd)


MOE_CHUNK = 256
MOE_SLOTS = 4


def _route_kernel(x_ref, nw_ref, wrt_ref, h_ref, sel_ref, gate_ref, rank_ref, cnt_ref):
    hn = _rms(x_ref[...], nw_ref[...])
    h_ref[...] = hn.astype(BF16)
    logits = lax.dot_general(wrt_ref[...], hn, (((1,), (1,)), ((), ())), preferred_element_type=F32,
                             precision=lax.Precision.HIGHEST)
    n_exp, tm = logits.shape
    sub = lax.broadcasted_iota(jnp.int32, logits.shape, 0)
    m1 = jnp.max(logits, axis=0, keepdims=True)
    i1 = jnp.min(jnp.where(logits == m1, sub, n_exp), axis=0, keepdims=True)
    rest = jnp.where(sub == i1, NEG_INF, logits)
    m2 = jnp.max(rest, axis=0, keepdims=True)
    i2 = jnp.min(jnp.where(rest == m2, sub, n_exp), axis=0, keepdims=True)
    e2 = jnp.exp(m2 - m1)
    first, second = sub == i1, sub == i2
    sel = jnp.where(first | second, 1.0, 0.0)
    upper = jnp.where(lax.broadcasted_iota(jnp.int32, (tm, tm), 0)
                      <= lax.broadcasted_iota(jnp.int32, (tm, tm), 1), 1.0, 0.0)
    cum = _bdot(sel, upper)
    sel_ref[...] = sel
    gate_ref[...] = jnp.where(first, 1.0 / (1.0 + e2), 0.0) + jnp.where(second, e2 / (1.0 + e2), 0.0)
    rank_ref[...] = cum - sel
    cnt_ref[0] = jnp.broadcast_to(cum[:, tm - 1:tm], (n_exp, LANES))


def _route(x, norm_w, w_router, *, tm):
    t, d = x.shape
    n_exp = w_router.shape[1]
    et = lambda i: (0, i)
    return pl.pallas_call(
        _route_kernel,
        grid=(t // tm,),
        in_specs=[pl.BlockSpec((tm, d), lambda i: (i, 0)), pl.BlockSpec((1, d), lambda i: (0, 0)),
                  pl.BlockSpec((n_exp, d), lambda i: (0, 0))],
        out_specs=[pl.BlockSpec((tm, d), lambda i: (i, 0)), pl.BlockSpec((n_exp, tm), et),
                   pl.BlockSpec((n_exp, tm), et), pl.BlockSpec((n_exp, tm), et),
                   pl.BlockSpec((1, n_exp, LANES), lambda i: (i, 0, 0))],
        out_shape=[jax.ShapeDtypeStruct((t, d), BF16), jax.ShapeDtypeStruct((n_exp, t), F32),
                   jax.ShapeDtypeStruct((n_exp, t), F32), jax.ShapeDtypeStruct((n_exp, t), F32),
                   jax.ShapeDtypeStruct((t // tm, n_exp, LANES), F32)],
        compiler_params=_cparams("parallel"),
        name="moe_route",
    )(x, norm_w.reshape(1, d), w_router.T)


def _moe_plan(cnt, *, tm, tp, p_rows):
    n_tiles, n_exp = cnt.shape
    seg = (cnt + 7) // 8 * 8
    total = jnp.sum(seg, axis=0)
    in_group = jnp.cumsum(seg, axis=0) - seg
    nch = (cnt + MOE_CHUNK - 1) // MOE_CHUNK
    reach = jnp.max(in_group + nch * MOE_CHUNK, axis=0)
    gsize = (jnp.maximum(total, reach) + tp - 1) // tp * tp
    gstart = jnp.cumsum(gsize) - gsize
    seg_start = gstart[None, :] + in_group
    cum_e = jnp.cumsum(nch, axis=1)
    q_max = n_exp + 2 * tm // MOE_CHUNK
    q = jnp.arange(q_max, dtype=jnp.int32)[None, :]
    flat_e = jnp.minimum(jnp.sum(q[:, :, None] >= cum_e[:, None, :], axis=-1), n_exp - 1).astype(jnp.int32)
    flat_c = q - jnp.take_along_axis(cum_e - nch, flat_e, axis=1)
    flat_row = jnp.take_along_axis(seg_start, flat_e, axis=1) + flat_c * MOE_CHUNK
    row_j = jnp.arange(p_rows // tp, dtype=jnp.int32) * tp
    tile_e = jnp.minimum(jnp.sum(row_j[:, None] >= (gstart + gsize)[None, :], axis=-1), n_exp - 1)
    tile_valid = row_j < jnp.take(gstart + total, tile_e)
    i32 = lambda a: a.astype(jnp.int32)
    return (i32(cum_e[:, -1]), i32(flat_e.reshape(-1)), i32(flat_c.reshape(-1)), i32(flat_row.reshape(-1)),
            i32(tile_e), i32(tile_valid))


def _dispatch_kernel(nq_ref, fe_ref, fc_ref, fr_ref, h_ref, sel_ref, rank_ref, xs_in, xs_ref, stage, sem,
                     *, q_max):
    del xs_in
    i = pl.program_id(0)
    n = nq_ref[i]
    h = h_ref[...]

    def chunk_copy(slot, row):
        return pltpu.make_async_copy(stage.at[slot], xs_ref.at[pl.ds(pl.multiple_of(row, 8), MOE_CHUNK)],
                                     sem.at[slot])

    def body(q, carry):
        slot = q % MOE_SLOTS
        e = fe_ref[i * q_max + q]
        c = fc_ref[i * q_max + q]

        @pl.when(q >= MOE_SLOTS)
        def _():
            chunk_copy(slot, 0).wait()

        pos = jnp.where(sel_ref[pl.ds(e, 1), :] > 0.0, rank_ref[pl.ds(e, 1), :], -1.0)
        want = (c * MOE_CHUNK + lax.broadcasted_iota(jnp.int32, (MOE_CHUNK, 1), 0)).astype(F32)
        pick = jnp.where(pos == want, 1.0, 0.0).astype(BF16)
        stage[slot] = jnp.dot(pick, h, preferred_element_type=F32)
        chunk_copy(slot, fr_ref[i * q_max + q]).start()
        return carry

    lax.fori_loop(0, n, body, 0)
    for s in range(MOE_SLOTS):
        @pl.when(n > s)
        def _():
            chunk_copy(s, 0).wait()


def _dispatch(h, sel, rank, plan, *, tm, p_rows):
    t, d = h.shape
    n_exp = sel.shape[0]
    n_flat, flat_e, flat_c, flat_row = plan[:4]
    q_max = flat_e.shape[0] // (t // tm)
    et = lambda i, *_: (0, i)
    return pl.pallas_call(
        functools.partial(_dispatch_kernel, q_max=q_max),
        grid_spec=pltpu.PrefetchScalarGridSpec(
            num_scalar_prefetch=4,
            grid=(t // tm,),
            in_specs=[pl.BlockSpec((tm, d), lambda i, *_: (i, 0)), pl.BlockSpec((n_exp, tm), et),
                      pl.BlockSpec((n_exp, tm), et), pl.BlockSpec(memory_space=pl.ANY)],
            out_specs=pl.BlockSpec(memory_space=pl.ANY),
            scratch_shapes=[pltpu.VMEM((MOE_SLOTS, MOE_CHUNK, d), F32),
                            pltpu.SemaphoreType.DMA((MOE_SLOTS,))]),
        out_shape=jax.ShapeDtypeStruct((p_rows, d), F32),
        input_output_aliases={7: 0},
        compiler_params=_cparams("arbitrary"),
        name="moe_dispatch",
    )(n_flat, flat_e, flat_c, flat_row, h, sel, rank, jnp.zeros((p_rows, d), F32))


def _gffn_kernel(te_ref, tv_ref, x_ref, wg_ref, wu_ref, wd_ref, o_ref, acc_ref):
    del te_ref
    j = pl.program_id(0)
    f = pl.program_id(1)
    valid = tv_ref[j] > 0

    @pl.when(f == 0)
    def _():
        acc_ref[...] = jnp.zeros_like(acc_ref)

    @pl.when(valid)
    def _():
        _swiglu_acc(x_ref, wg_ref, wu_ref, wd_ref, acc_ref)

    @pl.when(f == pl.num_programs(1) - 1)
    def _():
        o_ref[...] = acc_ref[...]


def _gffn(xs, wg, wu, wd, layer, tile_e, tile_valid, *, tp, tf):
    p_rows, d = xs.shape
    ff = wg.shape[3]

    def wmap(is_down):
        def index(j, f, te, tv):
            fi = jnp.where(tv[j] > 0, f, 0)
            return (layer, te[j], fi, 0) if is_down else (layer, te[j], 0, fi)
        return index

    return pl.pallas_call(
        _gffn_kernel,
        grid_spec=pltpu.PrefetchScalarGridSpec(
            num_scalar_prefetch=2,
            grid=(p_rows // tp, ff // tf),
            in_specs=[pl.BlockSpec((tp, d), lambda j, f, te, tv: (j, 0)),
                      pl.BlockSpec((None, None, d, tf), wmap(False)),
                      pl.BlockSpec((None, None, d, tf), wmap(False)),
                      pl.BlockSpec((None, None, tf, d), wmap(True))],
            out_specs=pl.BlockSpec((tp, d), lambda j, f, te, tv: (j, 0)),
            scratch_shapes=[pltpu.VMEM((tp, d), F32)]),
        out_shape=jax.ShapeDtypeStruct((p_rows, d), F32),
        compiler_params=_cparams("parallel", "arbitrary"),
        name="moe_ffn",
    )(tile_e, tile_valid, xs, wg, wu, wd)


def _combine_kernel(*refs, q_max, final_norm):
    nq_ref, fe_ref, fc_ref, fr_ref = refs[:4]
    x_ref, pos_ref, gate_ref = refs[4:7]
    fw_ref = refs[7] if final_norm else None
    ys_ref, o_ref, acc_ref, buf, sem = refs[7 + int(final_norm):]
    i = pl.program_id(0)
    n = nq_ref[i]
    acc_ref[...] = jnp.zeros_like(acc_ref)
    lane = lax.broadcasted_iota(jnp.int32, pos_ref.shape, 1)

    def chunk_copy(slot, row):
        return pltpu.make_async_copy(ys_ref.at[pl.ds(pl.multiple_of(row, 8), MOE_CHUNK)], buf.at[slot],
                                     sem.at[slot])

    @pl.when(n > 0)
    def _():
        chunk_copy(0, fr_ref[i * q_max]).start()

    def body(q, carry):
        slot = q % 2
        e = fe_ref[i * q_max + q]
        c = fc_ref[i * q_max + q]

        @pl.when(q + 1 < n)
        def _():
            chunk_copy(1 - slot, fr_ref[i * q_max + q + 1]).start()

        chunk_copy(slot, 0).wait()
        pos = jnp.sum(jnp.where(lane == e, pos_ref[...], 0.0), axis=1, keepdims=True)
        gate = jnp.sum(jnp.where(lane == e, gate_ref[...], 0.0), axis=1, keepdims=True)
        want = (c * MOE_CHUNK + lax.broadcasted_iota(jnp.int32, (1, MOE_CHUNK), 1)).astype(F32)
        pick = jnp.where(pos == want, 1.0, 0.0).astype(BF16)
        hi, lo = _split2(buf[slot])
        rows = jnp.dot(pick, hi, preferred_element_type=F32) + jnp.dot(pick, lo, preferred_element_type=F32)
        acc_ref[...] += gate * rows
        return carry

    lax.fori_loop(0, n, body, 0)
    y = x_ref[...] + acc_ref[...]
    if final_norm:
        y = _rms(y, fw_ref[...])
    o_ref[...] = y


def _combine(x, ys, pos_tok, gate_tok, plan, final_w, *, tm):
    t, d = x.shape
    n_exp = pos_tok.shape[1]
    n_flat, flat_e, flat_c, flat_row = plan[:4]
    q_max = flat_e.shape[0] // (t // tm)
    row = lambda i, *_: (i, 0)
    in_specs = [pl.BlockSpec((tm, d), row), pl.BlockSpec((tm, n_exp), row), pl.BlockSpec((tm, n_exp), row)]
    args = [x, pos_tok, gate_tok]
    if final_w is not None:
        in_specs.append(pl.BlockSpec((1, d), lambda i, *_: (0, 0)))
        args.append(final_w.reshape(1, d))
    in_specs.append(pl.BlockSpec(memory_space=pl.ANY))
    args.append(ys)
    return pl.pallas_call(
        functools.partial(_combine_kernel, q_max=q_max, final_norm=final_w is not None),
        grid_spec=pltpu.PrefetchScalarGridSpec(
            num_scalar_prefetch=4,
            grid=(t // tm,),
            in_specs=in_specs,
            out_specs=pl.BlockSpec((tm, d), row),
            scratch_shapes=[pltpu.VMEM((tm, d), F32), pltpu.VMEM((2, MOE_CHUNK, d), F32),
                            pltpu.SemaphoreType.DMA((2,))]),
        out_shape=jax.ShapeDtypeStruct((t, d), F32),
        compiler_params=_cparams("arbitrary"),
        name="moe_combine",
    )(n_flat, flat_e, flat_c, flat_row, *args)


def _moe(x, norm_w, w_router, wg, wu, wd, layer, final_w=None, *, tm, tp, tf):
    t, d = x.shape
    n_exp = w_router.shape[1]
    n_tiles = t // tm
    worst = 2 * t + n_tiles * n_exp * 7 + n_exp * (MOE_CHUNK + tp - 1)
    p_rows = (worst + tp - 1) // tp * tp
    h, sel, gate, rank, cnt = _route(x, norm_w, w_router, tm=tm)
    plan = _moe_plan(cnt[:, :, 0].astype(jnp.int32), tm=tm, tp=tp, p_rows=p_rows)
    xs = _dispatch(h, sel, rank, plan, tm=tm, p_rows=p_rows)
    ys = _gffn(xs, wg, wu, wd, layer, plan[4], plan[5], tp=tp, tf=tf)
    pos_tok = jnp.where(sel > 0.0, rank, -1.0).T
    return _combine(x, ys, pos_tok, gate.T, plan, final_w, tm=tm)


def _pad_rows(x, rows):
    if x.shape[0] == rows:
        return x
    return jnp.concatenate([x, jnp.zeros((rows - x.shape[0], x.shape[1]), x.dtype)], axis=0)


def _chunk_masks(rows):
    ri = lax.broadcasted_iota(jnp.int32, (rows, rows), 0)
    ci = lax.broadcasted_iota(jnp.int32, (rows, rows), 1)
    same = (ri // CHUNK) == (ci // CHUNK)
    return same & (ci <= ri), same


def _gla_kernel(*refs, rb, zero_init, has_alias):
    it = iter(refs)
    q_ref, k_ref, v_ref, g_ref, gk_ref = next(it), next(it), next(it), next(it), next(it)
    s0_ref = None if zero_init else next(it)
    wgk_ref, bgk_ref, nw_ref = next(it), next(it), next(it)
    if has_alias:
        next(it)
    o_ref, s_ref, st_ref = next(it), next(it), next(it)
    r = pl.program_id(1)
    rows = max(rb, CHUNK)
    n_chunks = rows // CHUNK
    hk = q_ref.shape[1] // GLA_HEADS
    hv = v_ref.shape[1] // GLA_HEADS

    @pl.when(r == 0)
    def _():
        for h in range(GLA_HEADS):
            if zero_init:
                st_ref[h] = jnp.zeros(st_ref.shape[1:], F32)
            else:
                st_ref[h] = s0_ref[0, h].T

    q = _pad_rows(q_ref[...] * hk ** -0.5, rows)
    k = _pad_rows(k_ref[...], rows)
    v = _pad_rows(v_ref[...], rows)
    la = _log_sigmoid(_bdot(gk_ref[...], wgk_ref[...]) + bgk_ref[...]) * (1.0 / GLA_GATE_NORMALIZER)
    la = _pad_rows(la, rows)
    causal, same = _chunk_masks(rows)
    gc = _dot_exact_lhs(causal, la)
    gt = _dot_exact_lhs(same, la)
    q_in = q * jnp.exp(gc)
    k_in = k * jnp.exp(-gc)
    k_out = k * jnp.exp(gt - gc)
    e_tot = jnp.exp(gt)
    tri = causal[:CHUNK, :CHUNK]
    for h in range(GLA_HEADS):
        st = st_ref[h]
        ks = slice(h * hk, (h + 1) * hk)
        vs = slice(h * hv, (h + 1) * hv)
        for c in range(n_chunks):
            rs = slice(c * CHUNK, (c + 1) * CHUNK)
            qi, ki, ko, vh = q_in[rs, ks], k_in[rs, ks], k_out[rs, ks], v[rs, vs]
            intra = jnp.where(tri, _bdot_nt(qi, ki), 0.0)
            o = _bdot(intra, vh) + _bdot_nt(qi, st)
            st = st * e_tot[c * CHUNK:c * CHUNK + 1, ks] + _bdot_tn(vh, ko)
            n_out = min(rb, CHUNK)
            og = _rms(o[:n_out], nw_ref[...]) * _silu(g_ref[c * CHUNK:c * CHUNK + n_out, vs])
            o_ref[c * CHUNK:c * CHUNK + n_out, vs] = og
        st_ref[h] = st

    @pl.when(r == pl.num_programs(1) - 1)
    def _():
        for h in range(GLA_HEADS):
            s_ref[0, h] = st_ref[h].T


def _gla_core(proj, s0, w_gk2, b_gk2, norm_w, o_full, *, n_seq, seq_len, row0, rb):
    t = proj.shape[0]
    dk = w_gk2.shape[1]
    dv = 2 * dk
    hk, hv = dk // GLA_HEADS, dv // GLA_HEADS
    nblk = seq_len // rb
    base = row0 // rb
    assert row0 % rb == 0 and seq_len % rb == 0

    def rowmap(col):
        return lambda b, r: (base + b * nblk + r, col)

    in_specs = [pl.BlockSpec((rb, dk), rowmap(0)), pl.BlockSpec((rb, dk), rowmap(1)),
                pl.BlockSpec((rb, dv), rowmap(1)), pl.BlockSpec((rb, dv), rowmap(2)),
                pl.BlockSpec((rb, LANES), rowmap((2 * dk + 2 * dv) // LANES))]
    args = [proj, proj, proj, proj, proj]
    if s0 is not None:
        in_specs.append(pl.BlockSpec((1, GLA_HEADS, hk, hv), lambda b, r: (b, 0, 0, 0)))
        args.append(s0)
    wgk = jnp.pad(w_gk2, ((0, LANES - w_gk2.shape[0]), (0, 0)))
    in_specs += [pl.BlockSpec((LANES, dk), lambda b, r: (0, 0)),
                 pl.BlockSpec((1, dk), lambda b, r: (0, 0)),
                 pl.BlockSpec((1, hv), lambda b, r: (0, 0))]
    args += [wgk, b_gk2.reshape(1, dk), norm_w.reshape(1, hv)]
    aliases = {}
    if o_full is not None:
        in_specs.append(pl.BlockSpec(memory_space=pl.ANY))
        aliases = {len(args): 0}
        args.append(o_full)
    kern = functools.partial(_gla_kernel, rb=rb, zero_init=s0 is None, has_alias=o_full is not None)
    return pl.pallas_call(
        kern,
        grid=(n_seq, nblk),
        in_specs=in_specs,
        out_specs=[pl.BlockSpec((rb, dv), rowmap(0)),
                   pl.BlockSpec((1, GLA_HEADS, hk, hv), lambda b, r: (b, 0, 0, 0))],
        out_shape=[jax.ShapeDtypeStruct((t, dv), F32),
                   jax.ShapeDtypeStruct((n_seq, GLA_HEADS, hk, hv), F32)],
        scratch_shapes=[pltpu.VMEM((GLA_HEADS, hv, hk), F32)],
        input_output_aliases=aliases,
        compiler_params=_cparams("parallel", "arbitrary"),
        name="gla_core",
    )(*args)


def _tile(n, pref):
    if n <= pref:
        return n
    for c in range(pref, 7, -8):
        if n % c == 0:
            return c
    return n


def _pad_cols(w, n):
    return jnp.pad(w, ((0, 0), (0, n - w.shape[1])))


def _gla_layer(x, nw, s0_s, w_in, w_gk2, b_gk2, norm_w, w_out, dims):
    n_p, t_p, n_s, t_s = dims
    t = x.shape[0]
    tm = _tile(t, 1280)
    dk = w_gk2.shape[1]
    width = 6 * dk + LANES
    proj = _linear(x, _pad_cols(w_in, width), norm_w=nw, tm=tm, tn=_tile(width, 640), name="gla_in")
    o, sp = _gla_core(proj, None, w_gk2, b_gk2, norm_w, None, n_seq=n_p, seq_len=t_p, row0=0,
                      rb=min(t_p, 256))
    o, ss = _gla_core(proj, s0_s, w_gk2, b_gk2, norm_w, o, n_seq=n_s, seq_len=t_s, row0=n_p * t_p,
                      rb=t_s)
    x = _linear(o, w_out, res=x, tm=tm, tn=512, name="gla_out")
    return x, sp, ss


def _rope_tables(pos, half):
    freqs = np.exp(-math.log(ROPE_THETA) * np.arange(half, dtype=np.float64) / half)
    ang = np.asarray(pos, np.float64)[:, None] * freqs[None, :]
    cos, sin = np.cos(ang), np.sin(ang)
    return (jnp.asarray(np.concatenate([cos, cos], axis=-1), F32),
            jnp.asarray(np.concatenate([-sin, sin], axis=-1), F32))


def _swap_halves(w, axis=-1):
    a, b = jnp.split(w, 2, axis=axis)
    return jnp.concatenate([b, a], axis=axis)


def _mla_in_kernel(x_ref, nw_ref, w_ref, kvw_ref, cos_ref, sin_ref, cq_ref, ckv_ref, kr_ref):
    h = _rms(x_ref[...], nw_ref[...])
    y = _bdot(h, w_ref[...])
    cq_ref[...] = y[:, :MLA_Q_LORA]
    ckv_ref[...] = _rms(y[:, MLA_Q_LORA:MLA_Q_LORA + MLA_KV_LORA], kvw_ref[...])
    o = MLA_Q_LORA + MLA_KV_LORA
    kr_ref[...] = (y[:, o:o + MLA_ROPE] * cos_ref[...]
                   + y[:, o + LANES:o + LANES + MLA_ROPE] * sin_ref[...])


def _mla_in(x, nw, w_in, kv_norm_w, cos, sin, *, tm):
    t, d = x.shape
    o = MLA_Q_LORA + MLA_KV_LORA
    kr_w = w_in[:, o:o + MLA_ROPE]
    w_aug = jnp.concatenate([w_in[:, :o], _pad_cols(kr_w, LANES), _pad_cols(_swap_halves(kr_w), LANES)],
                            axis=1)
    wid = w_aug.shape[1]
    row = lambda i: (i, 0)
    fix = lambda i: (0, 0)
    return pl.pallas_call(
        _mla_in_kernel,
        grid=(t // tm,),
        in_specs=[pl.BlockSpec((tm, d), row), pl.BlockSpec((1, d), fix), pl.BlockSpec((d, wid), fix),
                  pl.BlockSpec((1, MLA_KV_LORA), fix), pl.BlockSpec((tm, MLA_ROPE), row),
                  pl.BlockSpec((tm, MLA_ROPE), row)],
        out_specs=[pl.BlockSpec((tm, MLA_Q_LORA), row), pl.BlockSpec((tm, MLA_KV_LORA), row),
                   pl.BlockSpec((tm, MLA_ROPE), row)],
        out_shape=[jax.ShapeDtypeStruct((t, MLA_Q_LORA), F32), jax.ShapeDtypeStruct((t, MLA_KV_LORA), F32),
                   jax.ShapeDtypeStruct((t, MLA_ROPE), F32)],
        compiler_params=_cparams("parallel"),
        name="mla_in",
    )(x, nw.reshape(1, d), w_aug, kv_norm_w.reshape(1, -1), cos, sin)


def _mla_q_kernel(cq_ref, qw_ref, wn_ref, wr_ref, ws_ref, wk_ref, cos_ref, sin_ref, ql_ref, qr_ref):
    cq = _rms(cq_ref[...], qw_ref[...]).astype(BF16)
    qn = jnp.dot(cq, wn_ref[...].astype(BF16), preferred_element_type=F32).astype(BF16)
    for j in range(MLA_HEADS // 2):
        ql = jnp.dot(qn[:, j * LANES:(j + 1) * LANES], wk_ref[j].astype(BF16), preferred_element_type=F32)
        ql_ref[2 * j] = ql[:, :MLA_KV_LORA]
        ql_ref[2 * j + 1] = ql[:, MLA_KV_LORA:]
    qr = jnp.dot(cq, wr_ref[...].astype(BF16), preferred_element_type=F32)
    qs = jnp.dot(cq, ws_ref[...].astype(BF16), preferred_element_type=F32)
    per = LANES // MLA_ROPE
    cos = jnp.concatenate([cos_ref[...]] * (MLA_HEADS // per), axis=-1)
    sin = jnp.concatenate([sin_ref[...]] * (MLA_HEADS // per), axis=-1)
    rot = qr * cos + qs * sin
    for h in range(MLA_HEADS):
        qr_ref[h] = rot[:, h * MLA_ROPE:(h + 1) * MLA_ROPE]


def _mla_q(cq, q_norm_w, w_uq, w_uk, cos, sin, *, tm):
    t = cq.shape[0]
    per = LANES // MLA_ROPE
    w3 = w_uq.reshape(MLA_Q_LORA, MLA_HEADS, MLA_NOPE + MLA_ROPE)
    w_nope = w3[:, :, :MLA_NOPE].reshape(MLA_Q_LORA, MLA_HEADS * MLA_NOPE)
    w_rope = w3[:, :, MLA_NOPE:].reshape(MLA_Q_LORA, MLA_HEADS * MLA_ROPE)
    w_swap = _swap_halves(w3[:, :, MLA_NOPE:]).reshape(MLA_Q_LORA, MLA_HEADS * MLA_ROPE)
    a = jnp.transpose(w_uk, (1, 2, 0))
    z = jnp.zeros_like(a[0::2])
    w_bd = jnp.concatenate([jnp.concatenate([a[0::2], z], axis=2),
                            jnp.concatenate([z, a[1::2]], axis=2)], axis=1)
    cos4 = jnp.tile(cos, (1, per))
    sin4 = jnp.tile(sin, (1, per))
    row = lambda i: (i, 0)
    fix2 = lambda i: (0, 0)
    return pl.pallas_call(
        _mla_q_kernel,
        grid=(t // tm,),
        in_specs=[pl.BlockSpec((tm, MLA_Q_LORA), row), pl.BlockSpec((1, MLA_Q_LORA), fix2),
                  pl.BlockSpec(w_nope.shape, fix2), pl.BlockSpec(w_rope.shape, fix2),
                  pl.BlockSpec(w_swap.shape, fix2), pl.BlockSpec(w_bd.shape, lambda i: (0, 0, 0)),
                  pl.BlockSpec((tm, LANES), row), pl.BlockSpec((tm, LANES), row)],
        out_specs=[pl.BlockSpec((MLA_HEADS, tm, MLA_KV_LORA), lambda i: (0, i, 0)),
                   pl.BlockSpec((MLA_HEADS, tm, MLA_ROPE), lambda i: (0, i, 0))],
        out_shape=[jax.ShapeDtypeStruct((MLA_HEADS, t, MLA_KV_LORA), F32),
                   jax.ShapeDtypeStruct((MLA_HEADS, t, MLA_ROPE), F32)],
        compiler_params=_cparams("parallel"),
        name="mla_q",
    )(cq, q_norm_w.reshape(1, -1), w_nope, w_rope, w_swap, w_bd, cos4, sin4)


MLA_QSCALE = (MLA_NOPE + MLA_ROPE) ** -0.5 * math.log2(math.e)


def _lane_repeat(x, width):
    return jnp.concatenate([x] * (width // LANES), axis=1)


def _flash_chunk(s, cb, m_ref, l_ref, acc_ref, rs):
    m_prev = m_ref[rs]
    m_new = jnp.maximum(m_prev, jnp.max(s, axis=-1, keepdims=True))
    alpha = jnp.exp2(m_prev - m_new)
    p = jnp.exp2(s - _lane_repeat(m_new, s.shape[1]))
    l_ref[rs] = alpha * l_ref[rs] + jnp.sum(p, axis=-1, keepdims=True)
    acc_ref[rs] = (_lane_repeat(alpha, acc_ref.shape[1]) * acc_ref[rs]
                   + jnp.dot(p.astype(BF16), cb, preferred_element_type=F32))
    m_ref[rs] = m_new


def _flash_init(ql_ref, qr_ref, qlb_ref, qrb_ref, m_ref, l_ref, acc_ref):
    rows = qlb_ref.shape[0]
    qlb_ref[...] = (ql_ref[...].reshape(rows, MLA_KV_LORA) * MLA_QSCALE).astype(BF16)
    qrb_ref[...] = (qr_ref[...].reshape(rows, MLA_ROPE) * MLA_QSCALE).astype(BF16)
    m_ref[...] = jnp.full(m_ref.shape, NEG_INF, F32)
    l_ref[...] = jnp.zeros(l_ref.shape, F32)
    acc_ref[...] = jnp.zeros(acc_ref.shape, F32)


def _mla_finish(acc_ref, l_ref, wv_ref, o_ref, rows_per_head):
    inv = _lane_repeat(1.0 / l_ref[...], acc_ref.shape[1])
    outs = []
    for j in range(MLA_HEADS // 2):
        pair = None
        for h in (2 * j, 2 * j + 1):
            rs = slice(h * rows_per_head, (h + 1) * rows_per_head)
            part = _bdot(acc_ref[rs] * inv[rs], wv_ref[h])
            pair = part if pair is None else pair + part
        outs.append(pair)
    o_ref[...] = jnp.concatenate(outs, axis=-1)


def _mla_prompt_kernel(wq_ref, wk_ref, ql_ref, qr_ref, c_ref, r_ref, wv_ref, o_ref, qlb_ref, qrb_ref, m_ref,
                       l_ref, acc_ref, *, tq, tk, rc):
    qi = wq_ref[pl.program_id(1)]
    kj = wk_ref[pl.program_id(1)]
    last = (qi * tq + tq - 1) // tk
    rows = MLA_HEADS * tq

    @pl.when(kj == 0)
    def _():
        _flash_init(ql_ref, qr_ref, qlb_ref, qrb_ref, m_ref, l_ref, acc_ref)

    def step(masked):
        cb = c_ref[...].astype(BF16)
        rb = r_ref[...].astype(BF16)
        chunks = [slice(ch * rc, (ch + 1) * rc) for ch in range(rows // rc)]
        scores = [_bdot_nt(qlb_ref[rs], cb) + _bdot_nt(qrb_ref[rs], rb) for rs in chunks]
        if masked:
            q_pos = qi * tq + lax.broadcasted_iota(jnp.int32, (rc, tk), 0) % tq
            k_pos = kj * tk + lax.broadcasted_iota(jnp.int32, (rc, tk), 1)
            scores = [jnp.where(k_pos <= q_pos, s, NEG_INF) for s in scores]
        probs, alphas = [], []
        for rs, s in zip(chunks, scores):
            m_prev = m_ref[rs]
            m_new = jnp.maximum(m_prev, jnp.max(s, axis=-1, keepdims=True))
            alpha = jnp.exp2(m_prev - m_new)
            p = jnp.exp2(s - _lane_repeat(m_new, tk))
            l_ref[rs] = alpha * l_ref[rs] + jnp.sum(p, axis=-1, keepdims=True)
            m_ref[rs] = m_new
            probs.append(p.astype(BF16))
            alphas.append(alpha)
        for rs, p, alpha in zip(chunks, probs, alphas):
            acc_ref[rs] = (_lane_repeat(alpha, MLA_KV_LORA) * acc_ref[rs]
                           + jnp.dot(p, cb, preferred_element_type=F32))

    @pl.when(kj < last)
    def _():
        step(False)

    @pl.when(kj == last)
    def _():
        step(True)
        _mla_finish(acc_ref, l_ref, wv_ref, o_ref, tq)


def _pad_uv(w_uv):
    a = jnp.transpose(w_uv, (1, 0, 2))
    z = jnp.zeros_like(a)
    even = (jnp.arange(a.shape[0]) % 2 == 0)[:, None, None]
    return jnp.concatenate([jnp.where(even, a, z), jnp.where(even, z, a)], axis=2)


def _mla_prompt_attn(q_lat, q_rope, c_kv, k_r, w_uv_pad, *, n_seq, seq_len, tq, tk):
    t = c_kv.shape[0]
    nq, nk = seq_len // tq, seq_len // tk
    assert tk % tq == 0
    pairs = [(i, j) for i in range(nq) for j in range((i * tq + tq - 1) // tk + 1)]
    work_q = jnp.asarray([p[0] for p in pairs], jnp.int32)
    work_k = jnp.asarray([p[1] for p in pairs], jnp.int32)

    def qmap(b, w, wq, wk):
        return (0, b * nq + wq[w], 0)

    def kmap(b, w, wq, wk):
        return (b * nk + wk[w], 0)

    rows = MLA_HEADS * tq
    return pl.pallas_call(
        functools.partial(_mla_prompt_kernel, tq=tq, tk=tk, rc=2 * tq),
        grid_spec=pltpu.PrefetchScalarGridSpec(
            num_scalar_prefetch=2,
            grid=(n_seq, len(pairs)),
            in_specs=[pl.BlockSpec((MLA_HEADS, tq, MLA_KV_LORA), qmap),
                      pl.BlockSpec((MLA_HEADS, tq, MLA_ROPE), qmap),
                      pl.BlockSpec((tk, MLA_KV_LORA), kmap), pl.BlockSpec((tk, MLA_ROPE), kmap),
                      pl.BlockSpec(w_uv_pad.shape, lambda b, w, wq, wk: (0, 0, 0))],
            out_specs=pl.BlockSpec((tq, MLA_HEADS * MLA_V), lambda b, w, wq, wk: (b * nq + wq[w], 0)),
            scratch_shapes=_flash_scratch(rows)),
        out_shape=jax.ShapeDtypeStruct((t, MLA_HEADS * MLA_V), F32),
        compiler_params=_cparams("parallel", "arbitrary"),
        name="mla_prompt_attn",
    )(work_q, work_k, q_lat, q_rope, c_kv, k_r, w_uv_pad)


def _flash_scratch(rows):
    return [pltpu.VMEM((rows, MLA_KV_LORA), BF16), pltpu.VMEM((rows, MLA_ROPE), BF16),
            pltpu.VMEM((rows, LANES), F32), pltpu.VMEM((rows, LANES), F32),
            pltpu.VMEM((rows, MLA_KV_LORA), F32)]


def _mla_sample_kernel(*refs, n_pg, t_s):
    pt_ref = refs[0]
    ql_ref, qr_ref = refs[1], refs[2]
    lat_refs = refs[3:3 + n_pg]
    kr_refs = refs[3 + n_pg:3 + 2 * n_pg]
    cn_ref, rn_ref, wv_ref = refs[3 + 2 * n_pg:6 + 2 * n_pg]
    o_ref, qlb_ref, qrb_ref, m_ref, l_ref, acc_ref = refs[7 + 2 * n_pg:]
    del pt_ref
    kj = pl.program_id(1)
    n_steps = pl.num_programs(1)
    rows = MLA_HEADS * t_s
    everything = slice(0, rows)

    @pl.when(kj == 0)
    def _():
        _flash_init(ql_ref, qr_ref, qlb_ref, qrb_ref, m_ref, l_ref, acc_ref)

    @pl.when(kj < n_steps - 1)
    def _():
        cb = jnp.concatenate([ref[...].astype(BF16) for ref in lat_refs], axis=0)
        rbt = jnp.concatenate([ref[...].astype(BF16) for ref in kr_refs], axis=1)
        s = _bdot_nt(qlb_ref[...], cb) + jnp.dot(qrb_ref[...], rbt, preferred_element_type=F32)
        _flash_chunk(s, cb, m_ref, l_ref, acc_ref, everything)

    @pl.when(kj == n_steps - 1)
    def _():
        cb = _pad_rows(cn_ref[...], LANES).astype(BF16)
        rb = _pad_rows(rn_ref[...], LANES).astype(BF16)
        s = _bdot_nt(qlb_ref[...], cb) + _bdot_nt(qrb_ref[...], rb)
        q_t = lax.broadcasted_iota(jnp.int32, (rows, LANES), 0) % t_s
        k_t = lax.broadcasted_iota(jnp.int32, (rows, LANES), 1)
        s = jnp.where(k_t <= q_t, s, NEG_INF)
        _flash_chunk(s, cb, m_ref, l_ref, acc_ref, everything)
        _mla_finish(acc_ref, l_ref, wv_ref, o_ref, t_s)


def _mla_sample_attn(q_lat, q_rope, c_kv, k_r, cache_lat, cache_kr_t, page_table, w_uv_pad, o_full, *,
                     n_seq, t_s, row0, n_pg):
    n_pages = page_table.shape[1]
    page = cache_lat.shape[2]
    assert n_pages % n_pg == 0 and row0 % t_s == 0
    n_steps = n_pages // n_pg + 1
    base = row0 // t_s

    def qmap(b, j, pt):
        return (0, base + b, 0)

    def newmap(b, j, pt):
        return (base + b, 0)

    def pagemap(p):
        return lambda b, j, pt: (0, pt[b, jnp.minimum(j * n_pg + p, n_pages - 1)], 0, 0)

    in_specs = [pl.BlockSpec((MLA_HEADS, t_s, MLA_KV_LORA), qmap),
                pl.BlockSpec((MLA_HEADS, t_s, MLA_ROPE), qmap)]
    in_specs += [pl.BlockSpec((None, None, page, MLA_KV_LORA), pagemap(p)) for p in range(n_pg)]
    in_specs += [pl.BlockSpec((None, None, MLA_ROPE, page), pagemap(p)) for p in range(n_pg)]
    in_specs += [pl.BlockSpec((t_s, MLA_KV_LORA), newmap), pl.BlockSpec((t_s, MLA_ROPE), newmap),
                 pl.BlockSpec(w_uv_pad.shape, lambda b, j, pt: (0, 0, 0)),
                 pl.BlockSpec(memory_space=pl.ANY)]
    n_in = len(in_specs)
    return pl.pallas_call(
        functools.partial(_mla_sample_kernel, n_pg=n_pg, t_s=t_s),
        grid_spec=pltpu.PrefetchScalarGridSpec(
            num_scalar_prefetch=1,
            grid=(n_seq, n_steps),
            in_specs=in_specs,
            out_specs=pl.BlockSpec((t_s, MLA_HEADS * MLA_V), newmap),
            scratch_shapes=_flash_scratch(MLA_HEADS * t_s)),
        out_shape=jax.ShapeDtypeStruct(o_full.shape, F32),
        input_output_aliases={n_in: 0},
        compiler_params=_cparams("parallel", "arbitrary"),
        name="mla_sample_attn",
    )(page_table, q_lat, q_rope, *([cache_lat] * n_pg), *([cache_kr_t] * n_pg), c_kv, k_r, w_uv_pad, o_full)


def _positions(dims, past_len):
    n_p, t_p, n_s, t_s = dims
    return np.concatenate([np.tile(np.arange(t_p), n_p), np.tile(past_len + np.arange(t_s), n_s)])


def _mla_layer(x, nw, cache_lat, cache_kr, page_table, w_in, q_norm_w, w_uq, kv_norm_w, w_uk, w_uv, w_out,
               dims):
    n_p, t_p, n_s, t_s = dims
    t = x.shape[0]
    n_pages = page_table.shape[1]
    past_len = n_pages * cache_lat.shape[2]
    cos, sin = _rope_tables(_positions(dims, past_len), MLA_ROPE // 2)
    cq, ckv, kr = _mla_in(x, nw, w_in, kv_norm_w, cos, sin, tm=_tile(t, 640))
    q_lat, q_rope = _mla_q(cq, q_norm_w, w_uq, w_uk, cos, sin, tm=_tile(t, 256))
    wv = _pad_uv(w_uv)
    o = _mla_prompt_attn(q_lat, q_rope, ckv, kr, wv, n_seq=n_p, seq_len=t_p, tq=min(t_p, 128),
                         tk=min(t_p, 256))
    o = _mla_sample_attn(q_lat, q_rope, ckv, kr, cache_lat, jnp.swapaxes(cache_kr, 2, 3), page_table, wv, o,
                         n_seq=n_s, t_s=t_s, row0=n_p * t_p, n_pg=math.gcd(n_pages, 32))
    x = _linear(o, w_out, res=x, tm=_tile(t, 1280), tn=512, name="mla_out")
    return x, ckv, kr


def _lane_halves(a):
    half = LANES // 2
    low = lax.broadcasted_iota(jnp.int32, a.shape, 1) < half
    rolled = pltpu.roll(a, half, axis=1)
    head0 = (jnp.where(low, a, 0.0), jnp.where(low, 0.0, rolled))
    head1 = (jnp.where(low, rolled, 0.0), jnp.where(low, 0.0, a))
    return head0, head1


def _swa_heads(q, k_all, v_all, sink_ref, mask, o_ref):
    rq = q.shape[0]
    scale = SWA_HD ** -0.5
    top = lax.broadcasted_iota(jnp.int32, (2 * rq, 1), 0) < rq
    for cg in range(SWA_KV_HEADS // 2):
        k_heads = _lane_halves(k_all[:, cg * LANES:(cg + 1) * LANES])
        v_heads = _lane_halves(v_all[:, cg * LANES:(cg + 1) * LANES])
        for sub in range(2):
            kh = 2 * cg + sub
            (k_lo, k_hi), (v_lo, v_hi) = k_heads[sub], v_heads[sub]
            qs = jnp.concatenate([q[:, (2 * kh) * LANES:(2 * kh + 1) * LANES],
                                  q[:, (2 * kh + 1) * LANES:(2 * kh + 2) * LANES]], axis=0)
            acc = None
            for which, (kk, vv) in enumerate(((k_lo, v_lo), (k_hi, v_hi))):
                s = jnp.where(mask, _bdot_nt(qs, kk) * scale, NEG_INF)
                sink = jnp.where(top, sink_ref[4 * kh + which], sink_ref[4 * kh + 2 + which])
                m = jnp.maximum(jnp.max(s, axis=-1, keepdims=True), sink)
                e = jnp.exp(s - m)
                p = e / (jnp.sum(e, axis=-1, keepdims=True) + jnp.exp(sink - m))
                part = _bdot(p, vv)
                acc = part if acc is None else acc + part
            o_ref[:, (2 * kh) * LANES:(2 * kh + 1) * LANES] = acc[:rq]
            o_ref[:, (2 * kh + 1) * LANES:(2 * kh + 2) * LANES] = acc[rq:]


def _swa_prompt_kernel(sink_ref, q_ref, kp_ref, kc_ref, vp_ref, vc_ref, o_ref):
    n = pl.program_id(1)
    w = q_ref.shape[0]
    k_all = jnp.concatenate([kp_ref[...], kc_ref[...]], axis=0)
    v_all = jnp.concatenate([vp_ref[...], vc_ref[...]], axis=0)
    r = lax.broadcasted_iota(jnp.int32, (2 * w, 2 * w), 0) % w
    c = lax.broadcasted_iota(jnp.int32, (2 * w, 2 * w), 1)
    mask = (c >= r) & (c <= r + w) & ((n > 0) | (c >= w))
    _swa_heads(q_ref[...], k_all, v_all, sink_ref, mask, o_ref)


def _swa_prompt_attn(qkv, sinks, *, n_seq, seq_len):
    t = qkv.shape[0]
    w = WINDOW
    nb = seq_len // w
    dq = SWA_HEADS * SWA_HD
    dkv = SWA_KV_HEADS * SWA_HD
    kcol = dq // dkv
    cur = lambda col: (lambda b, n: (b * nb + n, col))
    prev = lambda col: (lambda b, n: (b * nb + jnp.maximum(n - 1, 0), col))
    return pl.pallas_call(
        _swa_prompt_kernel,
        grid=(n_seq, nb),
        in_specs=[pl.BlockSpec(memory_space=pltpu.SMEM),
                  pl.BlockSpec((w, dq), cur(0)),
                  pl.BlockSpec((w, dkv), prev(kcol)), pl.BlockSpec((w, dkv), cur(kcol)),
                  pl.BlockSpec((w, dkv), prev(kcol + 1)), pl.BlockSpec((w, dkv), cur(kcol + 1))],
        out_specs=pl.BlockSpec((w, dq), cur(0)),
        out_shape=jax.ShapeDtypeStruct((t, dq), F32),
        compiler_params=_cparams("parallel", "parallel"),
        name="swa_prompt_attn",
    )(sinks, qkv, qkv, qkv, qkv, qkv)


def _swa_sample_kernel(sink_ref, q_ref, kn_ref, vn_ref, kc_ref, vc_ref, alias_ref, o_ref, ko_ref, vo_ref):
    del alias_ref
    t_s = q_ref.shape[0]
    w = kc_ref.shape[0]
    k_all = jnp.concatenate([kc_ref[...], kn_ref[...]], axis=0)
    v_all = jnp.concatenate([vc_ref[...], vn_ref[...]], axis=0)
    r = lax.broadcasted_iota(jnp.int32, (2 * t_s, w + t_s), 0) % t_s
    c = lax.broadcasted_iota(jnp.int32, (2 * t_s, w + t_s), 1)
    mask = (c <= w + r) & (c >= r)
    _swa_heads(q_ref[...], k_all, v_all, sink_ref, mask, o_ref)
    ko_ref[...] = k_all[t_s:]
    vo_ref[...] = v_all[t_s:]


def _swa_sample_attn(qkv, cache_k, cache_v, sinks, o_full, *, n_seq, t_s, row0):
    w = cache_k.shape[1]
    dq = SWA_HEADS * SWA_HD
    dkv = SWA_KV_HEADS * SWA_HD
    kcol = dq // dkv
    base = row0 // t_s
    new = lambda col: (lambda b: (base + b, col))
    seq = lambda b: (b, 0, 0)
    return pl.pallas_call(
        _swa_sample_kernel,
        grid=(n_seq,),
        in_specs=[pl.BlockSpec(memory_space=pltpu.SMEM),
                  pl.BlockSpec((t_s, dq), new(0)),
                  pl.BlockSpec((t_s, dkv), new(kcol)), pl.BlockSpec((t_s, dkv), new(kcol + 1)),
                  pl.BlockSpec((None, w, dkv), seq), pl.BlockSpec((None, w, dkv), seq),
                  pl.BlockSpec(memory_space=pl.ANY)],
        out_specs=[pl.BlockSpec((t_s, dq), new(0)),
                   pl.BlockSpec((None, w, dkv), seq), pl.BlockSpec((None, w, dkv), seq)],
        out_shape=[jax.ShapeDtypeStruct(o_full.shape, F32),
                   jax.ShapeDtypeStruct(cache_k.shape, F32), jax.ShapeDtypeStruct(cache_v.shape, F32)],
        input_output_aliases={6: 0},
        compiler_params=_cparams("parallel"),
        name="swa_sample_attn",
    )(sinks, qkv, qkv, qkv, cache_k, cache_v, o_full)


def _swa_layer(x, nw, cache_k, cache_v, w_qkv, b_qkv, sinks, w_out, b_out, dims):
    n_p, t_p, n_s, t_s = dims
    t = x.shape[0]
    tm = _tile(t, 1280)
    dq = SWA_HEADS * SWA_HD
    dkv = SWA_KV_HEADS * SWA_HD
    qkv = _linear(x, w_qkv, norm_w=nw, bias=b_qkv, tm=tm, tn=512, name="swa_in")
    o = _swa_prompt_attn(qkv, sinks, n_seq=n_p, seq_len=t_p)
    o, k_s, v_s = _swa_sample_attn(qkv, cache_k.reshape(n_s, WINDOW, dkv), cache_v.reshape(n_s, WINDOW, dkv),
                                   sinks, o, n_seq=n_s, t_s=t_s, row0=n_p * t_p)
    x = _linear(o, w_out, bias=b_out, res=x, tm=tm, tn=512, name="swa_out")
    kv_p = jnp.stack([lax.slice(qkv, ((b + 1) * t_p - WINDOW, dq), ((b + 1) * t_p, dq + 2 * dkv))
                      for b in range(n_p)])
    kv_shape = (n_p, WINDOW, SWA_KV_HEADS, SWA_HD)
    k_p = kv_p[:, :, :dkv].reshape(kv_shape)
    v_p = kv_p[:, :, dkv:].reshape(kv_shape)
    return x, k_p, v_p, k_s.reshape(cache_k.shape), v_s.reshape(cache_v.shape)


def _l2norm(x):
    return x * lax.rsqrt(jnp.sum(x * x, axis=-1, keepdims=True) + 1e-6)


def _split2(x):
    h1 = x.astype(BF16)
    return h1, (x - h1.astype(F32)).astype(BF16)


def _bmm3(a, b):
    a1, a2 = _split2(a)
    b1, b2 = _split2(b)
    dot = lambda x, y: jnp.einsum("bij,bjk->bik", x, y, preferred_element_type=F32)
    return dot(a1, b1) + dot(a1, b2) + dot(a2, b1)


def _unit_lower_inverse(low):
    n = low.shape[-1]
    eye = (lax.broadcasted_iota(jnp.int32, (n, n), 0) == lax.broadcasted_iota(jnp.int32, (n, n), 1))
    eye = eye.astype(F32)[None]
    power = -low
    inv = eye + power
    steps = int(math.log2(n)) - 1
    for _ in range(steps):
        power = _bmm3(power, power)
        inv = inv + _bmm3(inv, power)
    return inv


def _dn_kernel(*refs, rb, zero_init, has_alias):
    it = iter(refs)
    x_ref, z_ref, ab_ref = next(it), next(it), next(it)
    s0_ref, c0_ref = (None, None) if zero_init else (next(it), next(it))
    cw_ref, al_ref, dt_ref, nw_ref = next(it), next(it), next(it), next(it)
    if has_alias:
        next(it)
    o_ref, s_ref, co_ref = next(it), next(it), next(it)
    st_ref, xp_ref = next(it), next(it)
    r = pl.program_id(1)
    rows = max(rb, CHUNK)
    n_chunks = rows // CHUNK
    halo = 8
    dqk = DN_HEADS * DN_HK

    @pl.when(r == 0)
    def _():
        if zero_init:
            st_ref[...] = jnp.zeros(st_ref.shape, F32)
            xp_ref[0:halo] = jnp.zeros((halo, xp_ref.shape[1]), F32)
        else:
            st_ref[...] = s0_ref[0]
            xp_ref[0:halo] = c0_ref[0]

    xp_ref[halo:halo + rb] = x_ref[...]
    full = xp_ref[...]
    conv = full[halo:] * cw_ref[DN_CONV - 1:DN_CONV]
    for w in range(DN_CONV - 1):
        conv = conv + pltpu.roll(full, DN_CONV - 1 - w, axis=0)[halo:] * cw_ref[w:w + 1]
    tail = xp_ref[rb:rb + halo]
    co_ref[0] = tail
    xp_ref[0:halo] = tail
    qkv = _pad_rows(_silu(conv), rows)
    ab = ab_ref[...]
    g_all = _pad_rows(-jnp.exp(al_ref[...]) * _softplus(ab + dt_ref[...]), rows)
    beta_all = _pad_rows(1.0 / (1.0 + jnp.exp(-ab)), rows)

    ri = lax.broadcasted_iota(jnp.int32, (rows, rows), 0)
    ci = lax.broadcasted_iota(jnp.int32, (rows, rows), 1)
    same = (ri // CHUNK) == (ci // CHUNK)
    causal = same & (ci <= ri)
    strict = same & (ci < ri)
    upper = (same & (ri <= ci)).astype(BF16)
    gc_col = _dot_exact_lhs(causal, g_all)
    gt_col = _dot_exact_lhs(same, g_all)
    g1, g2, g3 = _split3(g_all)
    tn = lambda a: lax.dot_general(a, upper, (((0,), (0,)), ((), ())), preferred_element_type=F32)
    gc_row = tn(g1) + tn(g2) + tn(g3)

    lows, rhss, attns, qds, kos, gls = [], [], [], [], [], []
    for h in range(DN_HEADS):
        hs = slice(h * DN_HK, (h + 1) * DN_HK)
        q = _l2norm(qkv[:, hs]) * DN_HK ** -0.5
        k = _l2norm(qkv[:, dqk + h * DN_HK:dqk + (h + 1) * DN_HK])
        v = qkv[:, 2 * dqk + h * DN_HV:2 * dqk + (h + 1) * DN_HV]
        beta = beta_all[:, DN_HEADS + h:DN_HEADS + h + 1]
        gc = gc_col[:, h:h + 1]
        gt = gt_col[:, h:h + 1]
        decay = jnp.where(causal, jnp.exp(jnp.where(causal, gc - gc_row[h:h + 1, :], 0.0)), 0.0)
        kb = k * beta
        low = jnp.where(strict, _bdot_nt(kb, k) * decay, 0.0)
        attn = _bdot_nt(q, k) * decay
        rhs = jnp.concatenate([v * beta, kb * jnp.exp(gc)], axis=-1)
        q_dec = q * jnp.exp(gc)
        k_out = k * jnp.exp(gt - gc)
        g_last = jnp.exp(gt)
        for c in range(n_chunks):
            rs = slice(c * CHUNK, (c + 1) * CHUNK)
            lows.append(low[rs, rs])
            attns.append(attn[rs, rs])
            rhss.append(rhs[rs])
            qds.append(q_dec[rs])
            kos.append(k_out[rs])
            gls.append(g_last[c * CHUNK:c * CHUNK + 1])
    sol = _bmm3(_unit_lower_inverse(jnp.stack(lows)), jnp.stack(rhss))
    n_out = min(rb, CHUNK)
    for h in range(DN_HEADS):
        s = st_ref[h]
        for c in range(n_chunks):
            i = h * n_chunks + c
            u, wm = sol[i, :, :DN_HV], sol[i, :, DN_HV:]
            v_new = u - _bdot(wm, s)
            o = _bdot(qds[i], s) + _bdot(attns[i], v_new)
            s = s * gls[i] + _bdot_tn(kos[i], v_new)
            zs = z_ref[c * CHUNK:c * CHUNK + n_out, h * DN_HV:(h + 1) * DN_HV]
            o_ref[c * CHUNK:c * CHUNK + n_out, h * DN_HV:(h + 1) * DN_HV] = (
                _rms(o[:n_out], nw_ref[...]) * _silu(zs))
        st_ref[h] = s

    @pl.when(r == pl.num_programs(1) - 1)
    def _():
        s_ref[0] = st_ref[...]


def _dn_core(proj, s0, conv0, conv_w, a_log, dt_bias, norm_w, o_full, *, n_seq, seq_len, row0, rb):
    t = proj.shape[0]
    dconv = conv_w.shape[1]
    dz = DN_HEADS * DN_HV
    nblk = seq_len // rb
    base = row0 // rb
    assert row0 % rb == 0 and seq_len % rb == 0 and rb % 8 == 0

    def rowmap(col):
        return lambda b, r: (base + b * nblk + r, col)

    seq4 = lambda b, r: (b, 0, 0, 0)
    seq3 = lambda b, r: (b, 0, 0)
    fix = lambda b, r: (0, 0)
    in_specs = [pl.BlockSpec((rb, dconv), rowmap(0)), pl.BlockSpec((rb, dz), rowmap(dconv // dz)),
                pl.BlockSpec((rb, LANES), rowmap((dconv + dz) // LANES))]
    args = [proj, proj, proj]
    if s0 is not None:
        in_specs += [pl.BlockSpec((1, DN_HEADS, DN_HK, DN_HV), seq4), pl.BlockSpec((1, 8, dconv), seq3)]
        args += [s0, conv0]
    in_specs += [pl.BlockSpec((DN_CONV, dconv), fix), pl.BlockSpec((1, LANES), fix),
                 pl.BlockSpec((1, LANES), fix), pl.BlockSpec((1, DN_HV), fix)]
    args += [conv_w, _pad_cols(a_log.reshape(1, -1), LANES), _pad_cols(dt_bias.reshape(1, -1), LANES),
             norm_w.reshape(1, -1)]
    aliases = {}
    if o_full is not None:
        in_specs.append(pl.BlockSpec(memory_space=pl.ANY))
        aliases = {len(args): 0}
        args.append(o_full)
    kern = functools.partial(_dn_kernel, rb=rb, zero_init=s0 is None, has_alias=o_full is not None)
    return pl.pallas_call(
        kern,
        grid=(n_seq, nblk),
        in_specs=in_specs,
        out_specs=[pl.BlockSpec((rb, dz), rowmap(0)),
                   pl.BlockSpec((1, DN_HEADS, DN_HK, DN_HV), seq4),
                   pl.BlockSpec((1, 8, dconv), seq3)],
        out_shape=[jax.ShapeDtypeStruct((t, dz), F32),
                   jax.ShapeDtypeStruct((n_seq, DN_HEADS, DN_HK, DN_HV), F32),
                   jax.ShapeDtypeStruct((n_seq, 8, dconv), F32)],
        scratch_shapes=[pltpu.VMEM((DN_HEADS, DN_HK, DN_HV), F32), pltpu.VMEM((rb + 8, dconv), F32)],
        input_output_aliases=aliases,
        compiler_params=_cparams("parallel", "arbitrary"),
        name="dn_core",
    )(*args)


def _dn_layer(x, nw, s0_s, conv0_s, w_in, conv_w, a_log, dt_bias, norm_w, w_out, dims):
    n_p, t_p, n_s, t_s = dims
    t = x.shape[0]
    tm = _tile(t, 1280)
    dconv = conv_w.shape[1]
    dz = DN_HEADS * DN_HV
    width = dconv + dz + LANES
    proj = _linear(x, _pad_cols(w_in, width), norm_w=nw, tm=tm, tn=_tile(width, 384), name="dn_in")
    o, sp, cp = _dn_core(proj, None, None, conv_w, a_log, dt_bias, norm_w, None, n_seq=n_p, seq_len=t_p,
                         row0=0, rb=min(t_p, 128))
    conv0 = jnp.pad(conv0_s, ((0, 0), (8 - conv0_s.shape[1], 0), (0, 0)))
    o, ss, cs = _dn_core(proj, s0_s, conv0, conv_w, a_log, dt_bias, norm_w, o, n_seq=n_s, seq_len=t_s,
                         row0=n_p * t_p, rb=t_s)
    x = _linear(o, w_out, res=x, tm=tm, tn=512, name="dn_out")
    keep = DN_CONV - 1
    return x, sp, ss, cp[:, 8 - keep:], cs[:, 8 - keep:]


def kernel(x_prompt, x_sample, state_gla, cache_mla_latent, cache_mla_krope, cache_swa_k, cache_swa_v, state_delta, state_delta_conv, page_table, norm_w, final_norm_w, gla_w_in, gla_w_gk2, gla_b_gk2, gla_norm_w, gla_w_out, mla_w_in, mla_q_norm_w, mla_w_uq, mla_kv_norm_w, mla_w_uk, mla_w_uv, mla_w_out, swa_w_qkv, swa_b_qkv, swa_sinks, swa_w_out, swa_b_out, dn_w_in, dn_conv_w, dn_a_log, dn_dt_bias, dn_norm_w, dn_w_out, ffn_w_gate, ffn_w_up, ffn_w_down, moe_w_router, moe_w_gate, moe_w_up, moe_w_down):
    n_p, t_p, d = x_prompt.shape
    n_s, t_s, _ = x_sample.shape
    dims = (n_p, t_p, n_s, t_s)
    x = jnp.concatenate([x_prompt.reshape(n_p * t_p, d), x_sample.reshape(n_s * t_s, d)], axis=0)
    t = x.shape[0]
    tm = _tile(t, 1280)
    n_tp = n_p * t_p

    x, gla_p, gla_s = _gla_layer(x, norm_w[0, 0], state_gla[0], gla_w_in[0], gla_w_gk2[0], gla_b_gk2[0],
                                 gla_norm_w[0], gla_w_out[0], dims)
    x = _ffn(x, norm_w[0, 1], ffn_w_gate, ffn_w_up, ffn_w_down, 0, tm=tm, tf=256)
    moe_tiles = dict(tm=_tile(t, 640), tp=1280, tf=512)

    x, ckv, kr = _mla_layer(x, norm_w[1, 0], cache_mla_latent[0:1], cache_mla_krope[0:1], page_table,
                            mla_w_in[0], mla_q_norm_w[0], mla_w_uq[0], mla_kv_norm_w[0], mla_w_uk[0],
                            mla_w_uv[0], mla_w_out[0], dims)
    x = _moe(x, norm_w[1, 1], moe_w_router[0], moe_w_gate, moe_w_up, moe_w_down, 0, **moe_tiles)

    x, swk_p, swv_p, swk_s, swv_s = _swa_layer(x, norm_w[2, 0], cache_swa_k[0], cache_swa_v[0],
                                               swa_w_qkv[0], swa_b_qkv[0], swa_sinks[0], swa_w_out[0],
                                               swa_b_out[0], dims)
    x = _ffn(x, norm_w[2, 1], ffn_w_gate, ffn_w_up, ffn_w_down, 1, tm=tm, tf=256)

    x, dn_p, dn_s, cv_p, cv_s = _dn_layer(x, norm_w[3, 0], state_delta[0], state_delta_conv[0], dn_w_in[0],
                                          dn_conv_w[0], dn_a_log[0], dn_dt_bias[0], dn_norm_w[0],
                                          dn_w_out[0], dims)
    y = _moe(x, norm_w[3, 1], moe_w_router[1], moe_w_gate, moe_w_up, moe_w_down, 1, final_norm_w,
             **moe_tiles)

    lead = lambda a: a[None]
    return (y[:n_tp].reshape(n_p, t_p, d), y[n_tp:].reshape(n_s, t_s, d),
            lead(gla_p), lead(gla_s),
            lead(ckv[:n_tp].reshape(n_p, t_p, -1)), lead(ckv[n_tp:].reshape(n_s, t_s, -1)),
            lead(kr[:n_tp].reshape(n_p, t_p, -1)), lead(kr[n_tp:].reshape(n_s, t_s, -1)),
            lead(swk_p), lead(swk_s), lead(swv_p), lead(swv_s),
            lead(dn_p), lead(dn_s), lead(cv_p), lead(cv_s))
```

```python
import functools
import math

import jax
import jax.numpy as jnp
import numpy as np
from jax import lax
from jax.experimental import pallas as pl
from jax.experimental.pallas import tpu as pltpu

F32 = jnp.float32
BF16 = jnp.bfloat16

NORM_EPS = 1e-6
GLA_HEADS = 4
GLA_GATE_RANK = 16
GLA_GATE_NORMALIZER = 16.0
CHUNK = 64
MLA_HEADS = 16
MLA_Q_LORA = 384
MLA_KV_LORA = 256
MLA_NOPE = 64
MLA_ROPE = 32
MLA_V = 64
ROPE_THETA = 10000.0
SWA_HEADS = 16
SWA_KV_HEADS = 4
SWA_HD = 64
WINDOW = 128
DN_HEADS = 8
DN_HK = 128
DN_HV = 128
DN_CONV = 4
N_EXPERTS = 8

LANES = 128
VMEM_LIMIT = 56 * 1024 * 1024
NEG_INF = float("-inf")


def _cparams(*sem):
    return pltpu.CompilerParams(dimension_semantics=sem, vmem_limit_bytes=VMEM_LIMIT)


def _bdot(a, b):
    return jnp.dot(a.astype(BF16), b.astype(BF16), preferred_element_type=F32)


def _bdot_nt(a, b):
    return lax.dot_general(a.astype(BF16), b.astype(BF16), (((1,), (1,)), ((), ())),
                           preferred_element_type=F32)


def _bdot_tn(a, b):
    return lax.dot_general(a.astype(BF16), b.astype(BF16), (((0,), (0,)), ((), ())),
                           preferred_element_type=F32)


def _split3(x):
    h1 = x.astype(BF16)
    r1 = x - h1.astype(F32)
    h2 = r1.astype(BF16)
    h3 = (r1 - h2.astype(F32)).astype(BF16)
    return h1, h2, h3


def _dot_exact_lhs(m, x):
    mb = m.astype(BF16)
    h1, h2, h3 = _split3(x)
    return (jnp.dot(mb, h1, preferred_element_type=F32) + jnp.dot(mb, h2, preferred_element_type=F32)
            + jnp.dot(mb, h3, preferred_element_type=F32))


def _rms(x, w):
    return x * lax.rsqrt(jnp.mean(x * x, axis=-1, keepdims=True) + NORM_EPS) * w


def _silu(x):
    return x / (1.0 + jnp.exp(-x))


def _log_sigmoid(x):
    return jnp.minimum(x, 0.0) - jnp.log(1.0 + jnp.exp(-jnp.abs(x)))


def _softplus(x):
    return jnp.maximum(x, 0.0) + jnp.log(1.0 + jnp.exp(-jnp.abs(x)))


def _linear_kernel(*refs, has_norm, has_bias, has_res):
    it = iter(refs)
    x_ref = next(it)
    nw_ref = next(it) if has_norm else None
    w_ref = next(it)
    b_ref = next(it) if has_bias else None
    r_ref = next(it) if has_res else None
    o_ref = next(it)
    h_ref = next(it)

    @pl.when(pl.program_id(1) == 0)
    def _():
        xv = x_ref[...].astype(F32)
        if has_norm:
            xv = _rms(xv, nw_ref[...])
        h_ref[...] = xv.astype(BF16)

    acc = jnp.dot(h_ref[...], w_ref[...].astype(BF16), preferred_element_type=F32)
    if has_bias:
        acc = acc + b_ref[...]
    if has_res:
        acc = acc + r_ref[...]
    o_ref[...] = acc.astype(o_ref.dtype)


def _linear(x, w, *, norm_w=None, bias=None, res=None, tm, tn, out_dtype=F32, name="linear"):
    t, k = x.shape
    n = w.shape[1]
    assert t % tm == 0 and n % tn == 0, (t, tm, n, tn)
    in_specs = [pl.BlockSpec((tm, k), lambda i, j: (i, 0))]
    args = [x]
    if norm_w is not None:
        in_specs.append(pl.BlockSpec((1, k), lambda i, j: (0, 0)))
        args.append(norm_w.reshape(1, k))
    in_specs.append(pl.BlockSpec((k, tn), lambda i, j: (0, j)))
    args.append(w)
    if bias is not None:
        in_specs.append(pl.BlockSpec((1, tn), lambda i, j: (0, j)))
        args.append(bias.reshape(1, n))
    if res is not None:
        in_specs.append(pl.BlockSpec((tm, tn), lambda i, j: (i, j)))
        args.append(res)
    kern = functools.partial(_linear_kernel, has_norm=norm_w is not None, has_bias=bias is not None,
                             has_res=res is not None)
    return pl.pallas_call(
        kern,
        grid=(t // tm, n // tn),
        in_specs=in_specs,
        out_specs=pl.BlockSpec((tm, tn), lambda i, j: (i, j)),
        out_shape=jax.ShapeDtypeStruct((t, n), out_dtype),
        scratch_shapes=[pltpu.VMEM((tm, k), BF16)],
        compiler_params=_cparams("parallel", "arbitrary"),
        name=name,
    )(*args)


def _swiglu_acc(h_ref, wg_ref, wu_ref, wd_ref, acc_ref):
    h = h_ref[...].astype(BF16)
    g = jnp.dot(h, wg_ref[...].astype(BF16), preferred_element_type=F32)
    u = jnp.dot(h, wu_ref[...].astype(BF16), preferred_element_type=F32)
    a = (_silu(g) * u).astype(BF16)
    acc_ref[...] += jnp.dot(a, wd_ref[...].astype(BF16), preferred_element_type=F32)


def _ffn_kernel(x_ref, nw_ref, wg_ref, wu_ref, wd_ref, o_ref, h_ref, acc_ref):
    f = pl.program_id(1)

    @pl.when(f == 0)
    def _():
        h_ref[...] = _rms(x_ref[...], nw_ref[...]).astype(BF16)
        acc_ref[...] = jnp.zeros_like(acc_ref)

    _swiglu_acc(h_ref, wg_ref, wu_ref, wd_ref, acc_ref)

    @pl.when(f == pl.num_programs(1) - 1)
    def _():
        o_ref[...] = x_ref[...] + acc_ref[...]


def _ffn(x, norm_w, wg, wu, wd, layer, *, tm, tf):
    t, d = x.shape
    ff = wg.shape[2]
    assert t % tm == 0 and ff % tf == 0
    return pl.pallas_call(
        _ffn_kernel,
        grid=(t // tm, ff // tf),
        in_specs=[pl.BlockSpec((tm, d), lambda i, f: (i, 0)),
                  pl.BlockSpec((1, d), lambda i, f: (0, 0)),
                  pl.BlockSpec((None, d, tf), lambda i, f: (layer, 0, f)),
                  pl.BlockSpec((None, d, tf), lambda i, f: (layer, 0, f)),
                  pl.BlockSpec((None, tf, d), lambda i, f: (layer, f, 0))],
        out_specs=pl.BlockSpec((tm, d), lambda i, f: (i, 0)),
        out_shape=jax.ShapeDtypeStruct((t, d), F32),
        scratch_shapes=[pltpu.VMEM((tm, d), BF16), pltpu.VMEM((tm, d), F32)],
        compiler_params=_cparams("parallel", "arbitrary"),
        name="ffn",
    )(x, norm_w.reshape(1, d), wg, wu, wd)


MOE_CHUNK = 256
MOE_ALIGN = 16
MOE_SLOTS = 4


def _route_kernel(x_ref, nw_ref, wrt_ref, h_ref, sel_ref, gate_ref, rank_ref, cnt_ref):
    hn = _rms(x_ref[...], nw_ref[...])
    h_ref[...] = hn.astype(BF16)
    logits = lax.dot_general(wrt_ref[...], hn, (((1,), (1,)), ((), ())), preferred_element_type=F32,
                             precision=lax.Precision.HIGHEST)
    n_exp, tm = logits.shape
    sub = lax.broadcasted_iota(jnp.int32, logits.shape, 0)
    m1 = jnp.max(logits, axis=0, keepdims=True)
    i1 = jnp.min(jnp.where(logits == m1, sub, n_exp), axis=0, keepdims=True)
    rest = jnp.where(sub == i1, NEG_INF, logits)
    m2 = jnp.max(rest, axis=0, keepdims=True)
    i2 = jnp.min(jnp.where(rest == m2, sub, n_exp), axis=0, keepdims=True)
    e2 = jnp.exp(m2 - m1)
    first, second = sub == i1, sub == i2
    sel = jnp.where(first | second, 1.0, 0.0)
    upper = jnp.where(lax.broadcasted_iota(jnp.int32, (tm, tm), 0)
                      <= lax.broadcasted_iota(jnp.int32, (tm, tm), 1), 1.0, 0.0)
    cum = _bdot(sel, upper)
    sel_ref[...] = sel
    gate_ref[...] = jnp.where(first, 1.0 / (1.0 + e2), 0.0) + jnp.where(second, e2 / (1.0 + e2), 0.0)
    rank_ref[...] = cum - sel
    cnt_ref[0] = jnp.broadcast_to(cum[:, tm - 1:tm], (n_exp, LANES))


def _route(x, norm_w, w_router, *, tm):
    t, d = x.shape
    n_exp = w_router.shape[1]
    et = lambda i: (0, i)
    return pl.pallas_call(
        _route_kernel,
        grid=(t // tm,),
        in_specs=[pl.BlockSpec((tm, d), lambda i: (i, 0)), pl.BlockSpec((1, d), lambda i: (0, 0)),
                  pl.BlockSpec((n_exp, d), lambda i: (0, 0))],
        out_specs=[pl.BlockSpec((tm, d), lambda i: (i, 0)), pl.BlockSpec((n_exp, tm), et),
                   pl.BlockSpec((n_exp, tm), et), pl.BlockSpec((n_exp, tm), et),
                   pl.BlockSpec((1, n_exp, LANES), lambda i: (i, 0, 0))],
        out_shape=[jax.ShapeDtypeStruct((t, d), BF16), jax.ShapeDtypeStruct((n_exp, t), F32),
                   jax.ShapeDtypeStruct((n_exp, t), F32), jax.ShapeDtypeStruct((n_exp, t), F32),
                   jax.ShapeDtypeStruct((t // tm, n_exp, LANES), F32)],
        compiler_params=_cparams("parallel"),
        name="moe_route",
    )(x, norm_w.reshape(1, d), w_router.T)


def _moe_plan(cnt, *, tm, tp, p_rows):
    n_tiles, n_exp = cnt.shape
    seg = (cnt + MOE_ALIGN - 1) // MOE_ALIGN * MOE_ALIGN
    total = jnp.sum(seg, axis=0)
    in_group = jnp.cumsum(seg, axis=0) - seg
    nch = (cnt + MOE_CHUNK - 1) // MOE_CHUNK
    reach = jnp.max(in_group + nch * MOE_CHUNK, axis=0)
    gsize = (jnp.maximum(total, reach) + tp - 1) // tp * tp
    gstart = jnp.cumsum(gsize) - gsize
    seg_start = gstart[None, :] + in_group
    cum_e = jnp.cumsum(nch, axis=1)
    q_max = n_exp + 2 * tm // MOE_CHUNK
    q = jnp.arange(q_max, dtype=jnp.int32)[None, :]
    flat_e = jnp.minimum(jnp.sum(q[:, :, None] >= cum_e[:, None, :], axis=-1), n_exp - 1).astype(jnp.int32)
    flat_c = q - jnp.take_along_axis(cum_e - nch, flat_e, axis=1)
    flat_row = jnp.take_along_axis(seg_start, flat_e, axis=1) + flat_c * MOE_CHUNK
    row_j = jnp.arange(p_rows // tp, dtype=jnp.int32) * tp
    tile_e = jnp.minimum(jnp.sum(row_j[:, None] >= (gstart + gsize)[None, :], axis=-1), n_exp - 1)
    tile_valid = row_j < jnp.take(gstart + total, tile_e)
    i32 = lambda a: a.astype(jnp.int32)
    return (i32(cum_e[:, -1]), i32(flat_e.reshape(-1)), i32(flat_c.reshape(-1)), i32(flat_row.reshape(-1)),
            i32(tile_e), i32(tile_valid))


def _dispatch_kernel(nq_ref, fe_ref, fc_ref, fr_ref, h_ref, sel_ref, rank_ref, xs_in, xs_ref, stage, sem,
                     *, q_max):
    del xs_in
    i = pl.program_id(0)
    n = nq_ref[i]
    h = h_ref[...]

    def chunk_copy(slot, row):
        return pltpu.make_async_copy(stage.at[slot], xs_ref.at[pl.ds(pl.multiple_of(row, MOE_ALIGN), MOE_CHUNK)],
                                     sem.at[slot])

    def body(q, carry):
        slot = q % MOE_SLOTS
        e = fe_ref[i * q_max + q]
        c = fc_ref[i * q_max + q]

        @pl.when(q >= MOE_SLOTS)
        def _():
            chunk_copy(slot, 0).wait()

        pos = jnp.where(sel_ref[pl.ds(e, 1), :] > 0.0, rank_ref[pl.ds(e, 1), :], -1.0)
        want = (c * MOE_CHUNK + lax.broadcasted_iota(jnp.int32, (MOE_CHUNK, 1), 0)).astype(F32)
        pick = jnp.where(pos == want, 1.0, 0.0).astype(BF16)
        stage[slot] = jnp.dot(pick, h, preferred_element_type=F32)
        chunk_copy(slot, fr_ref[i * q_max + q]).start()
        return carry

    lax.fori_loop(0, n, body, 0)
    for s in range(MOE_SLOTS):
        @pl.when(n > s)
        def _():
            chunk_copy(s, 0).wait()


def _dispatch(h, sel, rank, plan, *, tm, p_rows):
    t, d = h.shape
    n_exp = sel.shape[0]
    n_flat, flat_e, flat_c, flat_row = plan[:4]
    q_max = flat_e.shape[0] // (t // tm)
    et = lambda i, *_: (0, i)
    return pl.pallas_call(
        functools.partial(_dispatch_kernel, q_max=q_max),
        grid_spec=pltpu.PrefetchScalarGridSpec(
            num_scalar_prefetch=4,
            grid=(t // tm,),
            in_specs=[pl.BlockSpec((tm, d), lambda i, *_: (i, 0)), pl.BlockSpec((n_exp, tm), et),
                      pl.BlockSpec((n_exp, tm), et), pl.BlockSpec(memory_space=pl.ANY)],
            out_specs=pl.BlockSpec(memory_space=pl.ANY),
            scratch_shapes=[pltpu.VMEM((MOE_SLOTS, MOE_CHUNK, d), F32),
                            pltpu.SemaphoreType.DMA((MOE_SLOTS,))]),
        out_shape=jax.ShapeDtypeStruct((p_rows, d), F32),
        input_output_aliases={7: 0},
        compiler_params=_cparams("arbitrary"),
        name="moe_dispatch",
    )(n_flat, flat_e, flat_c, flat_row, h, sel, rank, jnp.zeros((p_rows, d), F32))


def _gffn_kernel(te_ref, tv_ref, x_ref, wg_ref, wu_ref, wd_ref, o_ref, acc_ref):
    del te_ref
    j = pl.program_id(0)
    f = pl.program_id(1)
    valid = tv_ref[j] > 0

    @pl.when(f == 0)
    def _():
        acc_ref[...] = jnp.zeros_like(acc_ref)

    @pl.when(valid)
    def _():
        _swiglu_acc(x_ref, wg_ref, wu_ref, wd_ref, acc_ref)

    @pl.when(f == pl.num_programs(1) - 1)
    def _():
        o_ref[...] = acc_ref[...].astype(o_ref.dtype)


def _gffn(xs, wg, wu, wd, layer, tile_e, tile_valid, *, tp, tf):
    p_rows, d = xs.shape
    ff = wg.shape[3]

    def wmap(is_down):
        def index(j, f, te, tv):
            fi = jnp.where(tv[j] > 0, f, 0)
            return (layer, te[j], fi, 0) if is_down else (layer, te[j], 0, fi)
        return index

    return pl.pallas_call(
        _gffn_kernel,
        grid_spec=pltpu.PrefetchScalarGridSpec(
            num_scalar_prefetch=2,
            grid=(p_rows // tp, ff // tf),
            in_specs=[pl.BlockSpec((tp, d), lambda j, f, te, tv: (j, 0)),
                      pl.BlockSpec((None, None, d, tf), wmap(False)),
                      pl.BlockSpec((None, None, d, tf), wmap(False)),
                      pl.BlockSpec((None, None, tf, d), wmap(True))],
            out_specs=pl.BlockSpec((tp, d), lambda j, f, te, tv: (j, 0)),
            scratch_shapes=[pltpu.VMEM((tp, d), F32)]),
        out_shape=jax.ShapeDtypeStruct((p_rows, d), BF16),
        compiler_params=_cparams("parallel", "arbitrary"),
        name="moe_ffn",
    )(tile_e, tile_valid, xs, wg, wu, wd)


def _combine_kernel(*refs, q_max, final_norm):
    nq_ref, fe_ref, fc_ref, fr_ref = refs[:4]
    x_ref, pos_ref, gate_ref = refs[4:7]
    fw_ref = refs[7] if final_norm else None
    ys_ref, o_ref, acc_ref, buf, sem = refs[7 + int(final_norm):]
    i = pl.program_id(0)
    n = nq_ref[i]
    acc_ref[...] = jnp.zeros_like(acc_ref)
    lane = lax.broadcasted_iota(jnp.int32, pos_ref.shape, 1)

    def chunk_copy(slot, row):
        return pltpu.make_async_copy(ys_ref.at[pl.ds(pl.multiple_of(row, MOE_ALIGN), MOE_CHUNK)], buf.at[slot],
                                     sem.at[slot])

    @pl.when(n > 0)
    def _():
        chunk_copy(0, fr_ref[i * q_max]).start()

    def body(q, carry):
        slot = q % 2
        e = fe_ref[i * q_max + q]
        c = fc_ref[i * q_max + q]

        @pl.when(q + 1 < n)
        def _():
            chunk_copy(1 - slot, fr_ref[i * q_max + q + 1]).start()

        chunk_copy(slot, 0).wait()
        pos = jnp.sum(jnp.where(lane == e, pos_ref[...], 0.0), axis=1, keepdims=True)
        gate = jnp.sum(jnp.where(lane == e, gate_ref[...], 0.0), axis=1, keepdims=True)
        want = (c * MOE_CHUNK + lax.broadcasted_iota(jnp.int32, (1, MOE_CHUNK), 1)).astype(F32)
        pick = jnp.where(pos == want, 1.0, 0.0).astype(BF16)
        acc_ref[...] += gate * jnp.dot(pick, buf[slot], preferred_element_type=F32)
        return carry

    lax.fori_loop(0, n, body, 0)
    y = x_ref[...] + acc_ref[...]
    if final_norm:
        y = _rms(y, fw_ref[...])
    o_ref[...] = y


def _combine(x, ys, pos_tok, gate_tok, plan, final_w, *, tm):
    t, d = x.shape
    n_exp = pos_tok.shape[1]
    n_flat, flat_e, flat_c, flat_row = plan[:4]
    q_max = flat_e.shape[0] // (t // tm)
    row = lambda i, *_: (i, 0)
    in_specs = [pl.BlockSpec((tm, d), row), pl.BlockSpec((tm, n_exp), row), pl.BlockSpec((tm, n_exp), row)]
    args = [x, pos_tok, gate_tok]
    if final_w is not None:
        in_specs.append(pl.BlockSpec((1, d), lambda i, *_: (0, 0)))
        args.append(final_w.reshape(1, d))
    in_specs.append(pl.BlockSpec(memory_space=pl.ANY))
    args.append(ys)
    return pl.pallas_call(
        functools.partial(_combine_kernel, q_max=q_max, final_norm=final_w is not None),
        grid_spec=pltpu.PrefetchScalarGridSpec(
            num_scalar_prefetch=4,
            grid=(t // tm,),
            in_specs=in_specs,
            out_specs=pl.BlockSpec((tm, d), row),
            scratch_shapes=[pltpu.VMEM((tm, d), F32), pltpu.VMEM((2, MOE_CHUNK, d), BF16),
                            pltpu.SemaphoreType.DMA((2,))]),
        out_shape=jax.ShapeDtypeStruct((t, d), F32),
        compiler_params=_cparams("arbitrary"),
        name="moe_combine",
    )(n_flat, flat_e, flat_c, flat_row, *args)


def _moe(x, norm_w, w_router, wg, wu, wd, layer, final_w=None, *, tm, tp, tf):
    t, d = x.shape
    n_exp = w_router.shape[1]
    n_tiles = t // tm
    worst = 2 * t + n_tiles * n_exp * (MOE_ALIGN - 1) + n_exp * (MOE_CHUNK + tp - 1)
    p_rows = (worst + tp - 1) // tp * tp
    h, sel, gate, rank, cnt = _route(x, norm_w, w_router, tm=tm)
    plan = _moe_plan(cnt[:, :, 0].astype(jnp.int32), tm=tm, tp=tp, p_rows=p_rows)
    xs = _dispatch(h, sel, rank, plan, tm=tm, p_rows=p_rows)
    ys = _gffn(xs, wg, wu, wd, layer, plan[4], plan[5], tp=tp, tf=tf)
    pos_tok = jnp.where(sel > 0.0, rank, -1.0).T
    return _combine(x, ys, pos_tok, gate.T, plan, final_w, tm=tm)


def _pad_rows(x, rows):
    if x.shape[0] == rows:
        return x
    return jnp.concatenate([x, jnp.zeros((rows - x.shape[0], x.shape[1]), x.dtype)], axis=0)


def _chunk_masks(rows):
    ri = lax.broadcasted_iota(jnp.int32, (rows, rows), 0)
    ci = lax.broadcasted_iota(jnp.int32, (rows, rows), 1)
    same = (ri // CHUNK) == (ci // CHUNK)
    return same & (ci <= ri), same


def _gla_kernel(*refs, rb, zero_init, has_alias):
    it = iter(refs)
    q_ref, k_ref, v_ref, g_ref, gk_ref = next(it), next(it), next(it), next(it), next(it)
    s0_ref = None if zero_init else next(it)
    wgk_ref, bgk_ref, nw_ref = next(it), next(it), next(it)
    if has_alias:
        next(it)
    o_ref, s_ref, st_ref = next(it), next(it), next(it)
    r = pl.program_id(1)
    rows = max(rb, CHUNK)
    n_chunks = rows // CHUNK
    hk = q_ref.shape[1] // GLA_HEADS
    hv = v_ref.shape[1] // GLA_HEADS

    @pl.when(r == 0)
    def _():
        for h in range(GLA_HEADS):
            if zero_init:
                st_ref[h] = jnp.zeros(st_ref.shape[1:], F32)
            else:
                st_ref[h] = s0_ref[0, h].T

    q = _pad_rows(q_ref[...] * hk ** -0.5, rows)
    k = _pad_rows(k_ref[...], rows)
    v = _pad_rows(v_ref[...], rows)
    la = _log_sigmoid(_bdot(gk_ref[...], wgk_ref[...]) + bgk_ref[...]) * (1.0 / GLA_GATE_NORMALIZER)
    la = _pad_rows(la, rows)
    causal, same = _chunk_masks(rows)
    gc = _dot_exact_lhs(causal, la)
    gt = _dot_exact_lhs(same, la)
    q_in = q * jnp.exp(gc)
    k_in = k * jnp.exp(-gc)
    k_out = k * jnp.exp(gt - gc)
    e_tot = jnp.exp(gt)
    tri = causal[:CHUNK, :CHUNK]
    for h in range(GLA_HEADS):
        st = st_ref[h]
        ks = slice(h * hk, (h + 1) * hk)
        vs = slice(h * hv, (h + 1) * hv)
        for c in range(n_chunks):
            rs = slice(c * CHUNK, (c + 1) * CHUNK)
            qi, ki, ko, vh = q_in[rs, ks], k_in[rs, ks], k_out[rs, ks], v[rs, vs]
            intra = jnp.where(tri, _bdot_nt(qi, ki), 0.0)
            o = _bdot(intra, vh) + _bdot_nt(qi, st)
            st = st * e_tot[c * CHUNK:c * CHUNK + 1, ks] + _bdot_tn(vh, ko)
            n_out = min(rb, CHUNK)
            og = _rms(o[:n_out], nw_ref[...]) * _silu(g_ref[c * CHUNK:c * CHUNK + n_out, vs])
            o_ref[c * CHUNK:c * CHUNK + n_out, vs] = og
        st_ref[h] = st

    @pl.when(r == pl.num_programs(1) - 1)
    def _():
        for h in range(GLA_HEADS):
            s_ref[0, h] = st_ref[h].T


def _gla_core(proj, s0, w_gk2, b_gk2, norm_w, o_full, *, n_seq, seq_len, row0, rb):
    t = proj.shape[0]
    dk = w_gk2.shape[1]
    dv = 2 * dk
    hk, hv = dk // GLA_HEADS, dv // GLA_HEADS
    nblk = seq_len // rb
    base = row0 // rb
    assert row0 % rb == 0 and seq_len % rb == 0

    def rowmap(col):
        return lambda b, r: (base + b * nblk + r, col)

    in_specs = [pl.BlockSpec((rb, dk), rowmap(0)), pl.BlockSpec((rb, dk), rowmap(1)),
                pl.BlockSpec((rb, dv), rowmap(1)), pl.BlockSpec((rb, dv), rowmap(2)),
                pl.BlockSpec((rb, LANES), rowmap((2 * dk + 2 * dv) // LANES))]
    args = [proj, proj, proj, proj, proj]
    if s0 is not None:
        in_specs.append(pl.BlockSpec((1, GLA_HEADS, hk, hv), lambda b, r: (b, 0, 0, 0)))
        args.append(s0)
    wgk = jnp.pad(w_gk2, ((0, LANES - w_gk2.shape[0]), (0, 0)))
    in_specs += [pl.BlockSpec((LANES, dk), lambda b, r: (0, 0)),
                 pl.BlockSpec((1, dk), lambda b, r: (0, 0)),
                 pl.BlockSpec((1, hv), lambda b, r: (0, 0))]
    args += [wgk, b_gk2.reshape(1, dk), norm_w.reshape(1, hv)]
    aliases = {}
    if o_full is not None:
        in_specs.append(pl.BlockSpec(memory_space=pl.ANY))
        aliases = {len(args): 0}
        args.append(o_full)
    kern = functools.partial(_gla_kernel, rb=rb, zero_init=s0 is None, has_alias=o_full is not None)
    return pl.pallas_call(
        kern,
        grid=(n_seq, nblk),
        in_specs=in_specs,
        out_specs=[pl.BlockSpec((rb, dv), rowmap(0)),
                   pl.BlockSpec((1, GLA_HEADS, hk, hv), lambda b, r: (b, 0, 0, 0))],
        out_shape=[jax.ShapeDtypeStruct((t, dv), F32),
                   jax.ShapeDtypeStruct((n_seq, GLA_HEADS, hk, hv), F32)],
        scratch_shapes=[pltpu.VMEM((GLA_HEADS, hv, hk), F32)],
        input_output_aliases=aliases,
        compiler_params=_cparams("parallel", "arbitrary"),
        name="gla_core",
    )(*args)


def _tile(n, pref):
    if n <= pref:
        return n
    for c in range(pref, 7, -8):
        if n % c == 0:
            return c
    return n


def _pad_cols(w, n):
    return jnp.pad(w, ((0, 0), (0, n - w.shape[1])))


def _gla_layer(x, nw, s0_s, w_in, w_gk2, b_gk2, norm_w, w_out, dims):
    n_p, t_p, n_s, t_s = dims
    t = x.shape[0]
    tm = _tile(t, 1280)
    dk = w_gk2.shape[1]
    width = 6 * dk + LANES
    proj = _linear(x, _pad_cols(w_in, width), norm_w=nw, tm=tm, tn=_tile(width, 640), name="gla_in")
    o, sp = _gla_core(proj, None, w_gk2, b_gk2, norm_w, None, n_seq=n_p, seq_len=t_p, row0=0,
                      rb=min(t_p, 256))
    o, ss = _gla_core(proj, s0_s, w_gk2, b_gk2, norm_w, o, n_seq=n_s, seq_len=t_s, row0=n_p * t_p,
                      rb=t_s)
    x = _linear(o, w_out, res=x, tm=tm, tn=512, name="gla_out")
    return x, sp, ss


def _rope_tables(pos, half):
    freqs = np.exp(-math.log(ROPE_THETA) * np.arange(half, dtype=np.float64) / half)
    ang = np.asarray(pos, np.float64)[:, None] * freqs[None, :]
    cos, sin = np.cos(ang), np.sin(ang)
    return (jnp.asarray(np.concatenate([cos, cos], axis=-1), F32),
            jnp.asarray(np.concatenate([-sin, sin], axis=-1), F32))


def _swap_halves(w, axis=-1):
    a, b = jnp.split(w, 2, axis=axis)
    return jnp.concatenate([b, a], axis=axis)


def _mla_in_kernel(x_ref, nw_ref, w_ref, kvw_ref, cos_ref, sin_ref, cq_ref, ckv_ref, kr_ref):
    h = _rms(x_ref[...], nw_ref[...])
    y = _bdot(h, w_ref[...])
    cq_ref[...] = y[:, :MLA_Q_LORA]
    ckv_ref[...] = _rms(y[:, MLA_Q_LORA:MLA_Q_LORA + MLA_KV_LORA], kvw_ref[...])
    o = MLA_Q_LORA + MLA_KV_LORA
    kr_ref[...] = (y[:, o:o + MLA_ROPE] * cos_ref[...]
                   + y[:, o + LANES:o + LANES + MLA_ROPE] * sin_ref[...])


def _mla_in(x, nw, w_in, kv_norm_w, cos, sin, *, tm):
    t, d = x.shape
    o = MLA_Q_LORA + MLA_KV_LORA
    kr_w = w_in[:, o:o + MLA_ROPE]
    w_aug = jnp.concatenate([w_in[:, :o], _pad_cols(kr_w, LANES), _pad_cols(_swap_halves(kr_w), LANES)],
                            axis=1)
    wid = w_aug.shape[1]
    row = lambda i: (i, 0)
    fix = lambda i: (0, 0)
    return pl.pallas_call(
        _mla_in_kernel,
        grid=(t // tm,),
        in_specs=[pl.BlockSpec((tm, d), row), pl.BlockSpec((1, d), fix), pl.BlockSpec((d, wid), fix),
                  pl.BlockSpec((1, MLA_KV_LORA), fix), pl.BlockSpec((tm, MLA_ROPE), row),
                  pl.BlockSpec((tm, MLA_ROPE), row)],
        out_specs=[pl.BlockSpec((tm, MLA_Q_LORA), row), pl.BlockSpec((tm, MLA_KV_LORA), row),
                   pl.BlockSpec((tm, MLA_ROPE), row)],
        out_shape=[jax.ShapeDtypeStruct((t, MLA_Q_LORA), F32), jax.ShapeDtypeStruct((t, MLA_KV_LORA), F32),
                   jax.ShapeDtypeStruct((t, MLA_ROPE), F32)],
        compiler_params=_cparams("parallel"),
        name="mla_in",
    )(x, nw.reshape(1, d), w_aug, kv_norm_w.reshape(1, -1), cos, sin)


def _mla_q_kernel(cq_ref, qw_ref, wn_ref, wr_ref, ws_ref, wk_ref, cos_ref, sin_ref, ql_ref, qr_ref):
    cq = _rms(cq_ref[...], qw_ref[...]).astype(BF16)
    qn = jnp.dot(cq, wn_ref[...].astype(BF16), preferred_element_type=F32).astype(BF16)
    for j in range(MLA_HEADS // 2):
        ql = jnp.dot(qn[:, j * LANES:(j + 1) * LANES], wk_ref[j].astype(BF16), preferred_element_type=F32)
        ql_ref[2 * j] = ql[:, :MLA_KV_LORA]
        ql_ref[2 * j + 1] = ql[:, MLA_KV_LORA:]
    qr = jnp.dot(cq, wr_ref[...].astype(BF16), preferred_element_type=F32)
    qs = jnp.dot(cq, ws_ref[...].astype(BF16), preferred_element_type=F32)
    per = LANES // MLA_ROPE
    cos = jnp.concatenate([cos_ref[...]] * (MLA_HEADS // per), axis=-1)
    sin = jnp.concatenate([sin_ref[...]] * (MLA_HEADS // per), axis=-1)
    rot = qr * cos + qs * sin
    for h in range(MLA_HEADS):
        qr_ref[h] = rot[:, h * MLA_ROPE:(h + 1) * MLA_ROPE]


def _mla_q(cq, q_norm_w, w_uq, w_uk, cos, sin, *, tm):
    t = cq.shape[0]
    per = LANES // MLA_ROPE
    w3 = w_uq.reshape(MLA_Q_LORA, MLA_HEADS, MLA_NOPE + MLA_ROPE)
    w_nope = w3[:, :, :MLA_NOPE].reshape(MLA_Q_LORA, MLA_HEADS * MLA_NOPE)
    w_rope = w3[:, :, MLA_NOPE:].reshape(MLA_Q_LORA, MLA_HEADS * MLA_ROPE)
    w_swap = _swap_halves(w3[:, :, MLA_NOPE:]).reshape(MLA_Q_LORA, MLA_HEADS * MLA_ROPE)
    a = jnp.transpose(w_uk, (1, 2, 0))
    z = jnp.zeros_like(a[0::2])
    w_bd = jnp.concatenate([jnp.concatenate([a[0::2], z], axis=2),
                            jnp.concatenate([z, a[1::2]], axis=2)], axis=1)
    cos4 = jnp.tile(cos, (1, per))
    sin4 = jnp.tile(sin, (1, per))
    row = lambda i: (i, 0)
    fix2 = lambda i: (0, 0)
    return pl.pallas_call(
        _mla_q_kernel,
        grid=(t // tm,),
        in_specs=[pl.BlockSpec((tm, MLA_Q_LORA), row), pl.BlockSpec((1, MLA_Q_LORA), fix2),
                  pl.BlockSpec(w_nope.shape, fix2), pl.BlockSpec(w_rope.shape, fix2),
                  pl.BlockSpec(w_swap.shape, fix2), pl.BlockSpec(w_bd.shape, lambda i: (0, 0, 0)),
                  pl.BlockSpec((tm, LANES), row), pl.BlockSpec((tm, LANES), row)],
        out_specs=[pl.BlockSpec((MLA_HEADS, tm, MLA_KV_LORA), lambda i: (0, i, 0)),
                   pl.BlockSpec((MLA_HEADS, tm, MLA_ROPE), lambda i: (0, i, 0))],
        out_shape=[jax.ShapeDtypeStruct((MLA_HEADS, t, MLA_KV_LORA), F32),
                   jax.ShapeDtypeStruct((MLA_HEADS, t, MLA_ROPE), F32)],
        compiler_params=_cparams("parallel"),
        name="mla_q",
    )(cq, q_norm_w.reshape(1, -1), w_nope, w_rope, w_swap, w_bd, cos4, sin4)


MLA_QSCALE = (MLA_NOPE + MLA_ROPE) ** -0.5 * math.log2(math.e)


def _lane_repeat(x, width):
    return jnp.concatenate([x] * (width // LANES), axis=1)


def _flash_chunk(s, cb, m_ref, l_ref, acc_ref, rs):
    m_prev = m_ref[rs]
    m_new = jnp.maximum(m_prev, jnp.max(s, axis=-1, keepdims=True))
    alpha = jnp.exp2(m_prev - m_new)
    p = jnp.exp2(s - _lane_repeat(m_new, s.shape[1]))
    l_ref[rs] = alpha * l_ref[rs] + jnp.sum(p, axis=-1, keepdims=True)
    acc_ref[rs] = (_lane_repeat(alpha, acc_ref.shape[1]) * acc_ref[rs]
                   + jnp.dot(p.astype(BF16), cb, preferred_element_type=F32))
    m_ref[rs] = m_new


def _flash_init(ql_ref, qr_ref, qlb_ref, qrb_ref, m_ref, l_ref, acc_ref):
    rows = qlb_ref.shape[0]
    qlb_ref[...] = (ql_ref[...].reshape(rows, MLA_KV_LORA) * MLA_QSCALE).astype(BF16)
    qrb_ref[...] = (qr_ref[...].reshape(rows, MLA_ROPE) * MLA_QSCALE).astype(BF16)
    m_ref[...] = jnp.full(m_ref.shape, NEG_INF, F32)
    l_ref[...] = jnp.zeros(l_ref.shape, F32)
    acc_ref[...] = jnp.zeros(acc_ref.shape, F32)


def _mla_finish(acc_ref, l_ref, wv_ref, o_ref, rows_per_head):
    inv = _lane_repeat(1.0 / l_ref[...], acc_ref.shape[1])
    outs = []
    for j in range(MLA_HEADS // 2):
        pair = None
        for h in (2 * j, 2 * j + 1):
            rs = slice(h * rows_per_head, (h + 1) * rows_per_head)
            part = _bdot(acc_ref[rs] * inv[rs], wv_ref[h])
            pair = part if pair is None else pair + part
        outs.append(pair)
    o_ref[...] = jnp.concatenate(outs, axis=-1)


def _mla_prompt_kernel(wq_ref, wk_ref, ql_ref, qr_ref, c_ref, r_ref, wv_ref, o_ref, qlb_ref, qrb_ref, m_ref,
                       l_ref, acc_ref, *, tq, tk, rc):
    qi = wq_ref[pl.program_id(1)]
    kj = wk_ref[pl.program_id(1)]
    last = (qi * tq + tq - 1) // tk
    rows = MLA_HEADS * tq

    @pl.when(kj == 0)
    def _():
        _flash_init(ql_ref, qr_ref, qlb_ref, qrb_ref, m_ref, l_ref, acc_ref)

    def step(masked):
        cb = c_ref[...].astype(BF16)
        rb = r_ref[...].astype(BF16)
        chunks = [slice(ch * rc, (ch + 1) * rc) for ch in range(rows // rc)]
        scores = [_bdot_nt(qlb_ref[rs], cb) + _bdot_nt(qrb_ref[rs], rb) for rs in chunks]
        if masked:
            q_pos = qi * tq + lax.broadcasted_iota(jnp.int32, (rc, tk), 0) % tq
            k_pos = kj * tk + lax.broadcasted_iota(jnp.int32, (rc, tk), 1)
            scores = [jnp.where(k_pos <= q_pos, s, NEG_INF) for s in scores]
        probs, alphas = [], []
        for rs, s in zip(chunks, scores):
            m_prev = m_ref[rs]
            m_new = jnp.maximum(m_prev, jnp.max(s, axis=-1, keepdims=True))
            alpha = jnp.exp2(m_prev - m_new)
            p = jnp.exp2(s - _lane_repeat(m_new, tk))
            l_ref[rs] = alpha * l_ref[rs] + jnp.sum(p, axis=-1, keepdims=True)
            m_ref[rs] = m_new
            probs.append(p.astype(BF16))
            alphas.append(alpha)
        for rs, p, alpha in zip(chunks, probs, alphas):
            acc_ref[rs] = (_lane_repeat(alpha, MLA_KV_LORA) * acc_ref[rs]
                           + jnp.dot(p, cb, preferred_element_type=F32))

    @pl.when(kj < last)
    def _():
        step(False)

    @pl.when(kj == last)
    def _():
        step(True)
        _mla_finish(acc_ref, l_ref, wv_ref, o_ref, tq)


def _pad_uv(w_uv):
    a = jnp.transpose(w_uv, (1, 0, 2))
    z = jnp.zeros_like(a)
    even = (jnp.arange(a.shape[0]) % 2 == 0)[:, None, None]
    return jnp.concatenate([jnp.where(even, a, z), jnp.where(even, z, a)], axis=2)


def _mla_prompt_attn(q_lat, q_rope, c_kv, k_r, w_uv_pad, *, n_seq, seq_len, tq, tk):
    t = c_kv.shape[0]
    nq, nk = seq_len // tq, seq_len // tk
    assert tk % tq == 0
    pairs = [(i, j) for i in range(nq) for j in range((i * tq + tq - 1) // tk + 1)]
    work_q = jnp.asarray([p[0] for p in pairs], jnp.int32)
    work_k = jnp.asarray([p[1] for p in pairs], jnp.int32)

    def qmap(b, w, wq, wk):
        return (0, b * nq + wq[w], 0)

    def kmap(b, w, wq, wk):
        return (b * nk + wk[w], 0)

    rows = MLA_HEADS * tq
    return pl.pallas_call(
        functools.partial(_mla_prompt_kernel, tq=tq, tk=tk, rc=4 * tq),
        grid_spec=pltpu.PrefetchScalarGridSpec(
            num_scalar_prefetch=2,
            grid=(n_seq, len(pairs)),
            in_specs=[pl.BlockSpec((MLA_HEADS, tq, MLA_KV_LORA), qmap),
                      pl.BlockSpec((MLA_HEADS, tq, MLA_ROPE), qmap),
                      pl.BlockSpec((tk, MLA_KV_LORA), kmap), pl.BlockSpec((tk, MLA_ROPE), kmap),
                      pl.BlockSpec(w_uv_pad.shape, lambda b, w, wq, wk: (0, 0, 0))],
            out_specs=pl.BlockSpec((tq, MLA_HEADS * MLA_V), lambda b, w, wq, wk: (b * nq + wq[w], 0)),
            scratch_shapes=_flash_scratch(rows)),
        out_shape=jax.ShapeDtypeStruct((t, MLA_HEADS * MLA_V), F32),
        compiler_params=_cparams("parallel", "arbitrary"),
        name="mla_prompt_attn",
    )(work_q, work_k, q_lat, q_rope, c_kv, k_r, w_uv_pad)


def _flash_scratch(rows):
    return [pltpu.VMEM((rows, MLA_KV_LORA), BF16), pltpu.VMEM((rows, MLA_ROPE), BF16),
            pltpu.VMEM((rows, LANES), F32), pltpu.VMEM((rows, LANES), F32),
            pltpu.VMEM((rows, MLA_KV_LORA), F32)]


def _mla_sample_kernel(*refs, n_pg, t_s):
    pt_ref = refs[0]
    ql_ref, qr_ref = refs[1], refs[2]
    lat_refs = refs[3:3 + n_pg]
    kr_refs = refs[3 + n_pg:3 + 2 * n_pg]
    cn_ref, rn_ref, wv_ref = refs[3 + 2 * n_pg:6 + 2 * n_pg]
    o_ref, qlb_ref, qrb_ref, m_ref, l_ref, acc_ref = refs[7 + 2 * n_pg:]
    del pt_ref
    kj = pl.program_id(1)
    n_steps = pl.num_programs(1)
    rows = MLA_HEADS * t_s
    everything = slice(0, rows)

    @pl.when(kj == 0)
    def _():
        _flash_init(ql_ref, qr_ref, qlb_ref, qrb_ref, m_ref, l_ref, acc_ref)

    @pl.when(kj < n_steps - 1)
    def _():
        cb = jnp.concatenate([ref[...].astype(BF16) for ref in lat_refs], axis=0)
        rbt = jnp.concatenate([ref[...].astype(BF16) for ref in kr_refs], axis=1)
        s = _bdot_nt(qlb_ref[...], cb) + jnp.dot(qrb_ref[...], rbt, preferred_element_type=F32)
        _flash_chunk(s, cb, m_ref, l_ref, acc_ref, everything)

    @pl.when(kj == n_steps - 1)
    def _():
        cb = _pad_rows(cn_ref[...], LANES).astype(BF16)
        rb = _pad_rows(rn_ref[...], LANES).astype(BF16)
        s = _bdot_nt(qlb_ref[...], cb) + _bdot_nt(qrb_ref[...], rb)
        q_t = lax.broadcasted_iota(jnp.int32, (rows, LANES), 0) % t_s
        k_t = lax.broadcasted_iota(jnp.int32, (rows, LANES), 1)
        s = jnp.where(k_t <= q_t, s, NEG_INF)
        _flash_chunk(s, cb, m_ref, l_ref, acc_ref, everything)
        _mla_finish(acc_ref, l_ref, wv_ref, o_ref, t_s)


def _mla_sample_attn(q_lat, q_rope, c_kv, k_r, cache_lat, cache_kr_t, page_table, w_uv_pad, o_full, *,
                     n_seq, t_s, row0, n_pg):
    n_pages = page_table.shape[1]
    page = cache_lat.shape[2]
    assert n_pages % n_pg == 0 and row0 % t_s == 0
    n_steps = n_pages // n_pg + 1
    base = row0 // t_s

    def qmap(b, j, pt):
        return (0, base + b, 0)

    def newmap(b, j, pt):
        return (base + b, 0)

    def pagemap(p):
        return lambda b, j, pt: (0, pt[b, jnp.minimum(j * n_pg + p, n_pages - 1)], 0, 0)

    in_specs = [pl.BlockSpec((MLA_HEADS, t_s, MLA_KV_LORA), qmap),
                pl.BlockSpec((MLA_HEADS, t_s, MLA_ROPE), qmap)]
    in_specs += [pl.BlockSpec((None, None, page, MLA_KV_LORA), pagemap(p)) for p in range(n_pg)]
    in_specs += [pl.BlockSpec((None, None, MLA_ROPE, page), pagemap(p)) for p in range(n_pg)]
    in_specs += [pl.BlockSpec((t_s, MLA_KV_LORA), newmap), pl.BlockSpec((t_s, MLA_ROPE), newmap),
                 pl.BlockSpec(w_uv_pad.shape, lambda b, j, pt: (0, 0, 0)),
                 pl.BlockSpec(memory_space=pl.ANY)]
    n_in = len(in_specs)
    return pl.pallas_call(
        functools.partial(_mla_sample_kernel, n_pg=n_pg, t_s=t_s),
        grid_spec=pltpu.PrefetchScalarGridSpec(
            num_scalar_prefetch=1,
            grid=(n_seq, n_steps),
            in_specs=in_specs,
            out_specs=pl.BlockSpec((t_s, MLA_HEADS * MLA_V), newmap),
            scratch_shapes=_flash_scratch(MLA_HEADS * t_s)),
        out_shape=jax.ShapeDtypeStruct(o_full.shape, F32),
        input_output_aliases={n_in: 0},
        compiler_params=_cparams("parallel", "arbitrary"),
        name="mla_sample_attn",
    )(page_table, q_lat, q_rope, *([cache_lat] * n_pg), *([cache_kr_t] * n_pg), c_kv, k_r, w_uv_pad, o_full)


def _positions(dims, past_len):
    n_p, t_p, n_s, t_s = dims
    return np.concatenate([np.tile(np.arange(t_p), n_p), np.tile(past_len + np.arange(t_s), n_s)])


def _mla_layer(x, nw, cache_lat, cache_kr, page_table, w_in, q_norm_w, w_uq, kv_norm_w, w_uk, w_uv, w_out,
               dims):
    n_p, t_p, n_s, t_s = dims
    t = x.shape[0]
    n_pages = page_table.shape[1]
    past_len = n_pages * cache_lat.shape[2]
    cos, sin = _rope_tables(_positions(dims, past_len), MLA_ROPE // 2)
    cq, ckv, kr = _mla_in(x, nw, w_in, kv_norm_w, cos, sin, tm=_tile(t, 640))
    q_lat, q_rope = _mla_q(cq, q_norm_w, w_uq, w_uk, cos, sin, tm=_tile(t, 256))
    wv = _pad_uv(w_uv)
    o = _mla_prompt_attn(q_lat, q_rope, ckv, kr, wv, n_seq=n_p, seq_len=t_p, tq=min(t_p, 128),
                         tk=min(t_p, 256))
    o = _mla_sample_attn(q_lat, q_rope, ckv, kr, cache_lat, jnp.swapaxes(cache_kr, 2, 3), page_table, wv, o,
                         n_seq=n_s, t_s=t_s, row0=n_p * t_p, n_pg=math.gcd(n_pages, 32))
    x = _linear(o, w_out, res=x, tm=_tile(t, 1280), tn=512, name="mla_out")
    return x, ckv, kr


def _lane_halves(a):
    half = LANES // 2
    low = lax.broadcasted_iota(jnp.int32, a.shape, 1) < half
    rolled = pltpu.roll(a, half, axis=1)
    head0 = (jnp.where(low, a, 0.0), jnp.where(low, 0.0, rolled))
    head1 = (jnp.where(low, rolled, 0.0), jnp.where(low, 0.0, a))
    return head0, head1


def _swa_heads(q, k_all, v_all, sink_ref, mask, o_ref):
    rq = q.shape[0]
    scale = SWA_HD ** -0.5
    top = lax.broadcasted_iota(jnp.int32, (2 * rq, 1), 0) < rq
    for cg in range(SWA_KV_HEADS // 2):
        k_heads = _lane_halves(k_all[:, cg * LANES:(cg + 1) * LANES])
        v_heads = _lane_halves(v_all[:, cg * LANES:(cg + 1) * LANES])
        for sub in range(2):
            kh = 2 * cg + sub
            (k_lo, k_hi), (v_lo, v_hi) = k_heads[sub], v_heads[sub]
            qs = jnp.concatenate([q[:, (2 * kh) * LANES:(2 * kh + 1) * LANES],
                                  q[:, (2 * kh + 1) * LANES:(2 * kh + 2) * LANES]], axis=0)
            acc = None
            for which, (kk, vv) in enumerate(((k_lo, v_lo), (k_hi, v_hi))):
                s = jnp.where(mask, _bdot_nt(qs, kk) * scale, NEG_INF)
                sink = jnp.where(top, sink_ref[4 * kh + which], sink_ref[4 * kh + 2 + which])
                m = jnp.maximum(jnp.max(s, axis=-1, keepdims=True), sink)
                e = jnp.exp(s - m)
                p = e / (jnp.sum(e, axis=-1, keepdims=True) + jnp.exp(sink - m))
                part = _bdot(p, vv)
                acc = part if acc is None else acc + part
            o_ref[:, (2 * kh) * LANES:(2 * kh + 1) * LANES] = acc[:rq]
            o_ref[:, (2 * kh + 1) * LANES:(2 * kh + 2) * LANES] = acc[rq:]


def _swa_prompt_kernel(sink_ref, q_ref, kp_ref, kc_ref, vp_ref, vc_ref, o_ref):
    n = pl.program_id(1)
    w = q_ref.shape[0]
    k_all = jnp.concatenate([kp_ref[...], kc_ref[...]], axis=0)
    v_all = jnp.concatenate([vp_ref[...], vc_ref[...]], axis=0)
    r = lax.broadcasted_iota(jnp.int32, (2 * w, 2 * w), 0) % w
    c = lax.broadcasted_iota(jnp.int32, (2 * w, 2 * w), 1)
    mask = (c >= r) & (c <= r + w) & ((n > 0) | (c >= w))
    _swa_heads(q_ref[...], k_all, v_all, sink_ref, mask, o_ref)


def _swa_prompt_attn(qkv, sinks, *, n_seq, seq_len):
    t = qkv.shape[0]
    w = WINDOW
    nb = seq_len // w
    dq = SWA_HEADS * SWA_HD
    dkv = SWA_KV_HEADS * SWA_HD
    kcol = dq // dkv
    cur = lambda col: (lambda b, n: (b * nb + n, col))
    prev = lambda col: (lambda b, n: (b * nb + jnp.maximum(n - 1, 0), col))
    return pl.pallas_call(
        _swa_prompt_kernel,
        grid=(n_seq, nb),
        in_specs=[pl.BlockSpec(memory_space=pltpu.SMEM),
                  pl.BlockSpec((w, dq), cur(0)),
                  pl.BlockSpec((w, dkv), prev(kcol)), pl.BlockSpec((w, dkv), cur(kcol)),
                  pl.BlockSpec((w, dkv), prev(kcol + 1)), pl.BlockSpec((w, dkv), cur(kcol + 1))],
        out_specs=pl.BlockSpec((w, dq), cur(0)),
        out_shape=jax.ShapeDtypeStruct((t, dq), F32),
        compiler_params=_cparams("parallel", "parallel"),
        name="swa_prompt_attn",
    )(sinks, qkv, qkv, qkv, qkv, qkv)


def _swa_sample_kernel(sink_ref, q_ref, kn_ref, vn_ref, kc_ref, vc_ref, alias_ref, o_ref, ko_ref, vo_ref):
    del alias_ref
    t_s = q_ref.shape[0]
    w = kc_ref.shape[0]
    k_all = jnp.concatenate([kc_ref[...], kn_ref[...]], axis=0)
    v_all = jnp.concatenate([vc_ref[...], vn_ref[...]], axis=0)
    r = lax.broadcasted_iota(jnp.int32, (2 * t_s, w + t_s), 0) % t_s
    c = lax.broadcasted_iota(jnp.int32, (2 * t_s, w + t_s), 1)
    mask = (c <= w + r) & (c >= r)
    _swa_heads(q_ref[...], k_all, v_all, sink_ref, mask, o_ref)
    ko_ref[...] = k_all[t_s:]
    vo_ref[...] = v_all[t_s:]


def _swa_sample_attn(qkv, cache_k, cache_v, sinks, o_full, *, n_seq, t_s, row0):
    w = cache_k.shape[1]
    dq = SWA_HEADS * SWA_HD
    dkv = SWA_KV_HEADS * SWA_HD
    kcol = dq // dkv
    base = row0 // t_s
    new = lambda col: (lambda b: (base + b, col))
    seq = lambda b: (b, 0, 0)
    return pl.pallas_call(
        _swa_sample_kernel,
        grid=(n_seq,),
        in_specs=[pl.BlockSpec(memory_space=pltpu.SMEM),
                  pl.BlockSpec((t_s, dq), new(0)),
                  pl.BlockSpec((t_s, dkv), new(kcol)), pl.BlockSpec((t_s, dkv), new(kcol + 1)),
                  pl.BlockSpec((None, w, dkv), seq), pl.BlockSpec((None, w, dkv), seq),
                  pl.BlockSpec(memory_space=pl.ANY)],
        out_specs=[pl.BlockSpec((t_s, dq), new(0)),
                   pl.BlockSpec((None, w, dkv), seq), pl.BlockSpec((None, w, dkv), seq)],
        out_shape=[jax.ShapeDtypeStruct(o_full.shape, F32),
                   jax.ShapeDtypeStruct(cache_k.shape, F32), jax.ShapeDtypeStruct(cache_v.shape, F32)],
        input_output_aliases={6: 0},
        compiler_params=_cparams("parallel"),
        name="swa_sample_attn",
    )(sinks, qkv, qkv, qkv, cache_k, cache_v, o_full)


def _swa_layer(x, nw, cache_k, cache_v, w_qkv, b_qkv, sinks, w_out, b_out, dims):
    n_p, t_p, n_s, t_s = dims
    t = x.shape[0]
    tm = _tile(t, 1280)
    dq = SWA_HEADS * SWA_HD
    dkv = SWA_KV_HEADS * SWA_HD
    qkv = _linear(x, w_qkv, norm_w=nw, bias=b_qkv, tm=tm, tn=512, name="swa_in")
    o = _swa_prompt_attn(qkv, sinks, n_seq=n_p, seq_len=t_p)
    o, k_s, v_s = _swa_sample_attn(qkv, cache_k.reshape(n_s, WINDOW, dkv), cache_v.reshape(n_s, WINDOW, dkv),
                                   sinks, o, n_seq=n_s, t_s=t_s, row0=n_p * t_p)
    x = _linear(o, w_out, bias=b_out, res=x, tm=tm, tn=512, name="swa_out")
    kv_p = jnp.stack([lax.slice(qkv, ((b + 1) * t_p - WINDOW, dq), ((b + 1) * t_p, dq + 2 * dkv))
                      for b in range(n_p)])
    kv_shape = (n_p, WINDOW, SWA_KV_HEADS, SWA_HD)
    k_p = kv_p[:, :, :dkv].reshape(kv_shape)
    v_p = kv_p[:, :, dkv:].reshape(kv_shape)
    return x, k_p, v_p, k_s.reshape(cache_k.shape), v_s.reshape(cache_v.shape)


def _l2norm(x):
    return x * lax.rsqrt(jnp.sum(x * x, axis=-1, keepdims=True) + 1e-6)


def _split2(x):
    h1 = x.astype(BF16)
    return h1, (x - h1.astype(F32)).astype(BF16)


def _bmm3(a, b):
    a1, a2 = _split2(a)
    b1, b2 = _split2(b)
    dot = lambda x, y: jnp.einsum("bij,bjk->bik", x, y, preferred_element_type=F32)
    return dot(a1, b1) + dot(a1, b2) + dot(a2, b1)


def _unit_lower_inverse(low):
    n = low.shape[-1]
    eye = (lax.broadcasted_iota(jnp.int32, (n, n), 0) == lax.broadcasted_iota(jnp.int32, (n, n), 1))
    eye = eye.astype(F32)[None]
    power = -low
    inv = eye + power
    steps = int(math.log2(n)) - 1
    for _ in range(steps):
        power = _bmm3(power, power)
        inv = inv + _bmm3(inv, power)
    return inv


def _dn_kernel(*refs, rb, zero_init, has_alias):
    it = iter(refs)
    x_ref, z_ref, ab_ref = next(it), next(it), next(it)
    s0_ref, c0_ref = (None, None) if zero_init else (next(it), next(it))
    cw_ref, al_ref, dt_ref, nw_ref = next(it), next(it), next(it), next(it)
    if has_alias:
        next(it)
    o_ref, s_ref, co_ref = next(it), next(it), next(it)
    st_ref, xp_ref = next(it), next(it)
    r = pl.program_id(1)
    rows = max(rb, CHUNK)
    n_chunks = rows // CHUNK
    halo = 8
    dqk = DN_HEADS * DN_HK

    @pl.when(r == 0)
    def _():
        if zero_init:
            st_ref[...] = jnp.zeros(st_ref.shape, F32)
            xp_ref[0:halo] = jnp.zeros((halo, xp_ref.shape[1]), F32)
        else:
            st_ref[...] = s0_ref[0]
            xp_ref[0:halo] = c0_ref[0]

    xp_ref[halo:halo + rb] = x_ref[...]
    full = xp_ref[...]
    conv = full[halo:] * cw_ref[DN_CONV - 1:DN_CONV]
    for w in range(DN_CONV - 1):
        conv = conv + pltpu.roll(full, DN_CONV - 1 - w, axis=0)[halo:] * cw_ref[w:w + 1]
    tail = xp_ref[rb:rb + halo]
    co_ref[0] = tail
    xp_ref[0:halo] = tail
    qkv = _pad_rows(_silu(conv), rows)
    ab = ab_ref[...]
    g_all = _pad_rows(-jnp.exp(al_ref[...]) * _softplus(ab + dt_ref[...]), rows)
    beta_all = _pad_rows(1.0 / (1.0 + jnp.exp(-ab)), rows)

    ri = lax.broadcasted_iota(jnp.int32, (rows, rows), 0)
    ci = lax.broadcasted_iota(jnp.int32, (rows, rows), 1)
    same = (ri // CHUNK) == (ci // CHUNK)
    causal = same & (ci <= ri)
    strict = same & (ci < ri)
    upper = (same & (ri <= ci)).astype(BF16)
    gc_col = _dot_exact_lhs(causal, g_all)
    gt_col = _dot_exact_lhs(same, g_all)
    g1, g2, g3 = _split3(g_all)
    tn = lambda a: lax.dot_general(a, upper, (((0,), (0,)), ((), ())), preferred_element_type=F32)
    gc_row = tn(g1) + tn(g2) + tn(g3)

    lows, rhss, attns, qds, kos, gls = [], [], [], [], [], []
    for h in range(DN_HEADS):
        hs = slice(h * DN_HK, (h + 1) * DN_HK)
        q = _l2norm(qkv[:, hs]) * DN_HK ** -0.5
        k = _l2norm(qkv[:, dqk + h * DN_HK:dqk + (h + 1) * DN_HK])
        v = qkv[:, 2 * dqk + h * DN_HV:2 * dqk + (h + 1) * DN_HV]
        beta = beta_all[:, DN_HEADS + h:DN_HEADS + h + 1]
        gc = gc_col[:, h:h + 1]
        gt = gt_col[:, h:h + 1]
        decay = jnp.where(causal, jnp.exp(jnp.where(causal, gc - gc_row[h:h + 1, :], 0.0)), 0.0)
        kb = k * beta
        low = jnp.where(strict, _bdot_nt(kb, k) * decay, 0.0)
        attn = _bdot_nt(q, k) * decay
        rhs = jnp.concatenate([v * beta, kb * jnp.exp(gc)], axis=-1)
        q_dec = q * jnp.exp(gc)
        k_out = k * jnp.exp(gt - gc)
        g_last = jnp.exp(gt)
        for c in range(n_chunks):
            rs = slice(c * CHUNK, (c + 1) * CHUNK)
            lows.append(low[rs, rs])
            attns.append(attn[rs, rs])
            rhss.append(rhs[rs])
            qds.append(q_dec[rs])
            kos.append(k_out[rs])
            gls.append(g_last[c * CHUNK:c * CHUNK + 1])
    sol = _bmm3(_unit_lower_inverse(jnp.stack(lows)), jnp.stack(rhss))
    n_out = min(rb, CHUNK)
    for h in range(DN_HEADS):
        s = st_ref[h]
        for c in range(n_chunks):
            i = h * n_chunks + c
            u, wm = sol[i, :, :DN_HV], sol[i, :, DN_HV:]
            v_new = u - _bdot(wm, s)
            o = _bdot(qds[i], s) + _bdot(attns[i], v_new)
            s = s * gls[i] + _bdot_tn(kos[i], v_new)
            zs = z_ref[c * CHUNK:c * CHUNK + n_out, h * DN_HV:(h + 1) * DN_HV]
            o_ref[c * CHUNK:c * CHUNK + n_out, h * DN_HV:(h + 1) * DN_HV] = (
                _rms(o[:n_out], nw_ref[...]) * _silu(zs))
        st_ref[h] = s

    @pl.when(r == pl.num_programs(1) - 1)
    def _():
        s_ref[0] = st_ref[...]


def _dn_core(proj, s0, conv0, conv_w, a_log, dt_bias, norm_w, o_full, *, n_seq, seq_len, row0, rb):
    t = proj.shape[0]
    dconv = conv_w.shape[1]
    dz = DN_HEADS * DN_HV
    nblk = seq_len // rb
    base = row0 // rb
    assert row0 % rb == 0 and seq_len % rb == 0 and rb % 8 == 0

    def rowmap(col):
        return lambda b, r: (base + b * nblk + r, col)

    seq4 = lambda b, r: (b, 0, 0, 0)
    seq3 = lambda b, r: (b, 0, 0)
    fix = lambda b, r: (0, 0)
    in_specs = [pl.BlockSpec((rb, dconv), rowmap(0)), pl.BlockSpec((rb, dz), rowmap(dconv // dz)),
                pl.BlockSpec((rb, LANES), rowmap((dconv + dz) // LANES))]
    args = [proj, proj, proj]
    if s0 is not None:
        in_specs += [pl.BlockSpec((1, DN_HEADS, DN_HK, DN_HV), seq4), pl.BlockSpec((1, 8, dconv), seq3)]
        args += [s0, conv0]
    in_specs += [pl.BlockSpec((DN_CONV, dconv), fix), pl.BlockSpec((1, LANES), fix),
                 pl.BlockSpec((1, LANES), fix), pl.BlockSpec((1, DN_HV), fix)]
    args += [conv_w, _pad_cols(a_log.reshape(1, -1), LANES), _pad_cols(dt_bias.reshape(1, -1), LANES),
             norm_w.reshape(1, -1)]
    aliases = {}
    if o_full is not None:
        in_specs.append(pl.BlockSpec(memory_space=pl.ANY))
        aliases = {len(args): 0}
        args.append(o_full)
    kern = functools.partial(_dn_kernel, rb=rb, zero_init=s0 is None, has_alias=o_full is not None)
    return pl.pallas_call(
        kern,
        grid=(n_seq, nblk),
        in_specs=in_specs,
        out_specs=[pl.BlockSpec((rb, dz), rowmap(0)),
                   pl.BlockSpec((1, DN_HEADS, DN_HK, DN_HV), seq4),
                   pl.BlockSpec((1, 8, dconv), seq3)],
        out_shape=[jax.ShapeDtypeStruct((t, dz), F32),
                   jax.ShapeDtypeStruct((n_seq, DN_HEADS, DN_HK, DN_HV), F32),
                   jax.ShapeDtypeStruct((n_seq, 8, dconv), F32)],
        scratch_shapes=[pltpu.VMEM((DN_HEADS, DN_HK, DN_HV), F32), pltpu.VMEM((rb + 8, dconv), F32)],
        input_output_aliases=aliases,
        compiler_params=_cparams("parallel", "arbitrary"),
        name="dn_core",
    )(*args)


def _dn_layer(x, nw, s0_s, conv0_s, w_in, conv_w, a_log, dt_bias, norm_w, w_out, dims):
    n_p, t_p, n_s, t_s = dims
    t = x.shape[0]
    tm = _tile(t, 1280)
    dconv = conv_w.shape[1]
    dz = DN_HEADS * DN_HV
    width = dconv + dz + LANES
    proj = _linear(x, _pad_cols(w_in, width), norm_w=nw, tm=tm, tn=_tile(width, 1408), name="dn_in")
    o, sp, cp = _dn_core(proj, None, None, conv_w, a_log, dt_bias, norm_w, None, n_seq=n_p, seq_len=t_p,
                         row0=0, rb=min(t_p, 128))
    conv0 = jnp.pad(conv0_s, ((0, 0), (8 - conv0_s.shape[1], 0), (0, 0)))
    o, ss, cs = _dn_core(proj, s0_s, conv0, conv_w, a_log, dt_bias, norm_w, o, n_seq=n_s, seq_len=t_s,
                         row0=n_p * t_p, rb=t_s)
    x = _linear(o, w_out, res=x, tm=tm, tn=512, name="dn_out")
    keep = DN_CONV - 1
    return x, sp, ss, cp[:, 8 - keep:], cs[:, 8 - keep:]


def kernel(x_prompt, x_sample, state_gla, cache_mla_latent, cache_mla_krope, cache_swa_k, cache_swa_v, state_delta, state_delta_conv, page_table, norm_w, final_norm_w, gla_w_in, gla_w_gk2, gla_b_gk2, gla_norm_w, gla_w_out, mla_w_in, mla_q_norm_w, mla_w_uq, mla_kv_norm_w, mla_w_uk, mla_w_uv, mla_w_out, swa_w_qkv, swa_b_qkv, swa_sinks, swa_w_out, swa_b_out, dn_w_in, dn_conv_w, dn_a_log, dn_dt_bias, dn_norm_w, dn_w_out, ffn_w_gate, ffn_w_up, ffn_w_down, moe_w_router, moe_w_gate, moe_w_up, moe_w_down):
    n_p, t_p, d = x_prompt.shape
    n_s, t_s, _ = x_sample.shape
    dims = (n_p, t_p, n_s, t_s)
    x = jnp.concatenate([x_prompt.reshape(n_p * t_p, d), x_sample.reshape(n_s * t_s, d)], axis=0)
    t = x.shape[0]
    tm = _tile(t, 1280)
    n_tp = n_p * t_p

    x, gla_p, gla_s = _gla_layer(x, norm_w[0, 0], state_gla[0], gla_w_in[0], gla_w_gk2[0], gla_b_gk2[0],
                                 gla_norm_w[0], gla_w_out[0], dims)
    x = _ffn(x, norm_w[0, 1], ffn_w_gate, ffn_w_up, ffn_w_down, 0, tm=tm, tf=512)
    moe_tiles = dict(tm=_tile(t, 640), tp=1280, tf=512)

    x, ckv, kr = _mla_layer(x, norm_w[1, 0], cache_mla_latent[0:1], cache_mla_krope[0:1], page_table,
                            mla_w_in[0], mla_q_norm_w[0], mla_w_uq[0], mla_kv_norm_w[0], mla_w_uk[0],
                            mla_w_uv[0], mla_w_out[0], dims)
    x = _moe(x, norm_w[1, 1], moe_w_router[0], moe_w_gate, moe_w_up, moe_w_down, 0, **moe_tiles)

    x, swk_p, swv_p, swk_s, swv_s = _swa_layer(x, norm_w[2, 0], cache_swa_k[0], cache_swa_v[0],
                                               swa_w_qkv[0], swa_b_qkv[0], swa_sinks[0], swa_w_out[0],
                                               swa_b_out[0], dims)
    x = _ffn(x, norm_w[2, 1], ffn_w_gate, ffn_w_up, ffn_w_down, 1, tm=tm, tf=512)

    x, dn_p, dn_s, cv_p, cv_s = _dn_layer(x, norm_w[3, 0], state_delta[0], state_delta_conv[0], dn_w_in[0],
                                          dn_conv_w[0], dn_a_log[0], dn_dt_bias[0], dn_norm_w[0],
                                          dn_w_out[0], dims)
    y = _moe(x, norm_w[3, 1], moe_w_router[1], moe_w_gate, moe_w_up, moe_w_down, 1, final_norm_w,
             **moe_tiles)

    lead = lambda a: a[None]
    return (y[:n_tp].reshape(n_p, t_p, d), y[n_tp:].reshape(n_s, t_s, d),
            lead(gla_p), lead(gla_s),
            lead(ckv[:n_tp].reshape(n_p, t_p, -1)), lead(ckv[n_tp:].reshape(n_s, t_s, -1)),
            lead(kr[:n_tp].reshape(n_p, t_p, -1)), lead(kr[n_tp:].reshape(n_s, t_s, -1)),
            lead(swk_p), lead(swk_s), lead(swv_p), lead(swv_s),
            lead(dn_p), lead(dn_s), lead(cv_p), lead(cv_s))
```

```python
import functools
import math

import jax
import jax.numpy as jnp
import numpy as np
from jax import lax
from jax.experimental import pallas as pl
from jax.experimental.pallas import tpu as pltpu

F32 = jnp.float32
BF16 = jnp.bfloat16

NORM_EPS = 1e-6
GLA_HEADS = 4
GLA_GATE_RANK = 16
GLA_GATE_NORMALIZER = 16.0
CHUNK = 64
MLA_HEADS = 16
MLA_Q_LORA = 384
MLA_KV_LORA = 256
MLA_NOPE = 64
MLA_ROPE = 32
MLA_V = 64
ROPE_THETA = 10000.0
SWA_HEADS = 16
SWA_KV_HEADS = 4
SWA_HD = 64
WINDOW = 128
DN_HEADS = 8
DN_HK = 128
DN_HV = 128
DN_CONV = 4
N_EXPERTS = 8

LANES = 128
VMEM_LIMIT = 56 * 1024 * 1024
NEG_INF = float("-inf")


def _cparams(*sem):
    return pltpu.CompilerParams(dimension_semantics=sem, vmem_limit_bytes=VMEM_LIMIT)


def _bdot(a, b):
    return jnp.dot(a.astype(BF16), b.astype(BF16), preferred_element_type=F32)


def _bdot_nt(a, b):
    return lax.dot_general(a.astype(BF16), b.astype(BF16), (((1,), (1,)), ((), ())),
                           preferred_element_type=F32)


def _bdot_tn(a, b):
    return lax.dot_general(a.astype(BF16), b.astype(BF16), (((0,), (0,)), ((), ())),
                           preferred_element_type=F32)


def _split3(x):
    h1 = x.astype(BF16)
    r1 = x - h1.astype(F32)
    h2 = r1.astype(BF16)
    h3 = (r1 - h2.astype(F32)).astype(BF16)
    return h1, h2, h3


def _dot_exact_lhs(m, x):
    mb = m.astype(BF16)
    h1, h2, h3 = _split3(x)
    return (jnp.dot(mb, h1, preferred_element_type=F32) + jnp.dot(mb, h2, preferred_element_type=F32)
            + jnp.dot(mb, h3, preferred_element_type=F32))


def _rms(x, w):
    return x * lax.rsqrt(jnp.mean(x * x, axis=-1, keepdims=True) + NORM_EPS) * w


def _silu(x):
    return x / (1.0 + jnp.exp(-x))


def _log_sigmoid(x):
    return jnp.minimum(x, 0.0) - jnp.log(1.0 + jnp.exp(-jnp.abs(x)))


def _softplus(x):
    return jnp.maximum(x, 0.0) + jnp.log(1.0 + jnp.exp(-jnp.abs(x)))


def _linear_kernel(*refs, has_norm, has_bias, has_res):
    it = iter(refs)
    x_ref = next(it)
    nw_ref = next(it) if has_norm else None
    w_ref = next(it)
    b_ref = next(it) if has_bias else None
    r_ref = next(it) if has_res else None
    o_ref = next(it)
    h_ref = next(it)

    @pl.when(pl.program_id(1) == 0)
    def _():
        xv = x_ref[...].astype(F32)
        if has_norm:
            xv = _rms(xv, nw_ref[...])
        h_ref[...] = xv.astype(BF16)

    acc = jnp.dot(h_ref[...], w_ref[...].astype(BF16), preferred_element_type=F32)
    if has_bias:
        acc = acc + b_ref[...]
    if has_res:
        acc = acc + r_ref[...]
    o_ref[...] = acc.astype(o_ref.dtype)


def _linear(x, w, *, norm_w=None, bias=None, res=None, tm, tn, out_dtype=F32, name="linear"):
    t, k = x.shape
    n = w.shape[1]
    assert t % tm == 0 and n % tn == 0, (t, tm, n, tn)
    in_specs = [pl.BlockSpec((tm, k), lambda i, j: (i, 0))]
    args = [x]
    if norm_w is not None:
        in_specs.append(pl.BlockSpec((1, k), lambda i, j: (0, 0)))
        args.append(norm_w.reshape(1, k))
    in_specs.append(pl.BlockSpec((k, tn), lambda i, j: (0, j)))
    args.append(w)
    if bias is not None:
        in_specs.append(pl.BlockSpec((1, tn), lambda i, j: (0, j)))
        args.append(bias.reshape(1, n))
    if res is not None:
        in_specs.append(pl.BlockSpec((tm, tn), lambda i, j: (i, j)))
        args.append(res)
    kern = functools.partial(_linear_kernel, has_norm=norm_w is not None, has_bias=bias is not None,
                             has_res=res is not None)
    return pl.pallas_call(
        kern,
        grid=(t // tm, n // tn),
        in_specs=in_specs,
        out_specs=pl.BlockSpec((tm, tn), lambda i, j: (i, j)),
        out_shape=jax.ShapeDtypeStruct((t, n), out_dtype),
        scratch_shapes=[pltpu.VMEM((tm, k), BF16)],
        compiler_params=_cparams("parallel", "arbitrary"),
        name=name,
    )(*args)


def _swiglu_acc(h_ref, wg_ref, wu_ref, wd_ref, acc_ref):
    h = h_ref[...].astype(BF16)
    g = jnp.dot(h, wg_ref[...].astype(BF16), preferred_element_type=F32)
    u = jnp.dot(h, wu_ref[...].astype(BF16), preferred_element_type=F32)
    a = (_silu(g) * u).astype(BF16)
    acc_ref[...] += jnp.dot(a, wd_ref[...].astype(BF16), preferred_element_type=F32)


def _ffn_kernel(x_ref, nw_ref, wg_ref, wu_ref, wd_ref, o_ref, h_ref, acc_ref):
    f = pl.program_id(1)

    @pl.when(f == 0)
    def _():
        h_ref[...] = _rms(x_ref[...], nw_ref[...]).astype(BF16)
        acc_ref[...] = jnp.zeros_like(acc_ref)

    _swiglu_acc(h_ref, wg_ref, wu_ref, wd_ref, acc_ref)

    @pl.when(f == pl.num_programs(1) - 1)
    def _():
        o_ref[...] = x_ref[...] + acc_ref[...]


def _ffn(x, norm_w, wg, wu, wd, layer, *, tm, tf):
    t, d = x.shape
    ff = wg.shape[2]
    assert t % tm == 0 and ff % tf == 0
    return pl.pallas_call(
        _ffn_kernel,
        grid=(t // tm, ff // tf),
        in_specs=[pl.BlockSpec((tm, d), lambda i, f: (i, 0)),
                  pl.BlockSpec((1, d), lambda i, f: (0, 0)),
                  pl.BlockSpec((None, d, tf), lambda i, f: (layer, 0, f)),
                  pl.BlockSpec((None, d, tf), lambda i, f: (layer, 0, f)),
                  pl.BlockSpec((None, tf, d), lambda i, f: (layer, f, 0))],
        out_specs=pl.BlockSpec((tm, d), lambda i, f: (i, 0)),
        out_shape=jax.ShapeDtypeStruct((t, d), F32),
        scratch_shapes=[pltpu.VMEM((tm, d), BF16), pltpu.VMEM((tm, d), F32)],
        compiler_params=_cparams("parallel", "arbitrary"),
        name="ffn",
    )(x, norm_w.reshape(1, d), wg, wu, wd)


MOE_CHUNK = 256
MOE_ALIGN = 16
MOE_SLOTS = 4


def _route_kernel(x_ref, nw_ref, wrt_ref, h_ref, sel_ref, gate_ref, rank_ref, cnt_ref):
    hn = _rms(x_ref[...], nw_ref[...])
    h_ref[...] = hn.astype(BF16)
    logits = lax.dot_general(wrt_ref[...], hn, (((1,), (1,)), ((), ())), preferred_element_type=F32,
                             precision=lax.Precision.HIGHEST)
    n_exp, tm = logits.shape
    sub = lax.broadcasted_iota(jnp.int32, logits.shape, 0)
    m1 = jnp.max(logits, axis=0, keepdims=True)
    i1 = jnp.min(jnp.where(logits == m1, sub, n_exp), axis=0, keepdims=True)
    rest = jnp.where(sub == i1, NEG_INF, logits)
    m2 = jnp.max(rest, axis=0, keepdims=True)
    i2 = jnp.min(jnp.where(rest == m2, sub, n_exp), axis=0, keepdims=True)
    e2 = jnp.exp(m2 - m1)
    first, second = sub == i1, sub == i2
    sel = jnp.where(first | second, 1.0, 0.0)
    upper = jnp.where(lax.broadcasted_iota(jnp.int32, (tm, tm), 0)
                      <= lax.broadcasted_iota(jnp.int32, (tm, tm), 1), 1.0, 0.0)
    cum = _bdot(sel, upper)
    sel_ref[...] = sel
    gate_ref[...] = jnp.where(first, 1.0 / (1.0 + e2), 0.0) + jnp.where(second, e2 / (1.0 + e2), 0.0)
    rank_ref[...] = cum - sel
    cnt_ref[0] = jnp.broadcast_to(cum[:, tm - 1:tm], (n_exp, LANES))


def _route(x, norm_w, w_router, *, tm):
    t, d = x.shape
    n_exp = w_router.shape[1]
    et = lambda i: (0, i)
    return pl.pallas_call(
        _route_kernel,
        grid=(t // tm,),
        in_specs=[pl.BlockSpec((tm, d), lambda i: (i, 0)), pl.BlockSpec((1, d), lambda i: (0, 0)),
                  pl.BlockSpec((n_exp, d), lambda i: (0, 0))],
        out_specs=[pl.BlockSpec((tm, d), lambda i: (i, 0)), pl.BlockSpec((n_exp, tm), et),
                   pl.BlockSpec((n_exp, tm), et), pl.BlockSpec((n_exp, tm), et),
                   pl.BlockSpec((1, n_exp, LANES), lambda i: (i, 0, 0))],
        out_shape=[jax.ShapeDtypeStruct((t, d), BF16), jax.ShapeDtypeStruct((n_exp, t), F32),
                   jax.ShapeDtypeStruct((n_exp, t), F32), jax.ShapeDtypeStruct((n_exp, t), F32),
                   jax.ShapeDtypeStruct((t // tm, n_exp, LANES), F32)],
        compiler_params=_cparams("parallel"),
        name="moe_route",
    )(x, norm_w.reshape(1, d), w_router.T)


def _moe_plan(cnt, *, tm, tp, p_rows):
    n_tiles, n_exp = cnt.shape
    seg = (cnt + MOE_ALIGN - 1) // MOE_ALIGN * MOE_ALIGN
    total = jnp.sum(seg, axis=0)
    in_group = jnp.cumsum(seg, axis=0) - seg
    nch = (cnt + MOE_CHUNK - 1) // MOE_CHUNK
    reach = jnp.max(in_group + nch * MOE_CHUNK, axis=0)
    gsize = (jnp.maximum(total, reach) + tp - 1) // tp * tp
    gstart = jnp.cumsum(gsize) - gsize
    seg_start = gstart[None, :] + in_group
    cum_e = jnp.cumsum(nch, axis=1)
    q_max = n_exp + 2 * tm // MOE_CHUNK
    q = jnp.arange(q_max, dtype=jnp.int32)[None, :]
    flat_e = jnp.minimum(jnp.sum(q[:, :, None] >= cum_e[:, None, :], axis=-1), n_exp - 1).astype(jnp.int32)
    flat_c = q - jnp.take_along_axis(cum_e - nch, flat_e, axis=1)
    flat_row = jnp.take_along_axis(seg_start, flat_e, axis=1) + flat_c * MOE_CHUNK
    row_j = jnp.arange(p_rows // tp, dtype=jnp.int32) * tp
    tile_e = jnp.minimum(jnp.sum(row_j[:, None] >= (gstart + gsize)[None, :], axis=-1), n_exp - 1)
    tile_valid = row_j < jnp.take(gstart + total, tile_e)
    i32 = lambda a: a.astype(jnp.int32)
    return (i32(cum_e[:, -1]), i32(flat_e.reshape(-1)), i32(flat_c.reshape(-1)), i32(flat_row.reshape(-1)),
            i32(tile_e), i32(tile_valid))


def _dispatch_kernel(nq_ref, fe_ref, fc_ref, fr_ref, h_ref, sel_ref, rank_ref, xs_in, xs_ref, stage, sem,
                     *, q_max):
    del xs_in
    i = pl.program_id(0)
    n = nq_ref[i]
    h = h_ref[...]

    def chunk_copy(slot, row):
        return pltpu.make_async_copy(stage.at[slot], xs_ref.at[pl.ds(pl.multiple_of(row, MOE_ALIGN), MOE_CHUNK)],
                                     sem.at[slot])

    def body(q, carry):
        slot = q % MOE_SLOTS
        e = fe_ref[i * q_max + q]
        c = fc_ref[i * q_max + q]

        @pl.when(q >= MOE_SLOTS)
        def _():
            chunk_copy(slot, 0).wait()

        pos = jnp.where(sel_ref[pl.ds(e, 1), :] > 0.0, rank_ref[pl.ds(e, 1), :], -1.0)
        want = (c * MOE_CHUNK + lax.broadcasted_iota(jnp.int32, (MOE_CHUNK, 1), 0)).astype(F32)
        pick = jnp.where(pos == want, 1.0, 0.0).astype(BF16)
        stage[slot] = jnp.dot(pick, h, preferred_element_type=F32)
        chunk_copy(slot, fr_ref[i * q_max + q]).start()
        return carry

    lax.fori_loop(0, n, body, 0)
    for s in range(MOE_SLOTS):
        @pl.when(n > s)
        def _():
            chunk_copy(s, 0).wait()


def _dispatch(h, sel, rank, plan, *, tm, p_rows):
    t, d = h.shape
    n_exp = sel.shape[0]
    n_flat, flat_e, flat_c, flat_row = plan[:4]
    q_max = flat_e.shape[0] // (t // tm)
    et = lambda i, *_: (0, i)
    return pl.pallas_call(
        functools.partial(_dispatch_kernel, q_max=q_max),
        grid_spec=pltpu.PrefetchScalarGridSpec(
            num_scalar_prefetch=4,
            grid=(t // tm,),
            in_specs=[pl.BlockSpec((tm, d), lambda i, *_: (i, 0)), pl.BlockSpec((n_exp, tm), et),
                      pl.BlockSpec((n_exp, tm), et), pl.BlockSpec(memory_space=pl.ANY)],
            out_specs=pl.BlockSpec(memory_space=pl.ANY),
            scratch_shapes=[pltpu.VMEM((MOE_SLOTS, MOE_CHUNK, d), F32),
                            pltpu.SemaphoreType.DMA((MOE_SLOTS,))]),
        out_shape=jax.ShapeDtypeStruct((p_rows, d), F32),
        input_output_aliases={7: 0},
        compiler_params=_cparams("arbitrary"),
        name="moe_dispatch",
    )(n_flat, flat_e, flat_c, flat_row, h, sel, rank, jnp.zeros((p_rows, d), F32))


def _gffn_kernel(te_ref, tv_ref, x_ref, wg_ref, wu_ref, wd_ref, o_ref, acc_ref):
    del te_ref
    j = pl.program_id(0)
    f = pl.program_id(1)
    valid = tv_ref[j] > 0

    @pl.when(f == 0)
    def _():
        acc_ref[...] = jnp.zeros_like(acc_ref)

    @pl.when(valid)
    def _():
        _swiglu_acc(x_ref, wg_ref, wu_ref, wd_ref, acc_ref)

    @pl.when(f == pl.num_programs(1) - 1)
    def _():
        o_ref[...] = acc_ref[...].astype(o_ref.dtype)


def _gffn(xs, wg, wu, wd, layer, tile_e, tile_valid, *, tp, tf):
    p_rows, d = xs.shape
    ff = wg.shape[3]

    def wmap(is_down):
        def index(j, f, te, tv):
            fi = jnp.where(tv[j] > 0, f, 0)
            return (layer, te[j], fi, 0) if is_down else (layer, te[j], 0, fi)
        return index

    return pl.pallas_call(
        _gffn_kernel,
        grid_spec=pltpu.PrefetchScalarGridSpec(
            num_scalar_prefetch=2,
            grid=(p_rows // tp, ff // tf),
            in_specs=[pl.BlockSpec((tp, d), lambda j, f, te, tv: (j, 0)),
                      pl.BlockSpec((None, None, d, tf), wmap(False)),
                      pl.BlockSpec((None, None, d, tf), wmap(False)),
                      pl.BlockSpec((None, None, tf, d), wmap(True))],
            out_specs=pl.BlockSpec((tp, d), lambda j, f, te, tv: (j, 0)),
            scratch_shapes=[pltpu.VMEM((tp, d), F32)]),
        out_shape=jax.ShapeDtypeStruct((p_rows, d), BF16),
        compiler_params=_cparams("parallel", "arbitrary"),
        name="moe_ffn",
    )(tile_e, tile_valid, xs, wg, wu, wd)


def _combine_kernel(*refs, q_max, final_norm):
    nq_ref, fe_ref, fc_ref, fr_ref = refs[:4]
    x_ref, pos_ref, gate_ref = refs[4:7]
    fw_ref = refs[7] if final_norm else None
    ys_ref, o_ref, acc_ref, buf, sem = refs[7 + int(final_norm):]
    i = pl.program_id(0)
    n = nq_ref[i]
    acc_ref[...] = jnp.zeros_like(acc_ref)
    lane = lax.broadcasted_iota(jnp.int32, pos_ref.shape, 1)

    def chunk_copy(slot, row):
        return pltpu.make_async_copy(ys_ref.at[pl.ds(pl.multiple_of(row, MOE_ALIGN), MOE_CHUNK)], buf.at[slot],
                                     sem.at[slot])

    @pl.when(n > 0)
    def _():
        chunk_copy(0, fr_ref[i * q_max]).start()

    def body(q, carry):
        slot = q % 2
        e = fe_ref[i * q_max + q]
        c = fc_ref[i * q_max + q]

        @pl.when(q + 1 < n)
        def _():
            chunk_copy(1 - slot, fr_ref[i * q_max + q + 1]).start()

        chunk_copy(slot, 0).wait()
        pos = jnp.sum(jnp.where(lane == e, pos_ref[...], 0.0), axis=1, keepdims=True)
        gate = jnp.sum(jnp.where(lane == e, gate_ref[...], 0.0), axis=1, keepdims=True)
        want = (c * MOE_CHUNK + lax.broadcasted_iota(jnp.int32, (1, MOE_CHUNK), 1)).astype(F32)
        pick = jnp.where(pos == want, 1.0, 0.0).astype(BF16)
        acc_ref[...] += gate * jnp.dot(pick, buf[slot], preferred_element_type=F32)
        return carry

    lax.fori_loop(0, n, body, 0)
    y = x_ref[...] + acc_ref[...]
    if final_norm:
        y = _rms(y, fw_ref[...])
    o_ref[...] = y


def _combine(x, ys, pos_tok, gate_tok, plan, final_w, *, tm):
    t, d = x.shape
    n_exp = pos_tok.shape[1]
    n_flat, flat_e, flat_c, flat_row = plan[:4]
    q_max = flat_e.shape[0] // (t // tm)
    row = lambda i, *_: (i, 0)
    in_specs = [pl.BlockSpec((tm, d), row), pl.BlockSpec((tm, n_exp), row), pl.BlockSpec((tm, n_exp), row)]
    args = [x, pos_tok, gate_tok]
    if final_w is not None:
        in_specs.append(pl.BlockSpec((1, d), lambda i, *_: (0, 0)))
        args.append(final_w.reshape(1, d))
    in_specs.append(pl.BlockSpec(memory_space=pl.ANY))
    args.append(ys)
    return pl.pallas_call(
        functools.partial(_combine_kernel, q_max=q_max, final_norm=final_w is not None),
        grid_spec=pltpu.PrefetchScalarGridSpec(
            num_scalar_prefetch=4,
            grid=(t // tm,),
            in_specs=in_specs,
            out_specs=pl.BlockSpec((tm, d), row),
            scratch_shapes=[pltpu.VMEM((tm, d), F32), pltpu.VMEM((2, MOE_CHUNK, d), BF16),
                            pltpu.SemaphoreType.DMA((2,))]),
        out_shape=jax.ShapeDtypeStruct((t, d), F32),
        compiler_params=_cparams("arbitrary"),
        name="moe_combine",
    )(n_flat, flat_e, flat_c, flat_row, *args)


def _moe(x, norm_w, w_router, wg, wu, wd, layer, final_w=None, *, tm, tp, tf):
    t, d = x.shape
    n_exp = w_router.shape[1]
    n_tiles = t // tm
    worst = 2 * t + n_tiles * n_exp * (MOE_ALIGN - 1) + n_exp * (MOE_CHUNK + tp - 1)
    p_rows = (worst + tp - 1) // tp * tp
    h, sel, gate, rank, cnt = _route(x, norm_w, w_router, tm=tm)
    plan = _moe_plan(cnt[:, :, 0].astype(jnp.int32), tm=tm, tp=tp, p_rows=p_rows)
    xs = _dispatch(h, sel, rank, plan, tm=tm, p_rows=p_rows)
    ys = _gffn(xs, wg, wu, wd, layer, plan[4], plan[5], tp=tp, tf=tf)
    pos_tok = jnp.where(sel > 0.0, rank, -1.0).T
    return _combine(x, ys, pos_tok, gate.T, plan, final_w, tm=tm)


def _pad_rows(x, rows):
    if x.shape[0] == rows:
        return x
    return jnp.concatenate([x, jnp.zeros((rows - x.shape[0], x.shape[1]), x.dtype)], axis=0)


def _chunk_masks(rows):
    ri = lax.broadcasted_iota(jnp.int32, (rows, rows), 0)
    ci = lax.broadcasted_iota(jnp.int32, (rows, rows), 1)
    same = (ri // CHUNK) == (ci // CHUNK)
    return same & (ci <= ri), same


def _gla_kernel(*refs, rb, zero_init, has_alias):
    it = iter(refs)
    q_ref, k_ref, v_ref, g_ref, gk_ref = next(it), next(it), next(it), next(it), next(it)
    s0_ref = None if zero_init else next(it)
    wgk_ref, bgk_ref, nw_ref = next(it), next(it), next(it)
    if has_alias:
        next(it)
    o_ref, s_ref, st_ref = next(it), next(it), next(it)
    r = pl.program_id(1)
    rows = max(rb, CHUNK)
    n_chunks = rows // CHUNK
    hk = q_ref.shape[1] // GLA_HEADS
    hv = v_ref.shape[1] // GLA_HEADS

    @pl.when(r == 0)
    def _():
        for h in range(GLA_HEADS):
            if zero_init:
                st_ref[h] = jnp.zeros(st_ref.shape[1:], F32)
            else:
                st_ref[h] = s0_ref[0, h].T

    q = _pad_rows(q_ref[...] * hk ** -0.5, rows)
    k = _pad_rows(k_ref[...], rows)
    v = _pad_rows(v_ref[...], rows)
    la = _log_sigmoid(_bdot(gk_ref[...], wgk_ref[...]) + bgk_ref[...]) * (1.0 / GLA_GATE_NORMALIZER)
    la = _pad_rows(la, rows)
    causal, same = _chunk_masks(rows)
    gc = _dot_exact_lhs(causal, la)
    gt = _dot_exact_lhs(same, la)
    q_in = q * jnp.exp(gc)
    k_in = k * jnp.exp(-gc)
    k_out = k * jnp.exp(gt - gc)
    e_tot = jnp.exp(gt)
    tri = causal[:CHUNK, :CHUNK]
    for h in range(GLA_HEADS):
        st = st_ref[h]
        ks = slice(h * hk, (h + 1) * hk)
        vs = slice(h * hv, (h + 1) * hv)
        for c in range(n_chunks):
            rs = slice(c * CHUNK, (c + 1) * CHUNK)
            qi, ki, ko, vh = q_in[rs, ks], k_in[rs, ks], k_out[rs, ks], v[rs, vs]
            intra = jnp.where(tri, _bdot_nt(qi, ki), 0.0)
            o = _bdot(intra, vh) + _bdot_nt(qi, st)
            st = st * e_tot[c * CHUNK:c * CHUNK + 1, ks] + _bdot_tn(vh, ko)
            n_out = min(rb, CHUNK)
            og = _rms(o[:n_out], nw_ref[...]) * _silu(g_ref[c * CHUNK:c * CHUNK + n_out, vs])
            o_ref[c * CHUNK:c * CHUNK + n_out, vs] = og
        st_ref[h] = st

    @pl.when(r == pl.num_programs(1) - 1)
    def _():
        for h in range(GLA_HEADS):
            s_ref[0, h] = st_ref[h].T


def _gla_core(proj, s0, w_gk2, b_gk2, norm_w, o_full, *, n_seq, seq_len, row0, rb):
    t = proj.shape[0]
    dk = w_gk2.shape[1]
    dv = 2 * dk
    hk, hv = dk // GLA_HEADS, dv // GLA_HEADS
    nblk = seq_len // rb
    base = row0 // rb
    assert row0 % rb == 0 and seq_len % rb == 0

    def rowmap(col):
        return lambda b, r: (base + b * nblk + r, col)

    in_specs = [pl.BlockSpec((rb, dk), rowmap(0)), pl.BlockSpec((rb, dk), rowmap(1)),
                pl.BlockSpec((rb, dv), rowmap(1)), pl.BlockSpec((rb, dv), rowmap(2)),
                pl.BlockSpec((rb, LANES), rowmap((2 * dk + 2 * dv) // LANES))]
    args = [proj, proj, proj, proj, proj]
    if s0 is not None:
        in_specs.append(pl.BlockSpec((1, GLA_HEADS, hk, hv), lambda b, r: (b, 0, 0, 0)))
        args.append(s0)
    wgk = jnp.pad(w_gk2, ((0, LANES - w_gk2.shape[0]), (0, 0)))
    in_specs += [pl.BlockSpec((LANES, dk), lambda b, r: (0, 0)),
                 pl.BlockSpec((1, dk), lambda b, r: (0, 0)),
                 pl.BlockSpec((1, hv), lambda b, r: (0, 0))]
    args += [wgk, b_gk2.reshape(1, dk), norm_w.reshape(1, hv)]
    aliases = {}
    if o_full is not None:
        in_specs.append(pl.BlockSpec(memory_space=pl.ANY))
        aliases = {len(args): 0}
        args.append(o_full)
    kern = functools.partial(_gla_kernel, rb=rb, zero_init=s0 is None, has_alias=o_full is not None)
    return pl.pallas_call(
        kern,
        grid=(n_seq, nblk),
        in_specs=in_specs,
        out_specs=[pl.BlockSpec((rb, dv), rowmap(0)),
                   pl.BlockSpec((1, GLA_HEADS, hk, hv), lambda b, r: (b, 0, 0, 0))],
        out_shape=[jax.ShapeDtypeStruct((t, dv), F32),
                   jax.ShapeDtypeStruct((n_seq, GLA_HEADS, hk, hv), F32)],
        scratch_shapes=[pltpu.VMEM((GLA_HEADS, hv, hk), F32)],
        input_output_aliases=aliases,
        compiler_params=_cparams("parallel", "arbitrary"),
        name="gla_core",
    )(*args)


def _tile(n, pref):
    if n <= pref:
        return n
    for c in range(pref, 7, -8):
        if n % c == 0:
            return c
    return n


def _pad_cols(w, n):
    return jnp.pad(w, ((0, 0), (0, n - w.shape[1])))


def _gla_layer(x, nw, s0_s, w_in, w_gk2, b_gk2, norm_w, w_out, dims):
    n_p, t_p, n_s, t_s = dims
    t = x.shape[0]
    tm = _tile(t, 1280)
    dk = w_gk2.shape[1]
    width = 6 * dk + LANES
    proj = _linear(x, _pad_cols(w_in, width), norm_w=nw, tm=tm, tn=_tile(width, 640), name="gla_in")
    o, sp = _gla_core(proj, None, w_gk2, b_gk2, norm_w, None, n_seq=n_p, seq_len=t_p, row0=0,
                      rb=min(t_p, 256))
    o, ss = _gla_core(proj, s0_s, w_gk2, b_gk2, norm_w, o, n_seq=n_s, seq_len=t_s, row0=n_p * t_p,
                      rb=t_s)
    x = _linear(o, w_out, res=x, tm=tm, tn=512, name="gla_out")
    return x, sp, ss


def _rope_tables(pos, half):
    freqs = np.exp(-math.log(ROPE_THETA) * np.arange(half, dtype=np.float64) / half)
    ang = np.asarray(pos, np.float64)[:, None] * freqs[None, :]
    cos, sin = np.cos(ang), np.sin(ang)
    return (jnp.asarray(np.concatenate([cos, cos], axis=-1), F32),
            jnp.asarray(np.concatenate([-sin, sin], axis=-1), F32))


def _swap_halves(w, axis=-1):
    a, b = jnp.split(w, 2, axis=axis)
    return jnp.concatenate([b, a], axis=axis)


def _mla_in_kernel(x_ref, nw_ref, w_ref, kvw_ref, cos_ref, sin_ref, cq_ref, ckv_ref, kr_ref):
    h = _rms(x_ref[...], nw_ref[...])
    y = _bdot(h, w_ref[...])
    cq_ref[...] = y[:, :MLA_Q_LORA]
    ckv_ref[...] = _rms(y[:, MLA_Q_LORA:MLA_Q_LORA + MLA_KV_LORA], kvw_ref[...])
    o = MLA_Q_LORA + MLA_KV_LORA
    kr_ref[...] = (y[:, o:o + MLA_ROPE] * cos_ref[...]
                   + y[:, o + LANES:o + LANES + MLA_ROPE] * sin_ref[...])


def _mla_in(x, nw, w_in, kv_norm_w, cos, sin, *, tm):
    t, d = x.shape
    o = MLA_Q_LORA + MLA_KV_LORA
    kr_w = w_in[:, o:o + MLA_ROPE]
    w_aug = jnp.concatenate([w_in[:, :o], _pad_cols(kr_w, LANES), _pad_cols(_swap_halves(kr_w), LANES)],
                            axis=1)
    wid = w_aug.shape[1]
    row = lambda i: (i, 0)
    fix = lambda i: (0, 0)
    return pl.pallas_call(
        _mla_in_kernel,
        grid=(t // tm,),
        in_specs=[pl.BlockSpec((tm, d), row), pl.BlockSpec((1, d), fix), pl.BlockSpec((d, wid), fix),
                  pl.BlockSpec((1, MLA_KV_LORA), fix), pl.BlockSpec((tm, MLA_ROPE), row),
                  pl.BlockSpec((tm, MLA_ROPE), row)],
        out_specs=[pl.BlockSpec((tm, MLA_Q_LORA), row), pl.BlockSpec((tm, MLA_KV_LORA), row),
                   pl.BlockSpec((tm, MLA_ROPE), row)],
        out_shape=[jax.ShapeDtypeStruct((t, MLA_Q_LORA), F32), jax.ShapeDtypeStruct((t, MLA_KV_LORA), F32),
                   jax.ShapeDtypeStruct((t, MLA_ROPE), F32)],
        compiler_params=_cparams("parallel"),
        name="mla_in",
    )(x, nw.reshape(1, d), w_aug, kv_norm_w.reshape(1, -1), cos, sin)


def _mla_q_kernel(cq_ref, qw_ref, wn_ref, wr_ref, ws_ref, wk_ref, cos_ref, sin_ref, ql_ref, qr_ref):
    cq = _rms(cq_ref[...], qw_ref[...]).astype(BF16)
    qn = jnp.dot(cq, wn_ref[...].astype(BF16), preferred_element_type=F32).astype(BF16)
    for j in range(MLA_HEADS // 2):
        ql = jnp.dot(qn[:, j * LANES:(j + 1) * LANES], wk_ref[j].astype(BF16), preferred_element_type=F32)
        ql_ref[2 * j] = ql[:, :MLA_KV_LORA]
        ql_ref[2 * j + 1] = ql[:, MLA_KV_LORA:]
    qr = jnp.dot(cq, wr_ref[...].astype(BF16), preferred_element_type=F32)
    qs = jnp.dot(cq, ws_ref[...].astype(BF16), preferred_element_type=F32)
    per = LANES // MLA_ROPE
    cos = jnp.concatenate([cos_ref[...]] * (MLA_HEADS // per), axis=-1)
    sin = jnp.concatenate([sin_ref[...]] * (MLA_HEADS // per), axis=-1)
    rot = qr * cos + qs * sin
    for h in range(MLA_HEADS):
        qr_ref[h] = rot[:, h * MLA_ROPE:(h + 1) * MLA_ROPE]


def _mla_q(cq, q_norm_w, w_uq, w_uk, cos, sin, *, tm):
    t = cq.shape[0]
    per = LANES // MLA_ROPE
    w3 = w_uq.reshape(MLA_Q_LORA, MLA_HEADS, MLA_NOPE + MLA_ROPE)
    w_nope = w3[:, :, :MLA_NOPE].reshape(MLA_Q_LORA, MLA_HEADS * MLA_NOPE)
    w_rope = w3[:, :, MLA_NOPE:].reshape(MLA_Q_LORA, MLA_HEADS * MLA_ROPE)
    w_swap = _swap_halves(w3[:, :, MLA_NOPE:]).reshape(MLA_Q_LORA, MLA_HEADS * MLA_ROPE)
    a = jnp.transpose(w_uk, (1, 2, 0))
    z = jnp.zeros_like(a[0::2])
    w_bd = jnp.concatenate([jnp.concatenate([a[0::2], z], axis=2),
                            jnp.concatenate([z, a[1::2]], axis=2)], axis=1)
    cos4 = jnp.tile(cos, (1, per))
    sin4 = jnp.tile(sin, (1, per))
    row = lambda i: (i, 0)
    fix2 = lambda i: (0, 0)
    return pl.pallas_call(
        _mla_q_kernel,
        grid=(t // tm,),
        in_specs=[pl.BlockSpec((tm, MLA_Q_LORA), row), pl.BlockSpec((1, MLA_Q_LORA), fix2),
                  pl.BlockSpec(w_nope.shape, fix2), pl.BlockSpec(w_rope.shape, fix2),
                  pl.BlockSpec(w_swap.shape, fix2), pl.BlockSpec(w_bd.shape, lambda i: (0, 0, 0)),
                  pl.BlockSpec((tm, LANES), row), pl.BlockSpec((tm, LANES), row)],
        out_specs=[pl.BlockSpec((MLA_HEADS, tm, MLA_KV_LORA), lambda i: (0, i, 0)),
                   pl.BlockSpec((MLA_HEADS, tm, MLA_ROPE), lambda i: (0, i, 0))],
        out_shape=[jax.ShapeDtypeStruct((MLA_HEADS, t, MLA_KV_LORA), F32),
                   jax.ShapeDtypeStruct((MLA_HEADS, t, MLA_ROPE), F32)],
        compiler_params=_cparams("parallel"),
        name="mla_q",
    )(cq, q_norm_w.reshape(1, -1), w_nope, w_rope, w_swap, w_bd, cos4, sin4)


MLA_QSCALE = (MLA_NOPE + MLA_ROPE) ** -0.5 * math.log2(math.e)


def _lane_repeat(x, width):
    return jnp.concatenate([x] * (width // LANES), axis=1)


def _flash_chunk(s, cb, m_ref, l_ref, acc_ref, rs):
    m_prev = m_ref[rs]
    m_new = jnp.maximum(m_prev, jnp.max(s, axis=-1, keepdims=True))
    alpha = jnp.exp2(m_prev - m_new)
    p = jnp.exp2(s - _lane_repeat(m_new, s.shape[1]))
    l_ref[rs] = alpha * l_ref[rs] + jnp.sum(p, axis=-1, keepdims=True)
    acc_ref[rs] = (_lane_repeat(alpha, acc_ref.shape[1]) * acc_ref[rs]
                   + jnp.dot(p.astype(BF16), cb, preferred_element_type=F32))
    m_ref[rs] = m_new


def _flash_init(ql_ref, qr_ref, qlb_ref, qrb_ref, m_ref, l_ref, acc_ref):
    rows = qlb_ref.shape[0]
    qlb_ref[...] = (ql_ref[...].reshape(rows, MLA_KV_LORA) * MLA_QSCALE).astype(BF16)
    qrb_ref[...] = (qr_ref[...].reshape(rows, MLA_ROPE) * MLA_QSCALE).astype(BF16)
    m_ref[...] = jnp.full(m_ref.shape, NEG_INF, F32)
    l_ref[...] = jnp.zeros(l_ref.shape, F32)
    acc_ref[...] = jnp.zeros(acc_ref.shape, F32)


def _mla_finish(acc_ref, l_ref, wv_ref, o_ref, rows_per_head):
    inv = _lane_repeat(1.0 / l_ref[...], acc_ref.shape[1])
    outs = []
    for j in range(MLA_HEADS // 2):
        pair = None
        for h in (2 * j, 2 * j + 1):
            rs = slice(h * rows_per_head, (h + 1) * rows_per_head)
            part = _bdot(acc_ref[rs] * inv[rs], wv_ref[h])
            pair = part if pair is None else pair + part
        outs.append(pair)
    o_ref[...] = jnp.concatenate(outs, axis=-1)


def _mla_prompt_kernel(wq_ref, wk_ref, ql_ref, qr_ref, c_ref, r_ref, wv_ref, o_ref, qlb_ref, qrb_ref, m_ref,
                       l_ref, acc_ref, *, tq, tk, rc):
    qi = wq_ref[pl.program_id(1)]
    kj = wk_ref[pl.program_id(1)]
    last = (qi * tq + tq - 1) // tk
    rows = MLA_HEADS * tq

    @pl.when(kj == 0)
    def _():
        _flash_init(ql_ref, qr_ref, qlb_ref, qrb_ref, m_ref, l_ref, acc_ref)

    def step(masked):
        cb = c_ref[...].astype(BF16)
        rb = r_ref[...].astype(BF16)
        chunks = [slice(ch * rc, (ch + 1) * rc) for ch in range(rows // rc)]
        scores = [_bdot_nt(qlb_ref[rs], cb) + _bdot_nt(qrb_ref[rs], rb) for rs in chunks]
        if masked:
            q_pos = qi * tq + lax.broadcasted_iota(jnp.int32, (rc, tk), 0) % tq
            k_pos = kj * tk + lax.broadcasted_iota(jnp.int32, (rc, tk), 1)
            scores = [jnp.where(k_pos <= q_pos, s, NEG_INF) for s in scores]
        probs, alphas = [], []
        for rs, s in zip(chunks, scores):
            m_prev = m_ref[rs]
            m_new = jnp.maximum(m_prev, jnp.max(s, axis=-1, keepdims=True))
            alpha = jnp.exp2(m_prev - m_new)
            p = jnp.exp2(s - _lane_repeat(m_new, tk))
            l_ref[rs] = alpha * l_ref[rs] + jnp.sum(p, axis=-1, keepdims=True)
            m_ref[rs] = m_new
            probs.append(p.astype(BF16))
            alphas.append(alpha)
        for rs, p, alpha in zip(chunks, probs, alphas):
            acc_ref[rs] = (_lane_repeat(alpha, MLA_KV_LORA) * acc_ref[rs]
                           + jnp.dot(p, cb, preferred_element_type=F32))

    @pl.when(kj < last)
    def _():
        step(False)

    @pl.when(kj == last)
    def _():
        step(True)
        _mla_finish(acc_ref, l_ref, wv_ref, o_ref, tq)


def _pad_uv(w_uv):
    a = jnp.transpose(w_uv, (1, 0, 2))
    z = jnp.zeros_like(a)
    even = (jnp.arange(a.shape[0]) % 2 == 0)[:, None, None]
    return jnp.concatenate([jnp.where(even, a, z), jnp.where(even, z, a)], axis=2)


def _mla_prompt_attn(q_lat, q_rope, c_kv, k_r, w_uv_pad, *, n_seq, seq_len, tq, tk):
    t = c_kv.shape[0]
    nq, nk = seq_len // tq, seq_len // tk
    assert tk % tq == 0
    pairs = [(i, j) for i in range(nq) for j in range((i * tq + tq - 1) // tk + 1)]
    work_q = jnp.asarray([p[0] for p in pairs], jnp.int32)
    work_k = jnp.asarray([p[1] for p in pairs], jnp.int32)

    def qmap(b, w, wq, wk):
        return (0, b * nq + wq[w], 0)

    def kmap(b, w, wq, wk):
        return (b * nk + wk[w], 0)

    rows = MLA_HEADS * tq
    return pl.pallas_call(
        functools.partial(_mla_prompt_kernel, tq=tq, tk=tk, rc=4 * tq),
        grid_spec=pltpu.PrefetchScalarGridSpec(
            num_scalar_prefetch=2,
            grid=(n_seq, len(pairs)),
            in_specs=[pl.BlockSpec((MLA_HEADS, tq, MLA_KV_LORA), qmap),
                      pl.BlockSpec((MLA_HEADS, tq, MLA_ROPE), qmap),
                      pl.BlockSpec((tk, MLA_KV_LORA), kmap), pl.BlockSpec((tk, MLA_ROPE), kmap),
                      pl.BlockSpec(w_uv_pad.shape, lambda b, w, wq, wk: (0, 0, 0))],
            out_specs=pl.BlockSpec((tq, MLA_HEADS * MLA_V), lambda b, w, wq, wk: (b * nq + wq[w], 0)),
            scratch_shapes=_flash_scratch(rows)),
        out_shape=jax.ShapeDtypeStruct((t, MLA_HEADS * MLA_V), F32),
        compiler_params=_cparams("parallel", "arbitrary"),
        name="mla_prompt_attn",
    )(work_q, work_k, q_lat, q_rope, c_kv, k_r, w_uv_pad)


def _flash_scratch(rows):
    return [pltpu.VMEM((rows, MLA_KV_LORA), BF16), pltpu.VMEM((rows, MLA_ROPE), BF16),
            pltpu.VMEM((rows, LANES), F32), pltpu.VMEM((rows, LANES), F32),
            pltpu.VMEM((rows, MLA_KV_LORA), F32)]


def _mla_sample_kernel(pt_ref, ql_ref, qr_ref, lat_hbm, kr_hbm, cn_ref, rn_ref, wv_ref, alias_ref, o_ref,
                       lat_buf, kr_buf, sem, qlb_ref, qrb_ref, m_ref, l_ref, acc_ref, *, n_pg, t_s):
    del alias_ref
    b = pl.program_id(0)
    n_groups = pt_ref.shape[1] // n_pg
    rows = MLA_HEADS * t_s
    everything = slice(0, rows)
    _flash_init(ql_ref, qr_ref, qlb_ref, qrb_ref, m_ref, l_ref, acc_ref)

    def copies(seq, g, slot):
        out = []
        for p in range(n_pg):
            page = pt_ref[seq, g * n_pg + p]
            out.append(pltpu.make_async_copy(lat_hbm.at[0, page], lat_buf.at[slot, p], sem.at[0, slot]))
            out.append(pltpu.make_async_copy(kr_hbm.at[0, page], kr_buf.at[slot, p], sem.at[1, slot]))
        return out

    @pl.when(b == 0)
    def _():
        for c in copies(0, 0, 0):
            c.start()

    def body(g, carry):
        slot = g % 2

        @pl.when(g + 1 < n_groups)
        def _():
            for c in copies(b, g + 1, 1 - slot):
                c.start()

        @pl.when((g + 1 == n_groups) & (b + 1 < pl.num_programs(0)))
        def _():
            for c in copies(b + 1, 0, 0):
                c.start()

        for c in copies(b, 0, slot):
            c.wait()
        page = lat_buf.shape[2]
        cb = lat_buf[slot].reshape(n_pg * page, MLA_KV_LORA).astype(BF16)
        rbt = jnp.concatenate([kr_buf[slot, p] for p in range(n_pg)], axis=1).astype(BF16)
        s = _bdot_nt(qlb_ref[...], cb) + jnp.dot(qrb_ref[...], rbt, preferred_element_type=F32)
        _flash_chunk(s, cb, m_ref, l_ref, acc_ref, everything)
        return carry

    lax.fori_loop(0, n_groups, body, 0)
    cb = _pad_rows(cn_ref[...], LANES).astype(BF16)
    rb = _pad_rows(rn_ref[...], LANES).astype(BF16)
    s = _bdot_nt(qlb_ref[...], cb) + _bdot_nt(qrb_ref[...], rb)
    q_t = lax.broadcasted_iota(jnp.int32, (rows, LANES), 0) % t_s
    k_t = lax.broadcasted_iota(jnp.int32, (rows, LANES), 1)
    s = jnp.where(k_t <= q_t, s, NEG_INF)
    _flash_chunk(s, cb, m_ref, l_ref, acc_ref, everything)
    _mla_finish(acc_ref, l_ref, wv_ref, o_ref, t_s)


def _mla_sample_attn(q_lat, q_rope, c_kv, k_r, cache_lat, cache_kr_t, page_table, w_uv_pad, o_full, *,
                     n_seq, t_s, row0, n_pg):
    n_pages = page_table.shape[1]
    page = cache_lat.shape[2]
    assert n_pages % (2 * n_pg) == 0 and row0 % t_s == 0
    base = row0 // t_s
    qmap = lambda b, pt: (0, base + b, 0)
    newmap = lambda b, pt: (base + b, 0)
    hbm = pl.BlockSpec(memory_space=pl.ANY)
    in_specs = [pl.BlockSpec((MLA_HEADS, t_s, MLA_KV_LORA), qmap), pl.BlockSpec((MLA_HEADS, t_s, MLA_ROPE), qmap),
                hbm, hbm,
                pl.BlockSpec((t_s, MLA_KV_LORA), newmap), pl.BlockSpec((t_s, MLA_ROPE), newmap),
                pl.BlockSpec(w_uv_pad.shape, lambda b, pt: (0, 0, 0)), hbm]
    scratch = [pltpu.VMEM((2, n_pg, page, MLA_KV_LORA), F32), pltpu.VMEM((2, n_pg, MLA_ROPE, page), F32),
               pltpu.SemaphoreType.DMA((2, 2))]
    return pl.pallas_call(
        functools.partial(_mla_sample_kernel, n_pg=n_pg, t_s=t_s),
        grid_spec=pltpu.PrefetchScalarGridSpec(
            num_scalar_prefetch=1,
            grid=(n_seq,),
            in_specs=in_specs,
            out_specs=pl.BlockSpec((t_s, MLA_HEADS * MLA_V), newmap),
            scratch_shapes=scratch + _flash_scratch(MLA_HEADS * t_s)),
        out_shape=jax.ShapeDtypeStruct(o_full.shape, F32),
        input_output_aliases={len(in_specs): 0},
        compiler_params=_cparams("arbitrary"),
        name="mla_sample_attn",
    )(page_table, q_lat, q_rope, cache_lat, cache_kr_t, c_kv, k_r, w_uv_pad, o_full)


def _positions(dims, past_len):
    n_p, t_p, n_s, t_s = dims
    return np.concatenate([np.tile(np.arange(t_p), n_p), np.tile(past_len + np.arange(t_s), n_s)])


def _mla_layer(x, nw, cache_lat, cache_kr, page_table, w_in, q_norm_w, w_uq, kv_norm_w, w_uk, w_uv, w_out,
               dims):
    n_p, t_p, n_s, t_s = dims
    t = x.shape[0]
    n_pages = page_table.shape[1]
    past_len = n_pages * cache_lat.shape[2]
    cos, sin = _rope_tables(_positions(dims, past_len), MLA_ROPE // 2)
    cq, ckv, kr = _mla_in(x, nw, w_in, kv_norm_w, cos, sin, tm=_tile(t, 640))
    q_lat, q_rope = _mla_q(cq, q_norm_w, w_uq, w_uk, cos, sin, tm=_tile(t, 256))
    wv = _pad_uv(w_uv)
    o = _mla_prompt_attn(q_lat, q_rope, ckv, kr, wv, n_seq=n_p, seq_len=t_p, tq=min(t_p, 128),
                         tk=min(t_p, 256))
    o = _mla_sample_attn(q_lat, q_rope, ckv, kr, cache_lat, jnp.swapaxes(cache_kr, 2, 3), page_table, wv, o,
                         n_seq=n_s, t_s=t_s, row0=n_p * t_p, n_pg=math.gcd(n_pages // 2, 32))
    x = _linear(o, w_out, res=x, tm=_tile(t, 1280), tn=512, name="mla_out")
    return x, ckv, kr


def _lane_halves(a):
    half = LANES // 2
    low = lax.broadcasted_iota(jnp.int32, a.shape, 1) < half
    rolled = pltpu.roll(a, half, axis=1)
    head0 = (jnp.where(low, a, 0.0), jnp.where(low, 0.0, rolled))
    head1 = (jnp.where(low, rolled, 0.0), jnp.where(low, 0.0, a))
    return head0, head1


def _swa_heads(q, k_all, v_all, sink_ref, mask, o_ref):
    rq = q.shape[0]
    scale = SWA_HD ** -0.5
    top = lax.broadcasted_iota(jnp.int32, (2 * rq, 1), 0) < rq
    for cg in range(SWA_KV_HEADS // 2):
        k_heads = _lane_halves(k_all[:, cg * LANES:(cg + 1) * LANES])
        v_heads = _lane_halves(v_all[:, cg * LANES:(cg + 1) * LANES])
        for sub in range(2):
            kh = 2 * cg + sub
            (k_lo, k_hi), (v_lo, v_hi) = k_heads[sub], v_heads[sub]
            qs = jnp.concatenate([q[:, (2 * kh) * LANES:(2 * kh + 1) * LANES],
                                  q[:, (2 * kh + 1) * LANES:(2 * kh + 2) * LANES]], axis=0)
            acc = None
            for which, (kk, vv) in enumerate(((k_lo, v_lo), (k_hi, v_hi))):
                s = jnp.where(mask, _bdot_nt(qs, kk) * scale, NEG_INF)
                sink = jnp.where(top, sink_ref[4 * kh + which], sink_ref[4 * kh + 2 + which])
                m = jnp.maximum(jnp.max(s, axis=-1, keepdims=True), sink)
                e = jnp.exp(s - m)
                p = e / (jnp.sum(e, axis=-1, keepdims=True) + jnp.exp(sink - m))
                part = _bdot(p, vv)
                acc = part if acc is None else acc + part
            o_ref[:, (2 * kh) * LANES:(2 * kh + 1) * LANES] = acc[:rq]
            o_ref[:, (2 * kh + 1) * LANES:(2 * kh + 2) * LANES] = acc[rq:]


def _swa_prompt_kernel(sink_ref, q_ref, kp_ref, kc_ref, vp_ref, vc_ref, o_ref):
    n = pl.program_id(1)
    w = q_ref.shape[0]
    k_all = jnp.concatenate([kp_ref[...], kc_ref[...]], axis=0)
    v_all = jnp.concatenate([vp_ref[...], vc_ref[...]], axis=0)
    r = lax.broadcasted_iota(jnp.int32, (2 * w, 2 * w), 0) % w
    c = lax.broadcasted_iota(jnp.int32, (2 * w, 2 * w), 1)
    mask = (c >= r) & (c <= r + w) & ((n > 0) | (c >= w))
    _swa_heads(q_ref[...], k_all, v_all, sink_ref, mask, o_ref)


def _swa_prompt_attn(qkv, sinks, *, n_seq, seq_len):
    t = qkv.shape[0]
    w = WINDOW
    nb = seq_len // w
    dq = SWA_HEADS * SWA_HD
    dkv = SWA_KV_HEADS * SWA_HD
    kcol = dq // dkv
    cur = lambda col: (lambda b, n: (b * nb + n, col))
    prev = lambda col: (lambda b, n: (b * nb + jnp.maximum(n - 1, 0), col))
    return pl.pallas_call(
        _swa_prompt_kernel,
        grid=(n_seq, nb),
        in_specs=[pl.BlockSpec(memory_space=pltpu.SMEM),
                  pl.BlockSpec((w, dq), cur(0)),
                  pl.BlockSpec((w, dkv), prev(kcol)), pl.BlockSpec((w, dkv), cur(kcol)),
                  pl.BlockSpec((w, dkv), prev(kcol + 1)), pl.BlockSpec((w, dkv), cur(kcol + 1))],
        out_specs=pl.BlockSpec((w, dq), cur(0)),
        out_shape=jax.ShapeDtypeStruct((t, dq), F32),
        compiler_params=_cparams("parallel", "parallel"),
        name="swa_prompt_attn",
    )(sinks, qkv, qkv, qkv, qkv, qkv)


def _swa_sample_kernel(sink_ref, q_ref, kn_ref, vn_ref, kc_ref, vc_ref, alias_ref, o_ref, ko_ref, vo_ref):
    del alias_ref
    t_s = q_ref.shape[0]
    w = kc_ref.shape[0]
    k_all = jnp.concatenate([kc_ref[...], kn_ref[...]], axis=0)
    v_all = jnp.concatenate([vc_ref[...], vn_ref[...]], axis=0)
    r = lax.broadcasted_iota(jnp.int32, (2 * t_s, w + t_s), 0) % t_s
    c = lax.broadcasted_iota(jnp.int32, (2 * t_s, w + t_s), 1)
    mask = (c <= w + r) & (c >= r)
    _swa_heads(q_ref[...], k_all, v_all, sink_ref, mask, o_ref)
    ko_ref[...] = k_all[t_s:]
    vo_ref[...] = v_all[t_s:]


def _swa_sample_attn(qkv, cache_k, cache_v, sinks, o_full, *, n_seq, t_s, row0):
    w = cache_k.shape[1]
    dq = SWA_HEADS * SWA_HD
    dkv = SWA_KV_HEADS * SWA_HD
    kcol = dq // dkv
    base = row0 // t_s
    new = lambda col: (lambda b: (base + b, col))
    seq = lambda b: (b, 0, 0)
    return pl.pallas_call(
        _swa_sample_kernel,
        grid=(n_seq,),
        in_specs=[pl.BlockSpec(memory_space=pltpu.SMEM),
                  pl.BlockSpec((t_s, dq), new(0)),
                  pl.BlockSpec((t_s, dkv), new(kcol)), pl.BlockSpec((t_s, dkv), new(kcol + 1)),
                  pl.BlockSpec((None, w, dkv), seq), pl.BlockSpec((None, w, dkv), seq),
                  pl.BlockSpec(memory_space=pl.ANY)],
        out_specs=[pl.BlockSpec((t_s, dq), new(0)),
                   pl.BlockSpec((None, w, dkv), seq), pl.BlockSpec((None, w, dkv), seq)],
        out_shape=[jax.ShapeDtypeStruct(o_full.shape, F32),
                   jax.ShapeDtypeStruct(cache_k.shape, F32), jax.ShapeDtypeStruct(cache_v.shape, F32)],
        input_output_aliases={6: 0},
        compiler_params=_cparams("parallel"),
        name="swa_sample_attn",
    )(sinks, qkv, qkv, qkv, cache_k, cache_v, o_full)


def _swa_layer(x, nw, cache_k, cache_v, w_qkv, b_qkv, sinks, w_out, b_out, dims):
    n_p, t_p, n_s, t_s = dims
    t = x.shape[0]
    tm = _tile(t, 1280)
    dq = SWA_HEADS * SWA_HD
    dkv = SWA_KV_HEADS * SWA_HD
    qkv = _linear(x, w_qkv, norm_w=nw, bias=b_qkv, tm=tm, tn=512, name="swa_in")
    o = _swa_prompt_attn(qkv, sinks, n_seq=n_p, seq_len=t_p)
    o, k_s, v_s = _swa_sample_attn(qkv, cache_k.reshape(n_s, WINDOW, dkv), cache_v.reshape(n_s, WINDOW, dkv),
                                   sinks, o, n_seq=n_s, t_s=t_s, row0=n_p * t_p)
    x = _linear(o, w_out, bias=b_out, res=x, tm=tm, tn=512, name="swa_out")
    kv_p = jnp.stack([lax.slice(qkv, ((b + 1) * t_p - WINDOW, dq), ((b + 1) * t_p, dq + 2 * dkv))
                      for b in range(n_p)])
    kv_shape = (n_p, WINDOW, SWA_KV_HEADS, SWA_HD)
    k_p = kv_p[:, :, :dkv].reshape(kv_shape)
    v_p = kv_p[:, :, dkv:].reshape(kv_shape)
    return x, k_p, v_p, k_s.reshape(cache_k.shape), v_s.reshape(cache_v.shape)


def _l2norm(x):
    return x * lax.rsqrt(jnp.sum(x * x, axis=-1, keepdims=True) + 1e-6)


def _split2(x):
    h1 = x.astype(BF16)
    return h1, (x - h1.astype(F32)).astype(BF16)


def _bmm3(a, b):
    a1, a2 = _split2(a)
    b1, b2 = _split2(b)
    dot = lambda x, y: jnp.einsum("bij,bjk->bik", x, y, preferred_element_type=F32)
    return dot(a1, b1) + dot(a1, b2) + dot(a2, b1)


def _unit_lower_inverse(low):
    n = low.shape[-1]
    eye = (lax.broadcasted_iota(jnp.int32, (n, n), 0) == lax.broadcasted_iota(jnp.int32, (n, n), 1))
    eye = eye.astype(F32)[None]
    power = -low
    inv = eye + power
    steps = int(math.log2(n)) - 1
    for _ in range(steps):
        power = _bmm3(power, power)
        inv = inv + _bmm3(inv, power)
    return inv


def _dn_kernel(*refs, rb, zero_init, has_alias):
    it = iter(refs)
    x_ref, z_ref, ab_ref = next(it), next(it), next(it)
    s0_ref, c0_ref = (None, None) if zero_init else (next(it), next(it))
    cw_ref, al_ref, dt_ref, nw_ref = next(it), next(it), next(it), next(it)
    if has_alias:
        next(it)
    o_ref, s_ref, co_ref = next(it), next(it), next(it)
    st_ref, xp_ref = next(it), next(it)
    r = pl.program_id(1)
    rows = max(rb, CHUNK)
    n_chunks = rows // CHUNK
    halo = 8
    dqk = DN_HEADS * DN_HK

    @pl.when(r == 0)
    def _():
        if zero_init:
            st_ref[...] = jnp.zeros(st_ref.shape, F32)
            xp_ref[0:halo] = jnp.zeros((halo, xp_ref.shape[1]), F32)
        else:
            st_ref[...] = s0_ref[0]
            xp_ref[0:halo] = c0_ref[0]

    xp_ref[halo:halo + rb] = x_ref[...]
    full = xp_ref[...]
    conv = full[halo:] * cw_ref[DN_CONV - 1:DN_CONV]
    for w in range(DN_CONV - 1):
        conv = conv + pltpu.roll(full, DN_CONV - 1 - w, axis=0)[halo:] * cw_ref[w:w + 1]
    tail = xp_ref[rb:rb + halo]
    co_ref[0] = tail
    xp_ref[0:halo] = tail
    qkv = _pad_rows(_silu(conv), rows)
    ab = ab_ref[...]
    g_all = _pad_rows(-jnp.exp(al_ref[...]) * _softplus(ab + dt_ref[...]), rows)
    beta_all = _pad_rows(1.0 / (1.0 + jnp.exp(-ab)), rows)

    ri = lax.broadcasted_iota(jnp.int32, (rows, rows), 0)
    ci = lax.broadcasted_iota(jnp.int32, (rows, rows), 1)
    same = (ri // CHUNK) == (ci // CHUNK)
    causal = same & (ci <= ri)
    strict = same & (ci < ri)
    upper = (same & (ri <= ci)).astype(BF16)
    gc_col = _dot_exact_lhs(causal, g_all)
    gt_col = _dot_exact_lhs(same, g_all)
    g1, g2, g3 = _split3(g_all)
    tn = lambda a: lax.dot_general(a, upper, (((0,), (0,)), ((), ())), preferred_element_type=F32)
    gc_row = tn(g1) + tn(g2) + tn(g3)

    lows, rhss, attns, qds, kos, gls = [], [], [], [], [], []
    for h in range(DN_HEADS):
        hs = slice(h * DN_HK, (h + 1) * DN_HK)
        q = _l2norm(qkv[:, hs]) * DN_HK ** -0.5
        k = _l2norm(qkv[:, dqk + h * DN_HK:dqk + (h + 1) * DN_HK])
        v = qkv[:, 2 * dqk + h * DN_HV:2 * dqk + (h + 1) * DN_HV]
        beta = beta_all[:, DN_HEADS + h:DN_HEADS + h + 1]
        gc = gc_col[:, h:h + 1]
        gt = gt_col[:, h:h + 1]
        decay = jnp.where(causal, jnp.exp(jnp.where(causal, gc - gc_row[h:h + 1, :], 0.0)), 0.0)
        kb = k * beta
        low = jnp.where(strict, _bdot_nt(kb, k) * decay, 0.0)
        attn = _bdot_nt(q, k) * decay
        rhs = jnp.concatenate([v * beta, kb * jnp.exp(gc)], axis=-1)
        q_dec = q * jnp.exp(gc)
        k_out = k * jnp.exp(gt - gc)
        g_last = jnp.exp(gt)
        for c in range(n_chunks):
            rs = slice(c * CHUNK, (c + 1) * CHUNK)
            lows.append(low[rs, rs])
            attns.append(attn[rs, rs])
            rhss.append(rhs[rs])
            qds.append(q_dec[rs])
            kos.append(k_out[rs])
            gls.append(g_last[c * CHUNK:c * CHUNK + 1])
    sol = _bmm3(_unit_lower_inverse(jnp.stack(lows)), jnp.stack(rhss))
    n_out = min(rb, CHUNK)
    for h in range(DN_HEADS):
        s = st_ref[h]
        for c in range(n_chunks):
            i = h * n_chunks + c
            u, wm = sol[i, :, :DN_HV], sol[i, :, DN_HV:]
            v_new = u - _bdot(wm, s)
            o = _bdot(qds[i], s) + _bdot(attns[i], v_new)
            s = s * gls[i] + _bdot_tn(kos[i], v_new)
            zs = z_ref[c * CHUNK:c * CHUNK + n_out, h * DN_HV:(h + 1) * DN_HV]
            o_ref[c * CHUNK:c * CHUNK + n_out, h * DN_HV:(h + 1) * DN_HV] = (
                _rms(o[:n_out], nw_ref[...]) * _silu(zs))
        st_ref[h] = s

    @pl.when(r == pl.num_programs(1) - 1)
    def _():
        s_ref[0] = st_ref[...]


def _dn_core(proj, s0, conv0, conv_w, a_log, dt_bias, norm_w, o_full, *, n_seq, seq_len, row0, rb):
    t = proj.shape[0]
    dconv = conv_w.shape[1]
    dz = DN_HEADS * DN_HV
    nblk = seq_len // rb
    base = row0 // rb
    assert row0 % rb == 0 and seq_len % rb == 0 and rb % 8 == 0

    def rowmap(col):
        return lambda b, r: (base + b * nblk + r, col)

    seq4 = lambda b, r: (b, 0, 0, 0)
    seq3 = lambda b, r: (b, 0, 0)
    fix = lambda b, r: (0, 0)
    in_specs = [pl.BlockSpec((rb, dconv), rowmap(0)), pl.BlockSpec((rb, dz), rowmap(dconv // dz)),
                pl.BlockSpec((rb, LANES), rowmap((dconv + dz) // LANES))]
    args = [proj, proj, proj]
    if s0 is not None:
        in_specs += [pl.BlockSpec((1, DN_HEADS, DN_HK, DN_HV), seq4), pl.BlockSpec((1, 8, dconv), seq3)]
        args += [s0, conv0]
    in_specs += [pl.BlockSpec((DN_CONV, dconv), fix), pl.BlockSpec((1, LANES), fix),
                 pl.BlockSpec((1, LANES), fix), pl.BlockSpec((1, DN_HV), fix)]
    args += [conv_w, _pad_cols(a_log.reshape(1, -1), LANES), _pad_cols(dt_bias.reshape(1, -1), LANES),
             norm_w.reshape(1, -1)]
    aliases = {}
    if o_full is not None:
        in_specs.append(pl.BlockSpec(memory_space=pl.ANY))
        aliases = {len(args): 0}
        args.append(o_full)
    kern = functools.partial(_dn_kernel, rb=rb, zero_init=s0 is None, has_alias=o_full is not None)
    return pl.pallas_call(
        kern,
        grid=(n_seq, nblk),
        in_specs=in_specs,
        out_specs=[pl.BlockSpec((rb, dz), rowmap(0)),
                   pl.BlockSpec((1, DN_HEADS, DN_HK, DN_HV), seq4),
                   pl.BlockSpec((1, 8, dconv), seq3)],
        out_shape=[jax.ShapeDtypeStruct((t, dz), F32),
                   jax.ShapeDtypeStruct((n_seq, DN_HEADS, DN_HK, DN_HV), F32),
                   jax.ShapeDtypeStruct((n_seq, 8, dconv), F32)],
        scratch_shapes=[pltpu.VMEM((DN_HEADS, DN_HK, DN_HV), F32), pltpu.VMEM((rb + 8, dconv), F32)],
        input_output_aliases=aliases,
        compiler_params=_cparams("parallel", "arbitrary"),
        name="dn_core",
    )(*args)


def _dn_layer(x, nw, s0_s, conv0_s, w_in, conv_w, a_log, dt_bias, norm_w, w_out, dims):
    n_p, t_p, n_s, t_s = dims
    t = x.shape[0]
    tm = _tile(t, 1280)
    dconv = conv_w.shape[1]
    dz = DN_HEADS * DN_HV
    width = dconv + dz + LANES
    proj = _linear(x, _pad_cols(w_in, width), norm_w=nw, tm=tm, tn=_tile(width, 1408), name="dn_in")
    o, sp, cp = _dn_core(proj, None, None, conv_w, a_log, dt_bias, norm_w, None, n_seq=n_p, seq_len=t_p,
                         row0=0, rb=min(t_p, 128))
    conv0 = jnp.pad(conv0_s, ((0, 0), (8 - conv0_s.shape[1], 0), (0, 0)))
    o, ss, cs = _dn_core(proj, s0_s, conv0, conv_w, a_log, dt_bias, norm_w, o, n_seq=n_s, seq_len=t_s,
                         row0=n_p * t_p, rb=t_s)
    x = _linear(o, w_out, res=x, tm=tm, tn=512, name="dn_out")
    keep = DN_CONV - 1
    return x, sp, ss, cp[:, 8 - keep:], cs[:, 8 - keep:]


def kernel(x_prompt, x_sample, state_gla, cache_mla_latent, cache_mla_krope, cache_swa_k, cache_swa_v, state_delta, state_delta_conv, page_table, norm_w, final_norm_w, gla_w_in, gla_w_gk2, gla_b_gk2, gla_norm_w, gla_w_out, mla_w_in, mla_q_norm_w, mla_w_uq, mla_kv_norm_w, mla_w_uk, mla_w_uv, mla_w_out, swa_w_qkv, swa_b_qkv, swa_sinks, swa_w_out, swa_b_out, dn_w_in, dn_conv_w, dn_a_log, dn_dt_bias, dn_norm_w, dn_w_out, ffn_w_gate, ffn_w_up, ffn_w_down, moe_w_router, moe_w_gate, moe_w_up, moe_w_down):
    n_p, t_p, d = x_prompt.shape
    n_s, t_s, _ = x_sample.shape
    dims = (n_p, t_p, n_s, t_s)
    x = jnp.concatenate([x_prompt.reshape(n_p * t_p, d), x_sample.reshape(n_s * t_s, d)], axis=0)
    t = x.shape[0]
    tm = _tile(t, 1280)
    n_tp = n_p * t_p

    x, gla_p, gla_s = _gla_layer(x, norm_w[0, 0], state_gla[0], gla_w_in[0], gla_w_gk2[0], gla_b_gk2[0],
                                 gla_norm_w[0], gla_w_out[0], dims)
    x = _ffn(x, norm_w[0, 1], ffn_w_gate, ffn_w_up, ffn_w_down, 0, tm=tm, tf=512)
    moe_tiles = dict(tm=_tile(t, 640), tp=1280, tf=512)

    x, ckv, kr = _mla_layer(x, norm_w[1, 0], cache_mla_latent[0:1], cache_mla_krope[0:1], page_table,
                            mla_w_in[0], mla_q_norm_w[0], mla_w_uq[0], mla_kv_norm_w[0], mla_w_uk[0],
                            mla_w_uv[0], mla_w_out[0], dims)
    x = _moe(x, norm_w[1, 1], moe_w_router[0], moe_w_gate, moe_w_up, moe_w_down, 0, **moe_tiles)

    x, swk_p, swv_p, swk_s, swv_s = _swa_layer(x, norm_w[2, 0], cache_swa_k[0], cache_swa_v[0],
                                               swa_w_qkv[0], swa_b_qkv[0], swa_sinks[0], swa_w_out[0],
                                               swa_b_out[0], dims)
    x = _ffn(x, norm_w[2, 1], ffn_w_gate, ffn_w_up, ffn_w_down, 1, tm=tm, tf=512)

    x, dn_p, dn_s, cv_p, cv_s = _dn_layer(x, norm_w[3, 0], state_delta[0], state_delta_conv[0], dn_w_in[0],
                                          dn_conv_w[0], dn_a_log[0], dn_dt_bias[0], dn_norm_w[0],
                                          dn_w_out[0], dims)
    y = _moe(x, norm_w[3, 1], moe_w_router[1], moe_w_gate, moe_w_up, moe_w_down, 1, final_norm_w,
             **moe_tiles)

    lead = lambda a: a[None]
    return (y[:n_tp].reshape(n_p, t_p, d), y[n_tp:].reshape(n_s, t_s, d),
            lead(gla_p), lead(gla_s),
            lead(ckv[:n_tp].reshape(n_p, t_p, -1)), lead(ckv[n_tp:].reshape(n_s, t_s, -1)),
            lead(kr[:n_tp].reshape(n_p, t_p, -1)), lead(kr[n_tp:].reshape(n_s, t_s, -1)),
            lead(swk_p), lead(swk_s), lead(swv_p), lead(swv_s),
            lead(dn_p), lead(dn_s), lead(cv_p), lead(cv_s))
```

```python
import functools
import math

import jax
import jax.numpy as jnp
import numpy as np
from jax import lax
from jax.experimental import pallas as pl
from jax.experimental.pallas import tpu as pltpu

F32 = jnp.float32
BF16 = jnp.bfloat16

NORM_EPS = 1e-6
GLA_HEADS = 4
GLA_GATE_RANK = 16
GLA_GATE_NORMALIZER = 16.0
CHUNK = 64
MLA_HEADS = 16
MLA_Q_LORA = 384
MLA_KV_LORA = 256
MLA_NOPE = 64
MLA_ROPE = 32
MLA_V = 64
ROPE_THETA = 10000.0
SWA_HEADS = 16
SWA_KV_HEADS = 4
SWA_HD = 64
WINDOW = 128
DN_HEADS = 8
DN_HK = 128
DN_HV = 128
DN_CONV = 4
N_EXPERTS = 8

LANES = 128
VMEM_LIMIT = 56 * 1024 * 1024
NEG_INF = float("-inf")


def _cparams(*sem):
    return pltpu.CompilerParams(dimension_semantics=sem, vmem_limit_bytes=VMEM_LIMIT)


def _bdot(a, b):
    return jnp.dot(a.astype(BF16), b.astype(BF16), preferred_element_type=F32)


def _bdot_nt(a, b):
    return lax.dot_general(a.astype(BF16), b.astype(BF16), (((1,), (1,)), ((), ())),
                           preferred_element_type=F32)


def _bdot_tn(a, b):
    return lax.dot_general(a.astype(BF16), b.astype(BF16), (((0,), (0,)), ((), ())),
                           preferred_element_type=F32)


def _split3(x):
    h1 = x.astype(BF16)
    r1 = x - h1.astype(F32)
    h2 = r1.astype(BF16)
    h3 = (r1 - h2.astype(F32)).astype(BF16)
    return h1, h2, h3


def _dot_exact_lhs(m, x):
    mb = m.astype(BF16)
    h1, h2, h3 = _split3(x)
    return (jnp.dot(mb, h1, preferred_element_type=F32) + jnp.dot(mb, h2, preferred_element_type=F32)
            + jnp.dot(mb, h3, preferred_element_type=F32))


def _rms(x, w):
    return x * lax.rsqrt(jnp.mean(x * x, axis=-1, keepdims=True) + NORM_EPS) * w


def _silu(x):
    return x / (1.0 + jnp.exp(-x))


def _log_sigmoid(x):
    return jnp.minimum(x, 0.0) - jnp.log(1.0 + jnp.exp(-jnp.abs(x)))


def _softplus(x):
    return jnp.maximum(x, 0.0) + jnp.log(1.0 + jnp.exp(-jnp.abs(x)))


def _linear_kernel(*refs, has_norm, has_bias, has_res):
    it = iter(refs)
    x_ref = next(it)
    nw_ref = next(it) if has_norm else None
    w_ref = next(it)
    b_ref = next(it) if has_bias else None
    r_ref = next(it) if has_res else None
    o_ref = next(it)
    h_ref = next(it)

    @pl.when(pl.program_id(1) == 0)
    def _():
        xv = x_ref[...].astype(F32)
        if has_norm:
            xv = _rms(xv, nw_ref[...])
        h_ref[...] = xv.astype(BF16)

    acc = jnp.dot(h_ref[...], w_ref[...].astype(BF16), preferred_element_type=F32)
    if has_bias:
        acc = acc + b_ref[...]
    if has_res:
        acc = acc + r_ref[...]
    o_ref[...] = acc.astype(o_ref.dtype)


def _linear(x, w, *, norm_w=None, bias=None, res=None, tm, tn, out_dtype=F32, name="linear"):
    t, k = x.shape
    n = w.shape[1]
    assert t % tm == 0 and n % tn == 0, (t, tm, n, tn)
    in_specs = [pl.BlockSpec((tm, k), lambda i, j: (i, 0))]
    args = [x]
    if norm_w is not None:
        in_specs.append(pl.BlockSpec((1, k), lambda i, j: (0, 0)))
        args.append(norm_w.reshape(1, k))
    in_specs.append(pl.BlockSpec((k, tn), lambda i, j: (0, j)))
    args.append(w)
    if bias is not None:
        in_specs.append(pl.BlockSpec((1, tn), lambda i, j: (0, j)))
        args.append(bias.reshape(1, n))
    if res is not None:
        in_specs.append(pl.BlockSpec((tm, tn), lambda i, j: (i, j)))
        args.append(res)
    kern = functools.partial(_linear_kernel, has_norm=norm_w is not None, has_bias=bias is not None,
                             has_res=res is not None)
    return pl.pallas_call(
        kern,
        grid=(t // tm, n // tn),
        in_specs=in_specs,
        out_specs=pl.BlockSpec((tm, tn), lambda i, j: (i, j)),
        out_shape=jax.ShapeDtypeStruct((t, n), out_dtype),
        scratch_shapes=[pltpu.VMEM((tm, k), BF16)],
        compiler_params=_cparams("parallel", "arbitrary"),
        name=name,
    )(*args)


def _swiglu_acc(h_ref, wg_ref, wu_ref, wd_ref, acc_ref):
    h = h_ref[...].astype(BF16)
    g = jnp.dot(h, wg_ref[...].astype(BF16), preferred_element_type=F32)
    u = jnp.dot(h, wu_ref[...].astype(BF16), preferred_element_type=F32)
    a = (_silu(g) * u).astype(BF16)
    acc_ref[...] += jnp.dot(a, wd_ref[...].astype(BF16), preferred_element_type=F32)


def _ffn_kernel(x_ref, nw_ref, wg_ref, wu_ref, wd_ref, o_ref, h_ref, acc_ref):
    f = pl.program_id(1)

    @pl.when(f == 0)
    def _():
        h_ref[...] = _rms(x_ref[...], nw_ref[...]).astype(BF16)
        acc_ref[...] = jnp.zeros_like(acc_ref)

    _swiglu_acc(h_ref, wg_ref, wu_ref, wd_ref, acc_ref)

    @pl.when(f == pl.num_programs(1) - 1)
    def _():
        o_ref[...] = x_ref[...] + acc_ref[...]


def _ffn(x, norm_w, wg, wu, wd, layer, *, tm, tf):
    t, d = x.shape
    ff = wg.shape[2]
    assert t % tm == 0 and ff % tf == 0
    return pl.pallas_call(
        _ffn_kernel,
        grid=(t // tm, ff // tf),
        in_specs=[pl.BlockSpec((tm, d), lambda i, f: (i, 0)),
                  pl.BlockSpec((1, d), lambda i, f: (0, 0)),
                  pl.BlockSpec((None, d, tf), lambda i, f: (layer, 0, f)),
                  pl.BlockSpec((None, d, tf), lambda i, f: (layer, 0, f)),
                  pl.BlockSpec((None, tf, d), lambda i, f: (layer, f, 0))],
        out_specs=pl.BlockSpec((tm, d), lambda i, f: (i, 0)),
        out_shape=jax.ShapeDtypeStruct((t, d), F32),
        scratch_shapes=[pltpu.VMEM((tm, d), BF16), pltpu.VMEM((tm, d), F32)],
        compiler_params=_cparams("parallel", "arbitrary"),
        name="ffn",
    )(x, norm_w.reshape(1, d), wg, wu, wd)


MOE_CHUNK = 256
MOE_ALIGN = 16
MOE_SLOTS = 4


def _route_kernel(x_ref, nw_ref, wrt_ref, h_ref, sel_ref, gate_ref, rank_ref, cnt_ref):
    hn = _rms(x_ref[...], nw_ref[...])
    h_ref[...] = hn.astype(BF16)
    logits = lax.dot_general(wrt_ref[...], hn, (((1,), (1,)), ((), ())), preferred_element_type=F32,
                             precision=lax.Precision.HIGHEST)
    n_exp, tm = logits.shape
    sub = lax.broadcasted_iota(jnp.int32, logits.shape, 0)
    m1 = jnp.max(logits, axis=0, keepdims=True)
    i1 = jnp.min(jnp.where(logits == m1, sub, n_exp), axis=0, keepdims=True)
    rest = jnp.where(sub == i1, NEG_INF, logits)
    m2 = jnp.max(rest, axis=0, keepdims=True)
    i2 = jnp.min(jnp.where(rest == m2, sub, n_exp), axis=0, keepdims=True)
    e2 = jnp.exp(m2 - m1)
    first, second = sub == i1, sub == i2
    sel = jnp.where(first | second, 1.0, 0.0)
    upper = jnp.where(lax.broadcasted_iota(jnp.int32, (tm, tm), 0)
                      <= lax.broadcasted_iota(jnp.int32, (tm, tm), 1), 1.0, 0.0)
    cum = _bdot(sel, upper)
    sel_ref[...] = sel
    gate_ref[...] = jnp.where(first, 1.0 / (1.0 + e2), 0.0) + jnp.where(second, e2 / (1.0 + e2), 0.0)
    rank_ref[...] = cum - sel
    cnt_ref[0] = jnp.broadcast_to(cum[:, tm - 1:tm], (n_exp, LANES))


def _route(x, norm_w, w_router, *, tm):
    t, d = x.shape
    n_exp = w_router.shape[1]
    et = lambda i: (0, i)
    return pl.pallas_call(
        _route_kernel,
        grid=(t // tm,),
        in_specs=[pl.BlockSpec((tm, d), lambda i: (i, 0)), pl.BlockSpec((1, d), lambda i: (0, 0)),
                  pl.BlockSpec((n_exp, d), lambda i: (0, 0))],
        out_specs=[pl.BlockSpec((tm, d), lambda i: (i, 0)), pl.BlockSpec((n_exp, tm), et),
                   pl.BlockSpec((n_exp, tm), et), pl.BlockSpec((n_exp, tm), et),
                   pl.BlockSpec((1, n_exp, LANES), lambda i: (i, 0, 0))],
        out_shape=[jax.ShapeDtypeStruct((t, d), BF16), jax.ShapeDtypeStruct((n_exp, t), F32),
                   jax.ShapeDtypeStruct((n_exp, t), F32), jax.ShapeDtypeStruct((n_exp, t), F32),
                   jax.ShapeDtypeStruct((t // tm, n_exp, LANES), F32)],
        compiler_params=_cparams("parallel"),
        name="moe_route",
    )(x, norm_w.reshape(1, d), w_router.T)


def _moe_plan(cnt, *, tm, tp, p_rows):
    n_tiles, n_exp = cnt.shape
    seg = (cnt + MOE_ALIGN - 1) // MOE_ALIGN * MOE_ALIGN
    total = jnp.sum(seg, axis=0)
    in_group = jnp.cumsum(seg, axis=0) - seg
    nch = (cnt + MOE_CHUNK - 1) // MOE_CHUNK
    reach = jnp.max(in_group + nch * MOE_CHUNK, axis=0)
    gsize = (jnp.maximum(total, reach) + tp - 1) // tp * tp
    gstart = jnp.cumsum(gsize) - gsize
    seg_start = gstart[None, :] + in_group
    cum_e = jnp.cumsum(nch, axis=1)
    q_max = n_exp + 2 * tm // MOE_CHUNK
    q = jnp.arange(q_max, dtype=jnp.int32)[None, :]
    flat_e = jnp.minimum(jnp.sum(q[:, :, None] >= cum_e[:, None, :], axis=-1), n_exp - 1).astype(jnp.int32)
    flat_c = q - jnp.take_along_axis(cum_e - nch, flat_e, axis=1)
    flat_row = jnp.take_along_axis(seg_start, flat_e, axis=1) + flat_c * MOE_CHUNK
    row_j = jnp.arange(p_rows // tp, dtype=jnp.int32) * tp
    tile_e = jnp.minimum(jnp.sum(row_j[:, None] >= (gstart + gsize)[None, :], axis=-1), n_exp - 1)
    tile_valid = row_j < jnp.take(gstart + total, tile_e)
    i32 = lambda a: a.astype(jnp.int32)
    return (i32(cum_e[:, -1]), i32(flat_e.reshape(-1)), i32(flat_c.reshape(-1)), i32(flat_row.reshape(-1)),
            i32(tile_e), i32(tile_valid))


def _dispatch_kernel(nq_ref, fe_ref, fc_ref, fr_ref, h_ref, sel_ref, rank_ref, xs_in, xs_ref, stage, sem,
                     done_ref, *, q_max):
    del xs_in
    i = pl.program_id(0)
    n = nq_ref[i]
    h = h_ref[...]

    @pl.when(i == 0)
    def _():
        done_ref[0] = 0

    k0 = done_ref[0]

    def chunk_copy(slot, row):
        return pltpu.make_async_copy(stage.at[slot], xs_ref.at[pl.ds(pl.multiple_of(row, MOE_ALIGN), MOE_CHUNK)],
                                     sem.at[slot])

    def body(q, carry):
        slot = (k0 + q) % MOE_SLOTS
        e = fe_ref[i * q_max + q]
        c = fc_ref[i * q_max + q]

        @pl.when(k0 + q >= MOE_SLOTS)
        def _():
            chunk_copy(slot, 0).wait()

        pos = jnp.where(sel_ref[pl.ds(e, 1), :] > 0.0, rank_ref[pl.ds(e, 1), :], -1.0)
        want = (c * MOE_CHUNK + lax.broadcasted_iota(jnp.int32, (MOE_CHUNK, 1), 0)).astype(F32)
        pick = jnp.where(pos == want, 1.0, 0.0).astype(BF16)
        stage[slot] = jnp.dot(pick, h, preferred_element_type=F32)
        chunk_copy(slot, fr_ref[i * q_max + q]).start()
        return carry

    lax.fori_loop(0, n, body, 0)
    done_ref[0] = k0 + n
    for s in range(MOE_SLOTS):
        @pl.when((i == pl.num_programs(0) - 1) & (k0 + n > s))
        def _():
            chunk_copy(s, 0).wait()


def _dispatch(h, sel, rank, plan, *, tm, p_rows):
    t, d = h.shape
    n_exp = sel.shape[0]
    n_flat, flat_e, flat_c, flat_row = plan[:4]
    q_max = flat_e.shape[0] // (t // tm)
    et = lambda i, *_: (0, i)
    return pl.pallas_call(
        functools.partial(_dispatch_kernel, q_max=q_max),
        grid_spec=pltpu.PrefetchScalarGridSpec(
            num_scalar_prefetch=4,
            grid=(t // tm,),
            in_specs=[pl.BlockSpec((tm, d), lambda i, *_: (i, 0)), pl.BlockSpec((n_exp, tm), et),
                      pl.BlockSpec((n_exp, tm), et), pl.BlockSpec(memory_space=pl.ANY)],
            out_specs=pl.BlockSpec(memory_space=pl.ANY),
            scratch_shapes=[pltpu.VMEM((MOE_SLOTS, MOE_CHUNK, d), F32),
                            pltpu.SemaphoreType.DMA((MOE_SLOTS,)), pltpu.SMEM((1,), jnp.int32)]),
        out_shape=jax.ShapeDtypeStruct((p_rows, d), F32),
        input_output_aliases={7: 0},
        compiler_params=_cparams("arbitrary"),
        name="moe_dispatch",
    )(n_flat, flat_e, flat_c, flat_row, h, sel, rank, jnp.zeros((p_rows, d), F32))


def _gffn_kernel(te_ref, tv_ref, x_ref, wg_ref, wu_ref, wd_ref, o_ref, acc_ref):
    del te_ref
    j = pl.program_id(0)
    f = pl.program_id(1)
    valid = tv_ref[j] > 0

    @pl.when(f == 0)
    def _():
        acc_ref[...] = jnp.zeros_like(acc_ref)

    @pl.when(valid)
    def _():
        _swiglu_acc(x_ref, wg_ref, wu_ref, wd_ref, acc_ref)

    @pl.when(f == pl.num_programs(1) - 1)
    def _():
        o_ref[...] = acc_ref[...].astype(o_ref.dtype)


def _gffn(xs, wg, wu, wd, layer, tile_e, tile_valid, *, tp, tf):
    p_rows, d = xs.shape
    ff = wg.shape[3]

    def wmap(is_down):
        def index(j, f, te, tv):
            fi = jnp.where(tv[j] > 0, f, 0)
            return (layer, te[j], fi, 0) if is_down else (layer, te[j], 0, fi)
        return index

    return pl.pallas_call(
        _gffn_kernel,
        grid_spec=pltpu.PrefetchScalarGridSpec(
            num_scalar_prefetch=2,
            grid=(p_rows // tp, ff // tf),
            in_specs=[pl.BlockSpec((tp, d), lambda j, f, te, tv: (j, 0)),
                      pl.BlockSpec((None, None, d, tf), wmap(False)),
                      pl.BlockSpec((None, None, d, tf), wmap(False)),
                      pl.BlockSpec((None, None, tf, d), wmap(True))],
            out_specs=pl.BlockSpec((tp, d), lambda j, f, te, tv: (j, 0)),
            scratch_shapes=[pltpu.VMEM((tp, d), F32)]),
        out_shape=jax.ShapeDtypeStruct((p_rows, d), BF16),
        compiler_params=_cparams("parallel", "arbitrary"),
        name="moe_ffn",
    )(tile_e, tile_valid, xs, wg, wu, wd)


def _combine_kernel(*refs, q_max, final_norm):
    nq_ref, fe_ref, fc_ref, fr_ref = refs[:4]
    x_ref, pos_ref, gate_ref = refs[4:7]
    fw_ref = refs[7] if final_norm else None
    ys_ref, o_ref, acc_ref, buf, sem, done_ref = refs[7 + int(final_norm):]
    i = pl.program_id(0)
    n = nq_ref[i]
    acc_ref[...] = jnp.zeros_like(acc_ref)
    lane = lax.broadcasted_iota(jnp.int32, pos_ref.shape, 1)

    def chunk_copy(slot, row):
        return pltpu.make_async_copy(ys_ref.at[pl.ds(pl.multiple_of(row, MOE_ALIGN), MOE_CHUNK)], buf.at[slot],
                                     sem.at[slot])

    @pl.when(i == 0)
    def _():
        done_ref[0] = 0
        chunk_copy(0, fr_ref[0]).start()

    k0 = done_ref[0]

    def body(q, carry):
        slot = (k0 + q) % 2
        e = fe_ref[i * q_max + q]
        c = fc_ref[i * q_max + q]

        @pl.when(q + 1 < n)
        def _():
            chunk_copy(1 - slot, fr_ref[i * q_max + q + 1]).start()

        @pl.when((q + 1 == n) & (i + 1 < pl.num_programs(0)))
        def _():
            chunk_copy(1 - slot, fr_ref[(i + 1) * q_max]).start()

        chunk_copy(slot, 0).wait()
        pos = jnp.sum(jnp.where(lane == e, pos_ref[...], 0.0), axis=1, keepdims=True)
        gate = jnp.sum(jnp.where(lane == e, gate_ref[...], 0.0), axis=1, keepdims=True)
        want = (c * MOE_CHUNK + lax.broadcasted_iota(jnp.int32, (1, MOE_CHUNK), 1)).astype(F32)
        pick = jnp.where(pos == want, 1.0, 0.0).astype(BF16)
        acc_ref[...] += gate * jnp.dot(pick, buf[slot], preferred_element_type=F32)
        return carry

    lax.fori_loop(0, n, body, 0)
    done_ref[0] = k0 + n
    y = x_ref[...] + acc_ref[...]
    if final_norm:
        y = _rms(y, fw_ref[...])
    o_ref[...] = y


def _combine(x, ys, pos_tok, gate_tok, plan, final_w, *, tm):
    t, d = x.shape
    n_exp = pos_tok.shape[1]
    n_flat, flat_e, flat_c, flat_row = plan[:4]
    q_max = flat_e.shape[0] // (t // tm)
    row = lambda i, *_: (i, 0)
    in_specs = [pl.BlockSpec((tm, d), row), pl.BlockSpec((tm, n_exp), row), pl.BlockSpec((tm, n_exp), row)]
    args = [x, pos_tok, gate_tok]
    if final_w is not None:
        in_specs.append(pl.BlockSpec((1, d), lambda i, *_: (0, 0)))
        args.append(final_w.reshape(1, d))
    in_specs.append(pl.BlockSpec(memory_space=pl.ANY))
    args.append(ys)
    return pl.pallas_call(
        functools.partial(_combine_kernel, q_max=q_max, final_norm=final_w is not None),
        grid_spec=pltpu.PrefetchScalarGridSpec(
            num_scalar_prefetch=4,
            grid=(t // tm,),
            in_specs=in_specs,
            out_specs=pl.BlockSpec((tm, d), row),
            scratch_shapes=[pltpu.VMEM((tm, d), F32), pltpu.VMEM((2, MOE_CHUNK, d), BF16),
                            pltpu.SemaphoreType.DMA((2,)), pltpu.SMEM((1,), jnp.int32)]),
        out_shape=jax.ShapeDtypeStruct((t, d), F32),
        compiler_params=_cparams("arbitrary"),
        name="moe_combine",
    )(n_flat, flat_e, flat_c, flat_row, *args)


def _moe(x, norm_w, w_router, wg, wu, wd, layer, final_w=None, *, tm, tp, tf):
    t, d = x.shape
    n_exp = w_router.shape[1]
    n_tiles = t // tm
    worst = 2 * t + n_tiles * n_exp * (MOE_ALIGN - 1) + n_exp * (MOE_CHUNK + tp - 1)
    p_rows = (worst + tp - 1) // tp * tp
    h, sel, gate, rank, cnt = _route(x, norm_w, w_router, tm=tm)
    plan = _moe_plan(cnt[:, :, 0].astype(jnp.int32), tm=tm, tp=tp, p_rows=p_rows)
    xs = _dispatch(h, sel, rank, plan, tm=tm, p_rows=p_rows)
    ys = _gffn(xs, wg, wu, wd, layer, plan[4], plan[5], tp=tp, tf=tf)
    pos_tok = jnp.where(sel > 0.0, rank, -1.0).T
    return _combine(x, ys, pos_tok, gate.T, plan, final_w, tm=tm)


def _pad_rows(x, rows):
    if x.shape[0] == rows:
        return x
    return jnp.concatenate([x, jnp.zeros((rows - x.shape[0], x.shape[1]), x.dtype)], axis=0)


def _chunk_masks(rows):
    ri = lax.broadcasted_iota(jnp.int32, (rows, rows), 0)
    ci = lax.broadcasted_iota(jnp.int32, (rows, rows), 1)
    same = (ri // CHUNK) == (ci // CHUNK)
    return same & (ci <= ri), same


def _gla_kernel(*refs, rb, zero_init, has_alias):
    it = iter(refs)
    q_ref, k_ref, v_ref, g_ref, gk_ref = next(it), next(it), next(it), next(it), next(it)
    s0_ref = None if zero_init else next(it)
    wgk_ref, bgk_ref, nw_ref = next(it), next(it), next(it)
    if has_alias:
        next(it)
    o_ref, s_ref, st_ref = next(it), next(it), next(it)
    r = pl.program_id(1)
    rows = max(rb, CHUNK)
    n_chunks = rows // CHUNK
    hk = q_ref.shape[1] // GLA_HEADS
    hv = v_ref.shape[1] // GLA_HEADS

    @pl.when(r == 0)
    def _():
        for h in range(GLA_HEADS):
            if zero_init:
                st_ref[h] = jnp.zeros(st_ref.shape[1:], F32)
            else:
                st_ref[h] = s0_ref[0, h].T

    q = _pad_rows(q_ref[...] * hk ** -0.5, rows)
    k = _pad_rows(k_ref[...], rows)
    v = _pad_rows(v_ref[...], rows)
    la = _log_sigmoid(_bdot(gk_ref[...], wgk_ref[...]) + bgk_ref[...]) * (1.0 / GLA_GATE_NORMALIZER)
    la = _pad_rows(la, rows)
    causal, same = _chunk_masks(rows)
    gc = _dot_exact_lhs(causal, la)
    gt = _dot_exact_lhs(same, la)
    q_in = q * jnp.exp(gc)
    k_in = k * jnp.exp(-gc)
    k_out = k * jnp.exp(gt - gc)
    e_tot = jnp.exp(gt)
    tri = causal[:CHUNK, :CHUNK]
    for h in range(GLA_HEADS):
        st = st_ref[h]
        ks = slice(h * hk, (h + 1) * hk)
        vs = slice(h * hv, (h + 1) * hv)
        for c in range(n_chunks):
            rs = slice(c * CHUNK, (c + 1) * CHUNK)
            qi, ki, ko, vh = q_in[rs, ks], k_in[rs, ks], k_out[rs, ks], v[rs, vs]
            intra = jnp.where(tri, _bdot_nt(qi, ki), 0.0)
            o = _bdot(intra, vh) + _bdot_nt(qi, st)
            st = st * e_tot[c * CHUNK:c * CHUNK + 1, ks] + _bdot_tn(vh, ko)
            n_out = min(rb, CHUNK)
            og = _rms(o[:n_out], nw_ref[...]) * _silu(g_ref[c * CHUNK:c * CHUNK + n_out, vs])
            o_ref[c * CHUNK:c * CHUNK + n_out, vs] = og
        st_ref[h] = st

    @pl.when(r == pl.num_programs(1) - 1)
    def _():
        for h in range(GLA_HEADS):
            s_ref[0, h] = st_ref[h].T


def _gla_core(proj, s0, w_gk2, b_gk2, norm_w, o_full, *, n_seq, seq_len, row0, rb):
    t = proj.shape[0]
    dk = w_gk2.shape[1]
    dv = 2 * dk
    hk, hv = dk // GLA_HEADS, dv // GLA_HEADS
    nblk = seq_len // rb
    base = row0 // rb
    assert row0 % rb == 0 and seq_len % rb == 0

    def rowmap(col):
        return lambda b, r: (base + b * nblk + r, col)

    in_specs = [pl.BlockSpec((rb, dk), rowmap(0)), pl.BlockSpec((rb, dk), rowmap(1)),
                pl.BlockSpec((rb, dv), rowmap(1)), pl.BlockSpec((rb, dv), rowmap(2)),
                pl.BlockSpec((rb, LANES), rowmap((2 * dk + 2 * dv) // LANES))]
    args = [proj, proj, proj, proj, proj]
    if s0 is not None:
        in_specs.append(pl.BlockSpec((1, GLA_HEADS, hk, hv), lambda b, r: (b, 0, 0, 0)))
        args.append(s0)
    wgk = jnp.pad(w_gk2, ((0, LANES - w_gk2.shape[0]), (0, 0)))
    in_specs += [pl.BlockSpec((LANES, dk), lambda b, r: (0, 0)),
                 pl.BlockSpec((1, dk), lambda b, r: (0, 0)),
                 pl.BlockSpec((1, hv), lambda b, r: (0, 0))]
    args += [wgk, b_gk2.reshape(1, dk), norm_w.reshape(1, hv)]
    aliases = {}
    if o_full is not None:
        in_specs.append(pl.BlockSpec(memory_space=pl.ANY))
        aliases = {len(args): 0}
        args.append(o_full)
    kern = functools.partial(_gla_kernel, rb=rb, zero_init=s0 is None, has_alias=o_full is not None)
    return pl.pallas_call(
        kern,
        grid=(n_seq, nblk),
        in_specs=in_specs,
        out_specs=[pl.BlockSpec((rb, dv), rowmap(0)),
                   pl.BlockSpec((1, GLA_HEADS, hk, hv), lambda b, r: (b, 0, 0, 0))],
        out_shape=[jax.ShapeDtypeStruct((t, dv), F32),
                   jax.ShapeDtypeStruct((n_seq, GLA_HEADS, hk, hv), F32)],
        scratch_shapes=[pltpu.VMEM((GLA_HEADS, hv, hk), F32)],
        input_output_aliases=aliases,
        compiler_params=_cparams("parallel", "arbitrary"),
        name="gla_core",
    )(*args)


def _tile(n, pref):
    if n <= pref:
        return n
    for c in range(pref, 7, -8):
        if n % c == 0:
            return c
    return n


def _pad_cols(w, n):
    return jnp.pad(w, ((0, 0), (0, n - w.shape[1])))


def _gla_layer(x, nw, s0_s, w_in, w_gk2, b_gk2, norm_w, w_out, dims):
    n_p, t_p, n_s, t_s = dims
    t = x.shape[0]
    tm = _tile(t, 1280)
    dk = w_gk2.shape[1]
    width = 6 * dk + LANES
    proj = _linear(x, _pad_cols(w_in, width), norm_w=nw, tm=tm, tn=_tile(width, 640), name="gla_in")
    o, sp = _gla_core(proj, None, w_gk2, b_gk2, norm_w, None, n_seq=n_p, seq_len=t_p, row0=0,
                      rb=min(t_p, 256))
    o, ss = _gla_core(proj, s0_s, w_gk2, b_gk2, norm_w, o, n_seq=n_s, seq_len=t_s, row0=n_p * t_p,
                      rb=t_s)
    x = _linear(o, w_out, res=x, tm=tm, tn=512, name="gla_out")
    return x, sp, ss


def _rope_tables(pos, half):
    freqs = np.exp(-math.log(ROPE_THETA) * np.arange(half, dtype=np.float64) / half)
    ang = np.asarray(pos, np.float64)[:, None] * freqs[None, :]
    cos, sin = np.cos(ang), np.sin(ang)
    return (jnp.asarray(np.concatenate([cos, cos], axis=-1), F32),
            jnp.asarray(np.concatenate([-sin, sin], axis=-1), F32))


def _swap_halves(w, axis=-1):
    a, b = jnp.split(w, 2, axis=axis)
    return jnp.concatenate([b, a], axis=axis)


def _mla_in_kernel(x_ref, nw_ref, w_ref, kvw_ref, cos_ref, sin_ref, cq_ref, ckv_ref, kr_ref):
    h = _rms(x_ref[...], nw_ref[...])
    y = _bdot(h, w_ref[...])
    cq_ref[...] = y[:, :MLA_Q_LORA]
    ckv_ref[...] = _rms(y[:, MLA_Q_LORA:MLA_Q_LORA + MLA_KV_LORA], kvw_ref[...])
    o = MLA_Q_LORA + MLA_KV_LORA
    kr_ref[...] = (y[:, o:o + MLA_ROPE] * cos_ref[...]
                   + y[:, o + LANES:o + LANES + MLA_ROPE] * sin_ref[...])


def _mla_in(x, nw, w_in, kv_norm_w, cos, sin, *, tm):
    t, d = x.shape
    o = MLA_Q_LORA + MLA_KV_LORA
    kr_w = w_in[:, o:o + MLA_ROPE]
    w_aug = jnp.concatenate([w_in[:, :o], _pad_cols(kr_w, LANES), _pad_cols(_swap_halves(kr_w), LANES)],
                            axis=1)
    wid = w_aug.shape[1]
    row = lambda i: (i, 0)
    fix = lambda i: (0, 0)
    return pl.pallas_call(
        _mla_in_kernel,
        grid=(t // tm,),
        in_specs=[pl.BlockSpec((tm, d), row), pl.BlockSpec((1, d), fix), pl.BlockSpec((d, wid), fix),
                  pl.BlockSpec((1, MLA_KV_LORA), fix), pl.BlockSpec((tm, MLA_ROPE), row),
                  pl.BlockSpec((tm, MLA_ROPE), row)],
        out_specs=[pl.BlockSpec((tm, MLA_Q_LORA), row), pl.BlockSpec((tm, MLA_KV_LORA), row),
                   pl.BlockSpec((tm, MLA_ROPE), row)],
        out_shape=[jax.ShapeDtypeStruct((t, MLA_Q_LORA), F32), jax.ShapeDtypeStruct((t, MLA_KV_LORA), F32),
                   jax.ShapeDtypeStruct((t, MLA_ROPE), F32)],
        compiler_params=_cparams("parallel"),
        name="mla_in",
    )(x, nw.reshape(1, d), w_aug, kv_norm_w.reshape(1, -1), cos, sin)


def _mla_q_kernel(cq_ref, qw_ref, wn_ref, wr_ref, ws_ref, wk_ref, cos_ref, sin_ref, ql_ref, qr_ref):
    cq = _rms(cq_ref[...], qw_ref[...]).astype(BF16)
    qn = jnp.dot(cq, wn_ref[...].astype(BF16), preferred_element_type=F32).astype(BF16)
    for j in range(MLA_HEADS // 2):
        ql = jnp.dot(qn[:, j * LANES:(j + 1) * LANES], wk_ref[j].astype(BF16), preferred_element_type=F32)
        ql_ref[2 * j] = ql[:, :MLA_KV_LORA]
        ql_ref[2 * j + 1] = ql[:, MLA_KV_LORA:]
    qr = jnp.dot(cq, wr_ref[...].astype(BF16), preferred_element_type=F32)
    qs = jnp.dot(cq, ws_ref[...].astype(BF16), preferred_element_type=F32)
    per = LANES // MLA_ROPE
    cos = jnp.concatenate([cos_ref[...]] * (MLA_HEADS // per), axis=-1)
    sin = jnp.concatenate([sin_ref[...]] * (MLA_HEADS // per), axis=-1)
    rot = qr * cos + qs * sin
    for h in range(MLA_HEADS):
        qr_ref[h] = rot[:, h * MLA_ROPE:(h + 1) * MLA_ROPE]


def _mla_q(cq, q_norm_w, w_uq, w_uk, cos, sin, *, tm):
    t = cq.shape[0]
    per = LANES // MLA_ROPE
    w3 = w_uq.reshape(MLA_Q_LORA, MLA_HEADS, MLA_NOPE + MLA_ROPE)
    w_nope = w3[:, :, :MLA_NOPE].reshape(MLA_Q_LORA, MLA_HEADS * MLA_NOPE)
    w_rope = w3[:, :, MLA_NOPE:].reshape(MLA_Q_LORA, MLA_HEADS * MLA_ROPE)
    w_swap = _swap_halves(w3[:, :, MLA_NOPE:]).reshape(MLA_Q_LORA, MLA_HEADS * MLA_ROPE)
    a = jnp.transpose(w_uk, (1, 2, 0))
    z = jnp.zeros_like(a[0::2])
    w_bd = jnp.concatenate([jnp.concatenate([a[0::2], z], axis=2),
                            jnp.concatenate([z, a[1::2]], axis=2)], axis=1)
    cos4 = jnp.tile(cos, (1, per))
    sin4 = jnp.tile(sin, (1, per))
    row = lambda i: (i, 0)
    fix2 = lambda i: (0, 0)
    return pl.pallas_call(
        _mla_q_kernel,
        grid=(t // tm,),
        in_specs=[pl.BlockSpec((tm, MLA_Q_LORA), row), pl.BlockSpec((1, MLA_Q_LORA), fix2),
                  pl.BlockSpec(w_nope.shape, fix2), pl.BlockSpec(w_rope.shape, fix2),
                  pl.BlockSpec(w_swap.shape, fix2), pl.BlockSpec(w_bd.shape, lambda i: (0, 0, 0)),
                  pl.BlockSpec((tm, LANES), row), pl.BlockSpec((tm, LANES), row)],
        out_specs=[pl.BlockSpec((MLA_HEADS, tm, MLA_KV_LORA), lambda i: (0, i, 0)),
                   pl.BlockSpec((MLA_HEADS, tm, MLA_ROPE), lambda i: (0, i, 0))],
        out_shape=[jax.ShapeDtypeStruct((MLA_HEADS, t, MLA_KV_LORA), F32),
                   jax.ShapeDtypeStruct((MLA_HEADS, t, MLA_ROPE), F32)],
        compiler_params=_cparams("parallel"),
        name="mla_q",
    )(cq, q_norm_w.reshape(1, -1), w_nope, w_rope, w_swap, w_bd, cos4, sin4)


MLA_QSCALE = (MLA_NOPE + MLA_ROPE) ** -0.5 * math.log2(math.e)


def _lane_repeat(x, width):
    return jnp.concatenate([x] * (width // LANES), axis=1)


def _flash_chunk(s, cb, m_ref, l_ref, acc_ref, rs):
    m_prev = m_ref[rs]
    m_new = jnp.maximum(m_prev, jnp.max(s, axis=-1, keepdims=True))
    alpha = jnp.exp2(m_prev - m_new)
    p = jnp.exp2(s - _lane_repeat(m_new, s.shape[1]))
    l_ref[rs] = alpha * l_ref[rs] + jnp.sum(p, axis=-1, keepdims=True)
    acc_ref[rs] = (_lane_repeat(alpha, acc_ref.shape[1]) * acc_ref[rs]
                   + jnp.dot(p.astype(BF16), cb, preferred_element_type=F32))
    m_ref[rs] = m_new


def _flash_init(ql_ref, qr_ref, qlb_ref, qrb_ref, m_ref, l_ref, acc_ref):
    rows = qlb_ref.shape[0]
    qlb_ref[...] = (ql_ref[...].reshape(rows, MLA_KV_LORA) * MLA_QSCALE).astype(BF16)
    qrb_ref[...] = (qr_ref[...].reshape(rows, MLA_ROPE) * MLA_QSCALE).astype(BF16)
    m_ref[...] = jnp.full(m_ref.shape, NEG_INF, F32)
    l_ref[...] = jnp.zeros(l_ref.shape, F32)
    acc_ref[...] = jnp.zeros(acc_ref.shape, F32)


def _mla_finish(acc_ref, l_ref, wv_ref, o_ref, rows_per_head):
    inv = _lane_repeat(1.0 / l_ref[...], acc_ref.shape[1])
    outs = []
    for j in range(MLA_HEADS // 2):
        pair = None
        for h in (2 * j, 2 * j + 1):
            rs = slice(h * rows_per_head, (h + 1) * rows_per_head)
            part = _bdot(acc_ref[rs] * inv[rs], wv_ref[h])
            pair = part if pair is None else pair + part
        outs.append(pair)
    o_ref[...] = jnp.concatenate(outs, axis=-1)


def _mla_prompt_kernel(wq_ref, wk_ref, ql_ref, qr_ref, c_ref, r_ref, wv_ref, o_ref, qlb_ref, qrb_ref, m_ref,
                       l_ref, acc_ref, *, tq, tk, rc):
    qi = wq_ref[pl.program_id(1)]
    kj = wk_ref[pl.program_id(1)]
    last = (qi * tq + tq - 1) // tk
    rows = MLA_HEADS * tq

    @pl.when(kj == 0)
    def _():
        _flash_init(ql_ref, qr_ref, qlb_ref, qrb_ref, m_ref, l_ref, acc_ref)

    def step(masked):
        cb = c_ref[...].astype(BF16)
        rb = r_ref[...].astype(BF16)
        chunks = [slice(ch * rc, (ch + 1) * rc) for ch in range(rows // rc)]
        scores = [_bdot_nt(qlb_ref[rs], cb) + _bdot_nt(qrb_ref[rs], rb) for rs in chunks]
        if masked:
            q_pos = qi * tq + lax.broadcasted_iota(jnp.int32, (rc, tk), 0) % tq
            k_pos = kj * tk + lax.broadcasted_iota(jnp.int32, (rc, tk), 1)
            scores = [jnp.where(k_pos <= q_pos, s, NEG_INF) for s in scores]
        probs, alphas = [], []
        for rs, s in zip(chunks, scores):
            m_prev = m_ref[rs]
            m_new = jnp.maximum(m_prev, jnp.max(s, axis=-1, keepdims=True))
            alpha = jnp.exp2(m_prev - m_new)
            p = jnp.exp2(s - _lane_repeat(m_new, tk))
            l_ref[rs] = alpha * l_ref[rs] + jnp.sum(p, axis=-1, keepdims=True)
            m_ref[rs] = m_new
            probs.append(p.astype(BF16))
            alphas.append(alpha)
        for rs, p, alpha in zip(chunks, probs, alphas):
            acc_ref[rs] = (_lane_repeat(alpha, MLA_KV_LORA) * acc_ref[rs]
                           + jnp.dot(p, cb, preferred_element_type=F32))

    @pl.when(kj < last)
    def _():
        step(False)

    @pl.when(kj == last)
    def _():
        step(True)
        _mla_finish(acc_ref, l_ref, wv_ref, o_ref, tq)


def _pad_uv(w_uv):
    a = jnp.transpose(w_uv, (1, 0, 2))
    z = jnp.zeros_like(a)
    even = (jnp.arange(a.shape[0]) % 2 == 0)[:, None, None]
    return jnp.concatenate([jnp.where(even, a, z), jnp.where(even, z, a)], axis=2)


def _mla_prompt_attn(q_lat, q_rope, c_kv, k_r, w_uv_pad, *, n_seq, seq_len, tq, tk):
    t = c_kv.shape[0]
    nq, nk = seq_len // tq, seq_len // tk
    assert tk % tq == 0
    pairs = [(i, j) for i in range(nq) for j in range((i * tq + tq - 1) // tk + 1)]
    work_q = jnp.asarray([p[0] for p in pairs], jnp.int32)
    work_k = jnp.asarray([p[1] for p in pairs], jnp.int32)

    def qmap(b, w, wq, wk):
        return (0, b * nq + wq[w], 0)

    def kmap(b, w, wq, wk):
        return (b * nk + wk[w], 0)

    rows = MLA_HEADS * tq
    return pl.pallas_call(
        functools.partial(_mla_prompt_kernel, tq=tq, tk=tk, rc=4 * tq),
        grid_spec=pltpu.PrefetchScalarGridSpec(
            num_scalar_prefetch=2,
            grid=(n_seq, len(pairs)),
            in_specs=[pl.BlockSpec((MLA_HEADS, tq, MLA_KV_LORA), qmap),
                      pl.BlockSpec((MLA_HEADS, tq, MLA_ROPE), qmap),
                      pl.BlockSpec((tk, MLA_KV_LORA), kmap), pl.BlockSpec((tk, MLA_ROPE), kmap),
                      pl.BlockSpec(w_uv_pad.shape, lambda b, w, wq, wk: (0, 0, 0))],
            out_specs=pl.BlockSpec((tq, MLA_HEADS * MLA_V), lambda b, w, wq, wk: (b * nq + wq[w], 0)),
            scratch_shapes=_flash_scratch(rows)),
        out_shape=jax.ShapeDtypeStruct((t, MLA_HEADS * MLA_V), F32),
        compiler_params=_cparams("parallel", "arbitrary"),
        name="mla_prompt_attn",
    )(work_q, work_k, q_lat, q_rope, c_kv, k_r, w_uv_pad)


def _flash_scratch(rows):
    return [pltpu.VMEM((rows, MLA_KV_LORA), BF16), pltpu.VMEM((rows, MLA_ROPE), BF16),
            pltpu.VMEM((rows, LANES), F32), pltpu.VMEM((rows, LANES), F32),
            pltpu.VMEM((rows, MLA_KV_LORA), F32)]


def _mla_sample_kernel(pt_ref, ql_ref, qr_ref, lat_hbm, kr_hbm, cn_ref, rn_ref, wv_ref, alias_ref, o_ref,
                       lat_buf, kr_buf, sem, qlb_ref, qrb_ref, m_ref, l_ref, acc_ref, *, n_pg, t_s):
    del alias_ref
    b = pl.program_id(0)
    n_groups = pt_ref.shape[1] // n_pg
    rows = MLA_HEADS * t_s
    everything = slice(0, rows)
    _flash_init(ql_ref, qr_ref, qlb_ref, qrb_ref, m_ref, l_ref, acc_ref)

    def copies(seq, g, slot):
        out = []
        for p in range(n_pg):
            page = pt_ref[seq, g * n_pg + p]
            out.append(pltpu.make_async_copy(lat_hbm.at[0, page], lat_buf.at[slot, p], sem.at[0, slot]))
            out.append(pltpu.make_async_copy(kr_hbm.at[0, page], kr_buf.at[slot, p], sem.at[1, slot]))
        return out

    @pl.when(b == 0)
    def _():
        for c in copies(0, 0, 0):
            c.start()

    def body(g, carry):
        slot = g % 2

        @pl.when(g + 1 < n_groups)
        def _():
            for c in copies(b, g + 1, 1 - slot):
                c.start()

        @pl.when((g + 1 == n_groups) & (b + 1 < pl.num_programs(0)))
        def _():
            for c in copies(b + 1, 0, 0):
                c.start()

        for c in copies(b, 0, slot):
            c.wait()
        page = lat_buf.shape[2]
        cb = lat_buf[slot].reshape(n_pg * page, MLA_KV_LORA).astype(BF16)
        rbt = jnp.concatenate([kr_buf[slot, p] for p in range(n_pg)], axis=1).astype(BF16)
        s = _bdot_nt(qlb_ref[...], cb) + jnp.dot(qrb_ref[...], rbt, preferred_element_type=F32)
        _flash_chunk(s, cb, m_ref, l_ref, acc_ref, everything)
        return carry

    lax.fori_loop(0, n_groups, body, 0)
    cb = _pad_rows(cn_ref[...], LANES).astype(BF16)
    rb = _pad_rows(rn_ref[...], LANES).astype(BF16)
    s = _bdot_nt(qlb_ref[...], cb) + _bdot_nt(qrb_ref[...], rb)
    q_t = lax.broadcasted_iota(jnp.int32, (rows, LANES), 0) % t_s
    k_t = lax.broadcasted_iota(jnp.int32, (rows, LANES), 1)
    s = jnp.where(k_t <= q_t, s, NEG_INF)
    _flash_chunk(s, cb, m_ref, l_ref, acc_ref, everything)
    _mla_finish(acc_ref, l_ref, wv_ref, o_ref, t_s)


def _mla_sample_attn(q_lat, q_rope, c_kv, k_r, cache_lat, cache_kr_t, page_table, w_uv_pad, o_full, *,
                     n_seq, t_s, row0, n_pg):
    n_pages = page_table.shape[1]
    page = cache_lat.shape[2]
    assert n_pages % (2 * n_pg) == 0 and row0 % t_s == 0
    base = row0 // t_s
    qmap = lambda b, pt: (0, base + b, 0)
    newmap = lambda b, pt: (base + b, 0)
    hbm = pl.BlockSpec(memory_space=pl.ANY)
    in_specs = [pl.BlockSpec((MLA_HEADS, t_s, MLA_KV_LORA), qmap), pl.BlockSpec((MLA_HEADS, t_s, MLA_ROPE), qmap),
                hbm, hbm,
                pl.BlockSpec((t_s, MLA_KV_LORA), newmap), pl.BlockSpec((t_s, MLA_ROPE), newmap),
                pl.BlockSpec(w_uv_pad.shape, lambda b, pt: (0, 0, 0)), hbm]
    scratch = [pltpu.VMEM((2, n_pg, page, MLA_KV_LORA), F32), pltpu.VMEM((2, n_pg, MLA_ROPE, page), F32),
               pltpu.SemaphoreType.DMA((2, 2))]
    return pl.pallas_call(
        functools.partial(_mla_sample_kernel, n_pg=n_pg, t_s=t_s),
        grid_spec=pltpu.PrefetchScalarGridSpec(
            num_scalar_prefetch=1,
            grid=(n_seq,),
            in_specs=in_specs,
            out_specs=pl.BlockSpec((t_s, MLA_HEADS * MLA_V), newmap),
            scratch_shapes=scratch + _flash_scratch(MLA_HEADS * t_s)),
        out_shape=jax.ShapeDtypeStruct(o_full.shape, F32),
        input_output_aliases={len(in_specs): 0},
        compiler_params=_cparams("arbitrary"),
        name="mla_sample_attn",
    )(page_table, q_lat, q_rope, cache_lat, cache_kr_t, c_kv, k_r, w_uv_pad, o_full)


def _positions(dims, past_len):
    n_p, t_p, n_s, t_s = dims
    return np.concatenate([np.tile(np.arange(t_p), n_p), np.tile(past_len + np.arange(t_s), n_s)])


def _mla_layer(x, nw, cache_lat, cache_kr, page_table, w_in, q_norm_w, w_uq, kv_norm_w, w_uk, w_uv, w_out,
               dims):
    n_p, t_p, n_s, t_s = dims
    t = x.shape[0]
    n_pages = page_table.shape[1]
    past_len = n_pages * cache_lat.shape[2]
    cos, sin = _rope_tables(_positions(dims, past_len), MLA_ROPE // 2)
    cq, ckv, kr = _mla_in(x, nw, w_in, kv_norm_w, cos, sin, tm=_tile(t, 640))
    q_lat, q_rope = _mla_q(cq, q_norm_w, w_uq, w_uk, cos, sin, tm=_tile(t, 256))
    wv = _pad_uv(w_uv)
    o = _mla_prompt_attn(q_lat, q_rope, ckv, kr, wv, n_seq=n_p, seq_len=t_p, tq=min(t_p, 128),
                         tk=min(t_p, 512))
    o = _mla_sample_attn(q_lat, q_rope, ckv, kr, cache_lat, jnp.swapaxes(cache_kr, 2, 3), page_table, wv, o,
                         n_seq=n_s, t_s=t_s, row0=n_p * t_p, n_pg=math.gcd(n_pages // 2, 32))
    x = _linear(o, w_out, res=x, tm=_tile(t, 1280), tn=512, name="mla_out")
    return x, ckv, kr


def _lane_halves(a):
    half = LANES // 2
    low = lax.broadcasted_iota(jnp.int32, a.shape, 1) < half
    rolled = pltpu.roll(a, half, axis=1)
    head0 = (jnp.where(low, a, 0.0), jnp.where(low, 0.0, rolled))
    head1 = (jnp.where(low, rolled, 0.0), jnp.where(low, 0.0, a))
    return head0, head1


def _swa_heads(q, k_all, v_all, sink_ref, mask, o_ref):
    rq = q.shape[0]
    scale = SWA_HD ** -0.5
    top = lax.broadcasted_iota(jnp.int32, (2 * rq, 1), 0) < rq
    for cg in range(SWA_KV_HEADS // 2):
        k_heads = _lane_halves(k_all[:, cg * LANES:(cg + 1) * LANES])
        v_heads = _lane_halves(v_all[:, cg * LANES:(cg + 1) * LANES])
        for sub in range(2):
            kh = 2 * cg + sub
            (k_lo, k_hi), (v_lo, v_hi) = k_heads[sub], v_heads[sub]
            qs = jnp.concatenate([q[:, (2 * kh) * LANES:(2 * kh + 1) * LANES],
                                  q[:, (2 * kh + 1) * LANES:(2 * kh + 2) * LANES]], axis=0)
            acc = None
            for which, (kk, vv) in enumerate(((k_lo, v_lo), (k_hi, v_hi))):
                s = jnp.where(mask, _bdot_nt(qs, kk) * scale, NEG_INF)
                sink = jnp.where(top, sink_ref[4 * kh + which], sink_ref[4 * kh + 2 + which])
                m = jnp.maximum(jnp.max(s, axis=-1, keepdims=True), sink)
                e = jnp.exp(s - m)
                p = e / (jnp.sum(e, axis=-1, keepdims=True) + jnp.exp(sink - m))
                part = _bdot(p, vv)
                acc = part if acc is None else acc + part
            o_ref[:, (2 * kh) * LANES:(2 * kh + 1) * LANES] = acc[:rq]
            o_ref[:, (2 * kh + 1) * LANES:(2 * kh + 2) * LANES] = acc[rq:]


def _swa_prompt_kernel(sink_ref, q_ref, kp_ref, kc_ref, vp_ref, vc_ref, o_ref):
    n = pl.program_id(1)
    w = q_ref.shape[0]
    k_all = jnp.concatenate([kp_ref[...], kc_ref[...]], axis=0)
    v_all = jnp.concatenate([vp_ref[...], vc_ref[...]], axis=0)
    r = lax.broadcasted_iota(jnp.int32, (2 * w, 2 * w), 0) % w
    c = lax.broadcasted_iota(jnp.int32, (2 * w, 2 * w), 1)
    mask = (c >= r) & (c <= r + w) & ((n > 0) | (c >= w))
    _swa_heads(q_ref[...], k_all, v_all, sink_ref, mask, o_ref)


def _swa_prompt_attn(qkv, sinks, *, n_seq, seq_len):
    t = qkv.shape[0]
    w = WINDOW
    nb = seq_len // w
    dq = SWA_HEADS * SWA_HD
    dkv = SWA_KV_HEADS * SWA_HD
    kcol = dq // dkv
    cur = lambda col: (lambda b, n: (b * nb + n, col))
    prev = lambda col: (lambda b, n: (b * nb + jnp.maximum(n - 1, 0), col))
    return pl.pallas_call(
        _swa_prompt_kernel,
        grid=(n_seq, nb),
        in_specs=[pl.BlockSpec(memory_space=pltpu.SMEM),
                  pl.BlockSpec((w, dq), cur(0)),
                  pl.BlockSpec((w, dkv), prev(kcol)), pl.BlockSpec((w, dkv), cur(kcol)),
                  pl.BlockSpec((w, dkv), prev(kcol + 1)), pl.BlockSpec((w, dkv), cur(kcol + 1))],
        out_specs=pl.BlockSpec((w, dq), cur(0)),
        out_shape=jax.ShapeDtypeStruct((t, dq), F32),
        compiler_params=_cparams("parallel", "parallel"),
        name="swa_prompt_attn",
    )(sinks, qkv, qkv, qkv, qkv, qkv)


def _swa_sample_kernel(sink_ref, q_ref, kn_ref, vn_ref, kc_ref, vc_ref, alias_ref, o_ref, ko_ref, vo_ref):
    del alias_ref
    t_s = q_ref.shape[0]
    w = kc_ref.shape[0]
    k_all = jnp.concatenate([kc_ref[...], kn_ref[...]], axis=0)
    v_all = jnp.concatenate([vc_ref[...], vn_ref[...]], axis=0)
    r = lax.broadcasted_iota(jnp.int32, (2 * t_s, w + t_s), 0) % t_s
    c = lax.broadcasted_iota(jnp.int32, (2 * t_s, w + t_s), 1)
    mask = (c <= w + r) & (c >= r)
    _swa_heads(q_ref[...], k_all, v_all, sink_ref, mask, o_ref)
    ko_ref[...] = k_all[t_s:]
    vo_ref[...] = v_all[t_s:]


def _swa_sample_attn(qkv, cache_k, cache_v, sinks, o_full, *, n_seq, t_s, row0):
    w = cache_k.shape[1]
    dq = SWA_HEADS * SWA_HD
    dkv = SWA_KV_HEADS * SWA_HD
    kcol = dq // dkv
    base = row0 // t_s
    new = lambda col: (lambda b: (base + b, col))
    seq = lambda b: (b, 0, 0)
    return pl.pallas_call(
        _swa_sample_kernel,
        grid=(n_seq,),
        in_specs=[pl.BlockSpec(memory_space=pltpu.SMEM),
                  pl.BlockSpec((t_s, dq), new(0)),
                  pl.BlockSpec((t_s, dkv), new(kcol)), pl.BlockSpec((t_s, dkv), new(kcol + 1)),
                  pl.BlockSpec((None, w, dkv), seq), pl.BlockSpec((None, w, dkv), seq),
                  pl.BlockSpec(memory_space=pl.ANY)],
        out_specs=[pl.BlockSpec((t_s, dq), new(0)),
                   pl.BlockSpec((None, w, dkv), seq), pl.BlockSpec((None, w, dkv), seq)],
        out_shape=[jax.ShapeDtypeStruct(o_full.shape, F32),
                   jax.ShapeDtypeStruct(cache_k.shape, F32), jax.ShapeDtypeStruct(cache_v.shape, F32)],
        input_output_aliases={6: 0},
        compiler_params=_cparams("parallel"),
        name="swa_sample_attn",
    )(sinks, qkv, qkv, qkv, cache_k, cache_v, o_full)


def _swa_layer(x, nw, cache_k, cache_v, w_qkv, b_qkv, sinks, w_out, b_out, dims):
    n_p, t_p, n_s, t_s = dims
    t = x.shape[0]
    tm = _tile(t, 1280)
    dq = SWA_HEADS * SWA_HD
    dkv = SWA_KV_HEADS * SWA_HD
    qkv = _linear(x, w_qkv, norm_w=nw, bias=b_qkv, tm=tm, tn=512, name="swa_in")
    o = _swa_prompt_attn(qkv, sinks, n_seq=n_p, seq_len=t_p)
    o, k_s, v_s = _swa_sample_attn(qkv, cache_k.reshape(n_s, WINDOW, dkv), cache_v.reshape(n_s, WINDOW, dkv),
                                   sinks, o, n_seq=n_s, t_s=t_s, row0=n_p * t_p)
    x = _linear(o, w_out, bias=b_out, res=x, tm=tm, tn=512, name="swa_out")
    kv_p = jnp.stack([lax.slice(qkv, ((b + 1) * t_p - WINDOW, dq), ((b + 1) * t_p, dq + 2 * dkv))
                      for b in range(n_p)])
    kv_shape = (n_p, WINDOW, SWA_KV_HEADS, SWA_HD)
    k_p = kv_p[:, :, :dkv].reshape(kv_shape)
    v_p = kv_p[:, :, dkv:].reshape(kv_shape)
    return x, k_p, v_p, k_s.reshape(cache_k.shape), v_s.reshape(cache_v.shape)


def _l2norm(x):
    return x * lax.rsqrt(jnp.sum(x * x, axis=-1, keepdims=True) + 1e-6)


def _split2(x):
    h1 = x.astype(BF16)
    return h1, (x - h1.astype(F32)).astype(BF16)


def _bmm3_parts(a_parts, b_parts):
    (a1, a2), (b1, b2) = a_parts, b_parts
    dot = lambda x, y: jnp.einsum("bij,bjk->bik", x, y, preferred_element_type=F32)
    return dot(a1, b1) + dot(a1, b2) + dot(a2, b1)


def _bmm3(a, b):
    return _bmm3_parts(_split2(a), _split2(b))


def _unit_lower_inverse(low):
    n = low.shape[-1]
    eye = (lax.broadcasted_iota(jnp.int32, (n, n), 0) == lax.broadcasted_iota(jnp.int32, (n, n), 1))
    eye = eye.astype(F32)[None]
    power = -low
    inv = eye + power
    parts = _split2(power)
    for _ in range(int(math.log2(n)) - 1):
        parts = _split2(_bmm3_parts(parts, parts))
        inv = inv + _bmm3_parts(_split2(inv), parts)
    return inv


def _dn_kernel(*refs, rb, zero_init, has_alias):
    it = iter(refs)
    x_ref, z_ref, ab_ref = next(it), next(it), next(it)
    s0_ref, c0_ref = (None, None) if zero_init else (next(it), next(it))
    cw_ref, al_ref, dt_ref, nw_ref = next(it), next(it), next(it), next(it)
    if has_alias:
        next(it)
    o_ref, s_ref, co_ref = next(it), next(it), next(it)
    st_ref, xp_ref = next(it), next(it)
    r = pl.program_id(1)
    rows = max(rb, CHUNK)
    n_chunks = rows // CHUNK
    halo = 8
    dqk = DN_HEADS * DN_HK

    @pl.when(r == 0)
    def _():
        if zero_init:
            st_ref[...] = jnp.zeros(st_ref.shape, F32)
            xp_ref[0:halo] = jnp.zeros((halo, xp_ref.shape[1]), F32)
        else:
            st_ref[...] = s0_ref[0]
            xp_ref[0:halo] = c0_ref[0]

    xp_ref[halo:halo + rb] = x_ref[...]
    full = xp_ref[...]
    conv = full[halo:] * cw_ref[DN_CONV - 1:DN_CONV]
    for w in range(DN_CONV - 1):
        conv = conv + pltpu.roll(full, DN_CONV - 1 - w, axis=0)[halo:] * cw_ref[w:w + 1]
    tail = xp_ref[rb:rb + halo]
    co_ref[0] = tail
    xp_ref[0:halo] = tail
    qkv = _pad_rows(_silu(conv), rows)
    ab = ab_ref[...]
    g_all = _pad_rows(-jnp.exp(al_ref[...]) * _softplus(ab + dt_ref[...]), rows)
    beta_all = _pad_rows(1.0 / (1.0 + jnp.exp(-ab)), rows)

    ri = lax.broadcasted_iota(jnp.int32, (rows, rows), 0)
    ci = lax.broadcasted_iota(jnp.int32, (rows, rows), 1)
    same = (ri // CHUNK) == (ci // CHUNK)
    causal = same & (ci <= ri)
    strict = same & (ci < ri)
    upper = (same & (ri <= ci)).astype(BF16)
    gc_col = _dot_exact_lhs(causal, g_all)
    gt_col = _dot_exact_lhs(same, g_all)
    g1, g2, g3 = _split3(g_all)
    tn = lambda a: lax.dot_general(a, upper, (((0,), (0,)), ((), ())), preferred_element_type=F32)
    gc_row = tn(g1) + tn(g2) + tn(g3)

    lows, rhss, attns, qds, kos, gls = [], [], [], [], [], []
    for h in range(DN_HEADS):
        hs = slice(h * DN_HK, (h + 1) * DN_HK)
        q = _l2norm(qkv[:, hs]) * DN_HK ** -0.5
        k = _l2norm(qkv[:, dqk + h * DN_HK:dqk + (h + 1) * DN_HK])
        v = qkv[:, 2 * dqk + h * DN_HV:2 * dqk + (h + 1) * DN_HV]
        beta = beta_all[:, DN_HEADS + h:DN_HEADS + h + 1]
        gc = gc_col[:, h:h + 1]
        gt = gt_col[:, h:h + 1]
        decay = jnp.where(causal, jnp.exp(jnp.where(causal, gc - gc_row[h:h + 1, :], 0.0)), 0.0)
        kb = k * beta
        low = jnp.where(strict, _bdot_nt(kb, k) * decay, 0.0)
        attn = _bdot_nt(q, k) * decay
        rhs = jnp.concatenate([v * beta, kb * jnp.exp(gc)], axis=-1)
        q_dec = q * jnp.exp(gc)
        k_out = k * jnp.exp(gt - gc)
        g_last = jnp.exp(gt)
        for c in range(n_chunks):
            rs = slice(c * CHUNK, (c + 1) * CHUNK)
            lows.append(low[rs, rs])
            attns.append(attn[rs, rs])
            rhss.append(rhs[rs])
            qds.append(q_dec[rs])
            kos.append(k_out[rs])
            gls.append(g_last[c * CHUNK:c * CHUNK + 1])
    sol = _bmm3(_unit_lower_inverse(jnp.stack(lows)), jnp.stack(rhss))
    n_out = min(rb, CHUNK)
    for h in range(DN_HEADS):
        s = st_ref[h]
        for c in range(n_chunks):
            i = h * n_chunks + c
            u, wm = sol[i, :, :DN_HV], sol[i, :, DN_HV:]
            v_new = u - _bdot(wm, s)
            o = _bdot(qds[i], s) + _bdot(attns[i], v_new)
            s = s * gls[i] + _bdot_tn(kos[i], v_new)
            zs = z_ref[c * CHUNK:c * CHUNK + n_out, h * DN_HV:(h + 1) * DN_HV]
            o_ref[c * CHUNK:c * CHUNK + n_out, h * DN_HV:(h + 1) * DN_HV] = (
                _rms(o[:n_out], nw_ref[...]) * _silu(zs))
        st_ref[h] = s

    @pl.when(r == pl.num_programs(1) - 1)
    def _():
        s_ref[0] = st_ref[...]


def _dn_core(proj, s0, conv0, conv_w, a_log, dt_bias, norm_w, o_full, *, n_seq, seq_len, row0, rb):
    t = proj.shape[0]
    dconv = conv_w.shape[1]
    dz = DN_HEADS * DN_HV
    nblk = seq_len // rb
    base = row0 // rb
    assert row0 % rb == 0 and seq_len % rb == 0 and rb % 8 == 0

    def rowmap(col):
        return lambda b, r: (base + b * nblk + r, col)

    seq4 = lambda b, r: (b, 0, 0, 0)
    seq3 = lambda b, r: (b, 0, 0)
    fix = lambda b, r: (0, 0)
    in_specs = [pl.BlockSpec((rb, dconv), rowmap(0)), pl.BlockSpec((rb, dz), rowmap(dconv // dz)),
                pl.BlockSpec((rb, LANES), rowmap((dconv + dz) // LANES))]
    args = [proj, proj, proj]
    if s0 is not None:
        in_specs += [pl.BlockSpec((1, DN_HEADS, DN_HK, DN_HV), seq4), pl.BlockSpec((1, 8, dconv), seq3)]
        args += [s0, conv0]
    in_specs += [pl.BlockSpec((DN_CONV, dconv), fix), pl.BlockSpec((1, LANES), fix),
                 pl.BlockSpec((1, LANES), fix), pl.BlockSpec((1, DN_HV), fix)]
    args += [conv_w, _pad_cols(a_log.reshape(1, -1), LANES), _pad_cols(dt_bias.reshape(1, -1), LANES),
             norm_w.reshape(1, -1)]
    aliases = {}
    if o_full is not None:
        in_specs.append(pl.BlockSpec(memory_space=pl.ANY))
        aliases = {len(args): 0}
        args.append(o_full)
    kern = functools.partial(_dn_kernel, rb=rb, zero_init=s0 is None, has_alias=o_full is not None)
    return pl.pallas_call(
        kern,
        grid=(n_seq, nblk),
        in_specs=in_specs,
        out_specs=[pl.BlockSpec((rb, dz), rowmap(0)),
                   pl.BlockSpec((1, DN_HEADS, DN_HK, DN_HV), seq4),
                   pl.BlockSpec((1, 8, dconv), seq3)],
        out_shape=[jax.ShapeDtypeStruct((t, dz), F32),
                   jax.ShapeDtypeStruct((n_seq, DN_HEADS, DN_HK, DN_HV), F32),
                   jax.ShapeDtypeStruct((n_seq, 8, dconv), F32)],
        scratch_shapes=[pltpu.VMEM((DN_HEADS, DN_HK, DN_HV), F32), pltpu.VMEM((rb + 8, dconv), F32)],
        input_output_aliases=aliases,
        compiler_params=_cparams("parallel", "arbitrary"),
        name="dn_core",
    )(*args)


def _dn_layer(x, nw, s0_s, conv0_s, w_in, conv_w, a_log, dt_bias, norm_w, w_out, dims):
    n_p, t_p, n_s, t_s = dims
    t = x.shape[0]
    tm = _tile(t, 1280)
    dconv = conv_w.shape[1]
    dz = DN_HEADS * DN_HV
    width = dconv + dz + LANES
    proj = _linear(x, _pad_cols(w_in, width), norm_w=nw, tm=tm, tn=_tile(width, 1408), name="dn_in")
    o, sp, cp = _dn_core(proj, None, None, conv_w, a_log, dt_bias, norm_w, None, n_seq=n_p, seq_len=t_p,
                         row0=0, rb=min(t_p, 128))
    conv0 = jnp.pad(conv0_s, ((0, 0), (8 - conv0_s.shape[1], 0), (0, 0)))
    o, ss, cs = _dn_core(proj, s0_s, conv0, conv_w, a_log, dt_bias, norm_w, o, n_seq=n_s, seq_len=t_s,
                         row0=n_p * t_p, rb=t_s)
    x = _linear(o, w_out, res=x, tm=tm, tn=512, name="dn_out")
    keep = DN_CONV - 1
    return x, sp, ss, cp[:, 8 - keep:], cs[:, 8 - keep:]


def kernel(x_prompt, x_sample, state_gla, cache_mla_latent, cache_mla_krope, cache_swa_k, cache_swa_v, state_delta, state_delta_conv, page_table, norm_w, final_norm_w, gla_w_in, gla_w_gk2, gla_b_gk2, gla_norm_w, gla_w_out, mla_w_in, mla_q_norm_w, mla_w_uq, mla_kv_norm_w, mla_w_uk, mla_w_uv, mla_w_out, swa_w_qkv, swa_b_qkv, swa_sinks, swa_w_out, swa_b_out, dn_w_in, dn_conv_w, dn_a_log, dn_dt_bias, dn_norm_w, dn_w_out, ffn_w_gate, ffn_w_up, ffn_w_down, moe_w_router, moe_w_gate, moe_w_up, moe_w_down):
    n_p, t_p, d = x_prompt.shape
    n_s, t_s, _ = x_sample.shape
    dims = (n_p, t_p, n_s, t_s)
    x = jnp.concatenate([x_prompt.reshape(n_p * t_p, d), x_sample.reshape(n_s * t_s, d)], axis=0)
    t = x.shape[0]
    tm = _tile(t, 1280)
    n_tp = n_p * t_p

    x, gla_p, gla_s = _gla_layer(x, norm_w[0, 0], state_gla[0], gla_w_in[0], gla_w_gk2[0], gla_b_gk2[0],
                                 gla_norm_w[0], gla_w_out[0], dims)
    x = _ffn(x, norm_w[0, 1], ffn_w_gate, ffn_w_up, ffn_w_down, 0, tm=tm, tf=512)
    moe_tiles = dict(tm=_tile(t, 640), tp=1280, tf=512)

    x, ckv, kr = _mla_layer(x, norm_w[1, 0], cache_mla_latent[0:1], cache_mla_krope[0:1], page_table,
                            mla_w_in[0], mla_q_norm_w[0], mla_w_uq[0], mla_kv_norm_w[0], mla_w_uk[0],
                            mla_w_uv[0], mla_w_out[0], dims)
    x = _moe(x, norm_w[1, 1], moe_w_router[0], moe_w_gate, moe_w_up, moe_w_down, 0, **moe_tiles)

    x, swk_p, swv_p, swk_s, swv_s = _swa_layer(x, norm_w[2, 0], cache_swa_k[0], cache_swa_v[0],
                                               swa_w_qkv[0], swa_b_qkv[0], swa_sinks[0], swa_w_out[0],
                                               swa_b_out[0], dims)
    x = _ffn(x, norm_w[2, 1], ffn_w_gate, ffn_w_up, ffn_w_down, 1, tm=tm, tf=512)

    x, dn_p, dn_s, cv_p, cv_s = _dn_layer(x, norm_w[3, 0], state_delta[0], state_delta_conv[0], dn_w_in[0],
                                          dn_conv_w[0], dn_a_log[0], dn_dt_bias[0], dn_norm_w[0],
                                          dn_w_out[0], dims)
    y = _moe(x, norm_w[3, 1], moe_w_router[1], moe_w_gate, moe_w_up, moe_w_down, 1, final_norm_w,
             **moe_tiles)

    lead = lambda a: a[None]
    return (y[:n_tp].reshape(n_p, t_p, d), y[n_tp:].reshape(n_s, t_s, d),
            lead(gla_p), lead(gla_s),
            lead(ckv[:n_tp].reshape(n_p, t_p, -1)), lead(ckv[n_tp:].reshape(n_s, t_s, -1)),
            lead(kr[:n_tp].reshape(n_p, t_p, -1)), lead(kr[n_tp:].reshape(n_s, t_s, -1)),
            lead(swk_p), lead(swk_s), lead(swv_p), lead(swv_s),
            lead(dn_p), lead(dn_s), lead(cv_p), lead(cv_s))
```

```python
import functools
import math

import jax
import jax.numpy as jnp
import numpy as np
from jax import lax
from jax.experimental import pallas as pl
from jax.experimental.pallas import tpu as pltpu

F32 = jnp.float32
BF16 = jnp.bfloat16

NORM_EPS = 1e-6
GLA_HEADS = 4
GLA_GATE_RANK = 16
GLA_GATE_NORMALIZER = 16.0
CHUNK = 64
MLA_HEADS = 16
MLA_Q_LORA = 384
MLA_KV_LORA = 256
MLA_NOPE = 64
MLA_ROPE = 32
MLA_V = 64
ROPE_THETA = 10000.0
SWA_HEADS = 16
SWA_KV_HEADS = 4
SWA_HD = 64
WINDOW = 128
DN_HEADS = 8
DN_HK = 128
DN_HV = 128
DN_CONV = 4
N_EXPERTS = 8

LANES = 128
VMEM_LIMIT = 56 * 1024 * 1024
NEG_INF = float("-inf")


def _cparams(*sem):
    return pltpu.CompilerParams(dimension_semantics=sem, vmem_limit_bytes=VMEM_LIMIT)


def _bdot(a, b):
    return jnp.dot(a.astype(BF16), b.astype(BF16), preferred_element_type=F32)


def _bdot_nt(a, b):
    return lax.dot_general(a.astype(BF16), b.astype(BF16), (((1,), (1,)), ((), ())),
                           preferred_element_type=F32)


def _bdot_tn(a, b):
    return lax.dot_general(a.astype(BF16), b.astype(BF16), (((0,), (0,)), ((), ())),
                           preferred_element_type=F32)


def _split3(x):
    h1 = x.astype(BF16)
    r1 = x - h1.astype(F32)
    h2 = r1.astype(BF16)
    h3 = (r1 - h2.astype(F32)).astype(BF16)
    return h1, h2, h3


def _dot_exact_lhs(m, x):
    mb = m.astype(BF16)
    h1, h2, h3 = _split3(x)
    return (jnp.dot(mb, h1, preferred_element_type=F32) + jnp.dot(mb, h2, preferred_element_type=F32)
            + jnp.dot(mb, h3, preferred_element_type=F32))


def _rms(x, w):
    return x * lax.rsqrt(jnp.mean(x * x, axis=-1, keepdims=True) + NORM_EPS) * w


def _silu(x):
    return x / (1.0 + jnp.exp(-x))


def _log_sigmoid(x):
    return jnp.minimum(x, 0.0) - jnp.log(1.0 + jnp.exp(-jnp.abs(x)))


def _softplus(x):
    return jnp.maximum(x, 0.0) + jnp.log(1.0 + jnp.exp(-jnp.abs(x)))


def _linear_kernel(*refs, has_norm, has_bias, has_res):
    it = iter(refs)
    x_ref = next(it)
    nw_ref = next(it) if has_norm else None
    w_ref = next(it)
    b_ref = next(it) if has_bias else None
    r_ref = next(it) if has_res else None
    o_ref = next(it)
    h_ref = next(it)

    @pl.when(pl.program_id(1) == 0)
    def _():
        xv = x_ref[...].astype(F32)
        if has_norm:
            xv = _rms(xv, nw_ref[...])
        h_ref[...] = xv.astype(BF16)

    acc = jnp.dot(h_ref[...], w_ref[...].astype(BF16), preferred_element_type=F32)
    if has_bias:
        acc = acc + b_ref[...]
    if has_res:
        acc = acc + r_ref[...]
    o_ref[...] = acc.astype(o_ref.dtype)


def _linear(x, w, *, norm_w=None, bias=None, res=None, tm, tn, out_dtype=F32, name="linear"):
    t, k = x.shape
    n = w.shape[1]
    assert t % tm == 0 and n % tn == 0, (t, tm, n, tn)
    in_specs = [pl.BlockSpec((tm, k), lambda i, j: (i, 0))]
    args = [x]
    if norm_w is not None:
        in_specs.append(pl.BlockSpec((1, k), lambda i, j: (0, 0)))
        args.append(norm_w.reshape(1, k))
    in_specs.append(pl.BlockSpec((k, tn), lambda i, j: (0, j)))
    args.append(w)
    if bias is not None:
        in_specs.append(pl.BlockSpec((1, tn), lambda i, j: (0, j)))
        args.append(bias.reshape(1, n))
    if res is not None:
        in_specs.append(pl.BlockSpec((tm, tn), lambda i, j: (i, j)))
        args.append(res)
    kern = functools.partial(_linear_kernel, has_norm=norm_w is not None, has_bias=bias is not None,
                             has_res=res is not None)
    return pl.pallas_call(
        kern,
        grid=(t // tm, n // tn),
        in_specs=in_specs,
        out_specs=pl.BlockSpec((tm, tn), lambda i, j: (i, j)),
        out_shape=jax.ShapeDtypeStruct((t, n), out_dtype),
        scratch_shapes=[pltpu.VMEM((tm, k), BF16)],
        compiler_params=_cparams("parallel", "arbitrary"),
        name=name,
    )(*args)


def _swiglu_acc(h_ref, wg_ref, wu_ref, wd_ref, acc_ref):
    h = h_ref[...].astype(BF16)
    g = jnp.dot(h, wg_ref[...].astype(BF16), preferred_element_type=F32)
    u = jnp.dot(h, wu_ref[...].astype(BF16), preferred_element_type=F32)
    a = (_silu(g) * u).astype(BF16)
    acc_ref[...] += jnp.dot(a, wd_ref[...].astype(BF16), preferred_element_type=F32)


def _ffn_kernel(x_ref, nw_ref, wg_ref, wu_ref, wd_ref, o_ref, h_ref, acc_ref):
    f = pl.program_id(1)

    @pl.when(f == 0)
    def _():
        h_ref[...] = _rms(x_ref[...], nw_ref[...]).astype(BF16)
        acc_ref[...] = jnp.zeros_like(acc_ref)

    _swiglu_acc(h_ref, wg_ref, wu_ref, wd_ref, acc_ref)

    @pl.when(f == pl.num_programs(1) - 1)
    def _():
        o_ref[...] = x_ref[...] + acc_ref[...]


def _ffn(x, norm_w, wg, wu, wd, layer, *, tm, tf):
    t, d = x.shape
    ff = wg.shape[2]
    assert t % tm == 0 and ff % tf == 0
    return pl.pallas_call(
        _ffn_kernel,
        grid=(t // tm, ff // tf),
        in_specs=[pl.BlockSpec((tm, d), lambda i, f: (i, 0)),
                  pl.BlockSpec((1, d), lambda i, f: (0, 0)),
                  pl.BlockSpec((None, d, tf), lambda i, f: (layer, 0, f)),
                  pl.BlockSpec((None, d, tf), lambda i, f: (layer, 0, f)),
                  pl.BlockSpec((None, tf, d), lambda i, f: (layer, f, 0))],
        out_specs=pl.BlockSpec((tm, d), lambda i, f: (i, 0)),
        out_shape=jax.ShapeDtypeStruct((t, d), F32),
        scratch_shapes=[pltpu.VMEM((tm, d), BF16), pltpu.VMEM((tm, d), F32)],
        compiler_params=_cparams("parallel", "arbitrary"),
        name="ffn",
    )(x, norm_w.reshape(1, d), wg, wu, wd)


MOE_CHUNK = 256
MOE_ALIGN = 16
MOE_SLOTS = 4


def _route_kernel(x_ref, nw_ref, wrt_ref, h_ref, sel_ref, gate_ref, rank_ref, cnt_ref):
    hn = _rms(x_ref[...], nw_ref[...])
    h_ref[...] = hn.astype(BF16)
    logits = lax.dot_general(wrt_ref[...], hn, (((1,), (1,)), ((), ())), preferred_element_type=F32,
                             precision=lax.Precision.HIGHEST)
    n_exp, tm = logits.shape
    sub = lax.broadcasted_iota(jnp.int32, logits.shape, 0)
    m1 = jnp.max(logits, axis=0, keepdims=True)
    i1 = jnp.min(jnp.where(logits == m1, sub, n_exp), axis=0, keepdims=True)
    rest = jnp.where(sub == i1, NEG_INF, logits)
    m2 = jnp.max(rest, axis=0, keepdims=True)
    i2 = jnp.min(jnp.where(rest == m2, sub, n_exp), axis=0, keepdims=True)
    e2 = jnp.exp(m2 - m1)
    first, second = sub == i1, sub == i2
    sel = jnp.where(first | second, 1.0, 0.0)
    upper = jnp.where(lax.broadcasted_iota(jnp.int32, (tm, tm), 0)
                      <= lax.broadcasted_iota(jnp.int32, (tm, tm), 1), 1.0, 0.0)
    cum = _bdot(sel, upper)
    sel_ref[...] = sel
    gate_ref[...] = jnp.where(first, 1.0 / (1.0 + e2), 0.0) + jnp.where(second, e2 / (1.0 + e2), 0.0)
    rank_ref[...] = cum - sel
    cnt_ref[0] = jnp.broadcast_to(cum[:, tm - 1:tm], (n_exp, LANES))


def _route(x, norm_w, w_router, *, tm):
    t, d = x.shape
    n_exp = w_router.shape[1]
    et = lambda i: (0, i)
    return pl.pallas_call(
        _route_kernel,
        grid=(t // tm,),
        in_specs=[pl.BlockSpec((tm, d), lambda i: (i, 0)), pl.BlockSpec((1, d), lambda i: (0, 0)),
                  pl.BlockSpec((n_exp, d), lambda i: (0, 0))],
        out_specs=[pl.BlockSpec((tm, d), lambda i: (i, 0)), pl.BlockSpec((n_exp, tm), et),
                   pl.BlockSpec((n_exp, tm), et), pl.BlockSpec((n_exp, tm), et),
                   pl.BlockSpec((1, n_exp, LANES), lambda i: (i, 0, 0))],
        out_shape=[jax.ShapeDtypeStruct((t, d), BF16), jax.ShapeDtypeStruct((n_exp, t), F32),
                   jax.ShapeDtypeStruct((n_exp, t), F32), jax.ShapeDtypeStruct((n_exp, t), F32),
                   jax.ShapeDtypeStruct((t // tm, n_exp, LANES), F32)],
        compiler_params=_cparams("parallel"),
        name="moe_route",
    )(x, norm_w.reshape(1, d), w_router.T)


def _moe_plan(cnt, *, tm, tp, p_rows):
    n_tiles, n_exp = cnt.shape
    seg = (cnt + MOE_ALIGN - 1) // MOE_ALIGN * MOE_ALIGN
    total = jnp.sum(seg, axis=0)
    in_group = jnp.cumsum(seg, axis=0) - seg
    nch = (cnt + MOE_CHUNK - 1) // MOE_CHUNK
    reach = jnp.max(in_group + nch * MOE_CHUNK, axis=0)
    gsize = (jnp.maximum(total, reach) + tp - 1) // tp * tp
    gstart = jnp.cumsum(gsize) - gsize
    seg_start = gstart[None, :] + in_group
    cum_e = jnp.cumsum(nch, axis=1)
    q_max = n_exp + 2 * tm // MOE_CHUNK
    q = jnp.arange(q_max, dtype=jnp.int32)[None, :]
    flat_e = jnp.minimum(jnp.sum(q[:, :, None] >= cum_e[:, None, :], axis=-1), n_exp - 1).astype(jnp.int32)
    flat_c = q - jnp.take_along_axis(cum_e - nch, flat_e, axis=1)
    flat_row = jnp.take_along_axis(seg_start, flat_e, axis=1) + flat_c * MOE_CHUNK
    row_j = jnp.arange(p_rows // tp, dtype=jnp.int32) * tp
    tile_e = jnp.minimum(jnp.sum(row_j[:, None] >= (gstart + gsize)[None, :], axis=-1), n_exp - 1)
    tile_valid = row_j < jnp.take(gstart + total, tile_e)
    i32 = lambda a: a.astype(jnp.int32)
    return (i32(cum_e[:, -1]), i32(flat_e.reshape(-1)), i32(flat_c.reshape(-1)), i32(flat_row.reshape(-1)),
            i32(tile_e), i32(tile_valid))


def _dispatch_kernel(nq_ref, fe_ref, fc_ref, fr_ref, h_ref, sel_ref, rank_ref, xs_in, xs_ref, stage, sem,
                     done_ref, *, q_max):
    del xs_in
    i = pl.program_id(0)
    n = nq_ref[i]
    h = h_ref[...]

    @pl.when(i == 0)
    def _():
        done_ref[0] = 0

    k0 = done_ref[0]

    def chunk_copy(slot, row):
        return pltpu.make_async_copy(stage.at[slot], xs_ref.at[pl.ds(pl.multiple_of(row, MOE_ALIGN), MOE_CHUNK)],
                                     sem.at[slot])

    def body(q, carry):
        slot = (k0 + q) % MOE_SLOTS
        e = fe_ref[i * q_max + q]
        c = fc_ref[i * q_max + q]

        @pl.when(k0 + q >= MOE_SLOTS)
        def _():
            chunk_copy(slot, 0).wait()

        pos = jnp.where(sel_ref[pl.ds(e, 1), :] > 0.0, rank_ref[pl.ds(e, 1), :], -1.0)
        want = (c * MOE_CHUNK + lax.broadcasted_iota(jnp.int32, (MOE_CHUNK, 1), 0)).astype(F32)
        pick = jnp.where(pos == want, 1.0, 0.0).astype(BF16)
        stage[slot] = jnp.dot(pick, h, preferred_element_type=F32).astype(stage.dtype)
        chunk_copy(slot, fr_ref[i * q_max + q]).start()
        return carry

    lax.fori_loop(0, n, body, 0)
    done_ref[0] = k0 + n
    for s in range(MOE_SLOTS):
        @pl.when((i == pl.num_programs(0) - 1) & (k0 + n > s))
        def _():
            chunk_copy(s, 0).wait()


def _dispatch(h, sel, rank, plan, *, tm, p_rows):
    t, d = h.shape
    n_exp = sel.shape[0]
    n_flat, flat_e, flat_c, flat_row = plan[:4]
    q_max = flat_e.shape[0] // (t // tm)
    et = lambda i, *_: (0, i)
    return pl.pallas_call(
        functools.partial(_dispatch_kernel, q_max=q_max),
        grid_spec=pltpu.PrefetchScalarGridSpec(
            num_scalar_prefetch=4,
            grid=(t // tm,),
            in_specs=[pl.BlockSpec((tm, d), lambda i, *_: (i, 0)), pl.BlockSpec((n_exp, tm), et),
                      pl.BlockSpec((n_exp, tm), et), pl.BlockSpec(memory_space=pl.ANY)],
            out_specs=pl.BlockSpec(memory_space=pl.ANY),
            scratch_shapes=[pltpu.VMEM((MOE_SLOTS, MOE_CHUNK, d), BF16),
                            pltpu.SemaphoreType.DMA((MOE_SLOTS,)), pltpu.SMEM((1,), jnp.int32)]),
        out_shape=jax.ShapeDtypeStruct((p_rows, d), BF16),
        input_output_aliases={7: 0},
        compiler_params=_cparams("arbitrary"),
        name="moe_dispatch",
    )(n_flat, flat_e, flat_c, flat_row, h, sel, rank, jnp.zeros((p_rows, d), BF16))


def _gffn_kernel(te_ref, tv_ref, x_ref, wg_ref, wu_ref, wd_ref, o_ref, acc_ref):
    del te_ref
    j = pl.program_id(0)
    f = pl.program_id(1)
    valid = tv_ref[j] > 0

    @pl.when(f == 0)
    def _():
        acc_ref[...] = jnp.zeros_like(acc_ref)

    @pl.when(valid)
    def _():
        _swiglu_acc(x_ref, wg_ref, wu_ref, wd_ref, acc_ref)

    @pl.when(f == pl.num_programs(1) - 1)
    def _():
        o_ref[...] = acc_ref[...].astype(o_ref.dtype)


def _gffn(xs, wg, wu, wd, layer, tile_e, tile_valid, *, tp, tf):
    p_rows, d = xs.shape
    ff = wg.shape[3]

    def wmap(is_down):
        def index(j, f, te, tv):
            fi = jnp.where(tv[j] > 0, f, 0)
            return (layer, te[j], fi, 0) if is_down else (layer, te[j], 0, fi)
        return index

    return pl.pallas_call(
        _gffn_kernel,
        grid_spec=pltpu.PrefetchScalarGridSpec(
            num_scalar_prefetch=2,
            grid=(p_rows // tp, ff // tf),
            in_specs=[pl.BlockSpec((tp, d), lambda j, f, te, tv: (j, 0)),
                      pl.BlockSpec((None, None, d, tf), wmap(False)),
                      pl.BlockSpec((None, None, d, tf), wmap(False)),
                      pl.BlockSpec((None, None, tf, d), wmap(True))],
            out_specs=pl.BlockSpec((tp, d), lambda j, f, te, tv: (j, 0)),
            scratch_shapes=[pltpu.VMEM((tp, d), F32)]),
        out_shape=jax.ShapeDtypeStruct((p_rows, d), BF16),
        compiler_params=_cparams("parallel", "arbitrary"),
        name="moe_ffn",
    )(tile_e, tile_valid, xs, wg, wu, wd)


def _combine_kernel(*refs, q_max, final_norm):
    nq_ref, fe_ref, fc_ref, fr_ref = refs[:4]
    x_ref, pos_ref, gate_ref = refs[4:7]
    fw_ref = refs[7] if final_norm else None
    ys_ref, o_ref, acc_ref, buf, sem, done_ref = refs[7 + int(final_norm):]
    i = pl.program_id(0)
    n = nq_ref[i]
    acc_ref[...] = jnp.zeros_like(acc_ref)
    lane = lax.broadcasted_iota(jnp.int32, pos_ref.shape, 1)

    def chunk_copy(slot, row):
        return pltpu.make_async_copy(ys_ref.at[pl.ds(pl.multiple_of(row, MOE_ALIGN), MOE_CHUNK)], buf.at[slot],
                                     sem.at[slot])

    @pl.when(i == 0)
    def _():
        done_ref[0] = 0
        chunk_copy(0, fr_ref[0]).start()

    k0 = done_ref[0]

    def body(q, carry):
        slot = (k0 + q) % 2
        e = fe_ref[i * q_max + q]
        c = fc_ref[i * q_max + q]

        @pl.when(q + 1 < n)
        def _():
            chunk_copy(1 - slot, fr_ref[i * q_max + q + 1]).start()

        @pl.when((q + 1 == n) & (i + 1 < pl.num_programs(0)))
        def _():
            chunk_copy(1 - slot, fr_ref[(i + 1) * q_max]).start()

        chunk_copy(slot, 0).wait()
        pos = jnp.sum(jnp.where(lane == e, pos_ref[...], 0.0), axis=1, keepdims=True)
        gate = jnp.sum(jnp.where(lane == e, gate_ref[...], 0.0), axis=1, keepdims=True)
        want = (c * MOE_CHUNK + lax.broadcasted_iota(jnp.int32, (1, MOE_CHUNK), 1)).astype(F32)
        pick = jnp.where(pos == want, 1.0, 0.0).astype(BF16)
        acc_ref[...] += gate * jnp.dot(pick, buf[slot], preferred_element_type=F32)
        return carry

    lax.fori_loop(0, n, body, 0)
    done_ref[0] = k0 + n
    y = x_ref[...] + acc_ref[...]
    if final_norm:
        y = _rms(y, fw_ref[...])
    o_ref[...] = y


def _combine(x, ys, pos_tok, gate_tok, plan, final_w, *, tm):
    t, d = x.shape
    n_exp = pos_tok.shape[1]
    n_flat, flat_e, flat_c, flat_row = plan[:4]
    q_max = flat_e.shape[0] // (t // tm)
    row = lambda i, *_: (i, 0)
    in_specs = [pl.BlockSpec((tm, d), row), pl.BlockSpec((tm, n_exp), row), pl.BlockSpec((tm, n_exp), row)]
    args = [x, pos_tok, gate_tok]
    if final_w is not None:
        in_specs.append(pl.BlockSpec((1, d), lambda i, *_: (0, 0)))
        args.append(final_w.reshape(1, d))
    in_specs.append(pl.BlockSpec(memory_space=pl.ANY))
    args.append(ys)
    return pl.pallas_call(
        functools.partial(_combine_kernel, q_max=q_max, final_norm=final_w is not None),
        grid_spec=pltpu.PrefetchScalarGridSpec(
            num_scalar_prefetch=4,
            grid=(t // tm,),
            in_specs=in_specs,
            out_specs=pl.BlockSpec((tm, d), row),
            scratch_shapes=[pltpu.VMEM((tm, d), F32), pltpu.VMEM((2, MOE_CHUNK, d), BF16),
                            pltpu.SemaphoreType.DMA((2,)), pltpu.SMEM((1,), jnp.int32)]),
        out_shape=jax.ShapeDtypeStruct((t, d), F32),
        compiler_params=_cparams("arbitrary"),
        name="moe_combine",
    )(n_flat, flat_e, flat_c, flat_row, *args)


def _moe(x, norm_w, w_router, wg, wu, wd, layer, final_w=None, *, tm, tp, tf):
    t, d = x.shape
    n_exp = w_router.shape[1]
    n_tiles = t // tm
    worst = 2 * t + n_tiles * n_exp * (MOE_ALIGN - 1) + n_exp * (MOE_CHUNK + tp - 1)
    p_rows = (worst + tp - 1) // tp * tp
    h, sel, gate, rank, cnt = _route(x, norm_w, w_router, tm=tm)
    plan = _moe_plan(cnt[:, :, 0].astype(jnp.int32), tm=tm, tp=tp, p_rows=p_rows)
    xs = _dispatch(h, sel, rank, plan, tm=tm, p_rows=p_rows)
    ys = _gffn(xs, wg, wu, wd, layer, plan[4], plan[5], tp=tp, tf=tf)
    pos_tok = jnp.where(sel > 0.0, rank, -1.0).T
    return _combine(x, ys, pos_tok, gate.T, plan, final_w, tm=tm)


def _pad_rows(x, rows):
    if x.shape[0] == rows:
        return x
    return jnp.concatenate([x, jnp.zeros((rows - x.shape[0], x.shape[1]), x.dtype)], axis=0)


def _chunk_masks(rows):
    ri = lax.broadcasted_iota(jnp.int32, (rows, rows), 0)
    ci = lax.broadcasted_iota(jnp.int32, (rows, rows), 1)
    same = (ri // CHUNK) == (ci // CHUNK)
    return same & (ci <= ri), same


def _gla_kernel(*refs, rb, zero_init, has_alias):
    it = iter(refs)
    q_ref, k_ref, v_ref, g_ref, gk_ref = next(it), next(it), next(it), next(it), next(it)
    s0_ref = None if zero_init else next(it)
    wgk_ref, bgk_ref, nw_ref = next(it), next(it), next(it)
    if has_alias:
        next(it)
    o_ref, s_ref, st_ref = next(it), next(it), next(it)
    r = pl.program_id(1)
    rows = max(rb, CHUNK)
    n_chunks = rows // CHUNK
    hk = q_ref.shape[1] // GLA_HEADS
    hv = v_ref.shape[1] // GLA_HEADS

    @pl.when(r == 0)
    def _():
        for h in range(GLA_HEADS):
            if zero_init:
                st_ref[h] = jnp.zeros(st_ref.shape[1:], F32)
            else:
                st_ref[h] = s0_ref[0, h].T

    q = _pad_rows(q_ref[...] * hk ** -0.5, rows)
    k = _pad_rows(k_ref[...], rows)
    v = _pad_rows(v_ref[...], rows)
    la = _log_sigmoid(_bdot(gk_ref[...], wgk_ref[...]) + bgk_ref[...]) * (1.0 / GLA_GATE_NORMALIZER)
    la = _pad_rows(la, rows)
    causal, same = _chunk_masks(rows)
    gc = _dot_exact_lhs(causal, la)
    gt = _dot_exact_lhs(same, la)
    q_in = q * jnp.exp(gc)
    k_in = k * jnp.exp(-gc)
    k_out = k * jnp.exp(gt - gc)
    e_tot = jnp.exp(gt)
    tri = causal[:CHUNK, :CHUNK]
    for h in range(GLA_HEADS):
        st = st_ref[h]
        ks = slice(h * hk, (h + 1) * hk)
        vs = slice(h * hv, (h + 1) * hv)
        for c in range(n_chunks):
            rs = slice(c * CHUNK, (c + 1) * CHUNK)
            qi, ki, ko, vh = q_in[rs, ks], k_in[rs, ks], k_out[rs, ks], v[rs, vs]
            intra = jnp.where(tri, _bdot_nt(qi, ki), 0.0)
            o = _bdot(intra, vh) + _bdot_nt(qi, st)
            st = st * e_tot[c * CHUNK:c * CHUNK + 1, ks] + _bdot_tn(vh, ko)
            n_out = min(rb, CHUNK)
            og = _rms(o[:n_out], nw_ref[...]) * _silu(g_ref[c * CHUNK:c * CHUNK + n_out, vs])
            o_ref[c * CHUNK:c * CHUNK + n_out, vs] = og
        st_ref[h] = st

    @pl.when(r == pl.num_programs(1) - 1)
    def _():
        for h in range(GLA_HEADS):
            s_ref[0, h] = st_ref[h].T


def _gla_core(proj, s0, w_gk2, b_gk2, norm_w, o_full, *, n_seq, seq_len, row0, rb):
    t = proj.shape[0]
    dk = w_gk2.shape[1]
    dv = 2 * dk
    hk, hv = dk // GLA_HEADS, dv // GLA_HEADS
    nblk = seq_len // rb
    base = row0 // rb
    assert row0 % rb == 0 and seq_len % rb == 0

    def rowmap(col):
        return lambda b, r: (base + b * nblk + r, col)

    in_specs = [pl.BlockSpec((rb, dk), rowmap(0)), pl.BlockSpec((rb, dk), rowmap(1)),
                pl.BlockSpec((rb, dv), rowmap(1)), pl.BlockSpec((rb, dv), rowmap(2)),
                pl.BlockSpec((rb, LANES), rowmap((2 * dk + 2 * dv) // LANES))]
    args = [proj, proj, proj, proj, proj]
    if s0 is not None:
        in_specs.append(pl.BlockSpec((1, GLA_HEADS, hk, hv), lambda b, r: (b, 0, 0, 0)))
        args.append(s0)
    wgk = jnp.pad(w_gk2, ((0, LANES - w_gk2.shape[0]), (0, 0)))
    in_specs += [pl.BlockSpec((LANES, dk), lambda b, r: (0, 0)),
                 pl.BlockSpec((1, dk), lambda b, r: (0, 0)),
                 pl.BlockSpec((1, hv), lambda b, r: (0, 0))]
    args += [wgk, b_gk2.reshape(1, dk), norm_w.reshape(1, hv)]
    aliases = {}
    if o_full is not None:
        in_specs.append(pl.BlockSpec(memory_space=pl.ANY))
        aliases = {len(args): 0}
        args.append(o_full)
    kern = functools.partial(_gla_kernel, rb=rb, zero_init=s0 is None, has_alias=o_full is not None)
    return pl.pallas_call(
        kern,
        grid=(n_seq, nblk),
        in_specs=in_specs,
        out_specs=[pl.BlockSpec((rb, dv), rowmap(0)),
                   pl.BlockSpec((1, GLA_HEADS, hk, hv), lambda b, r: (b, 0, 0, 0))],
        out_shape=[jax.ShapeDtypeStruct((t, dv), F32),
                   jax.ShapeDtypeStruct((n_seq, GLA_HEADS, hk, hv), F32)],
        scratch_shapes=[pltpu.VMEM((GLA_HEADS, hv, hk), F32)],
        input_output_aliases=aliases,
        compiler_params=_cparams("parallel", "arbitrary"),
        name="gla_core",
    )(*args)


def _tile(n, pref):
    if n <= pref:
        return n
    for c in range(pref, 7, -8):
        if n % c == 0:
            return c
    return n


def _pad_cols(w, n):
    return jnp.pad(w, ((0, 0), (0, n - w.shape[1])))


def _gla_layer(x, nw, s0_s, w_in, w_gk2, b_gk2, norm_w, w_out, dims):
    n_p, t_p, n_s, t_s = dims
    t = x.shape[0]
    tm = _tile(t, 1280)
    dk = w_gk2.shape[1]
    width = 6 * dk + LANES
    proj = _linear(x, _pad_cols(w_in, width), norm_w=nw, tm=tm, tn=_tile(width, 640), name="gla_in")
    o, sp = _gla_core(proj, None, w_gk2, b_gk2, norm_w, None, n_seq=n_p, seq_len=t_p, row0=0,
                      rb=min(t_p, 256))
    o, ss = _gla_core(proj, s0_s, w_gk2, b_gk2, norm_w, o, n_seq=n_s, seq_len=t_s, row0=n_p * t_p,
                      rb=t_s)
    x = _linear(o, w_out, res=x, tm=tm, tn=512, name="gla_out")
    return x, sp, ss


def _rope_tables(pos, half):
    freqs = np.exp(-math.log(ROPE_THETA) * np.arange(half, dtype=np.float64) / half)
    ang = np.asarray(pos, np.float64)[:, None] * freqs[None, :]
    cos, sin = np.cos(ang), np.sin(ang)
    return (jnp.asarray(np.concatenate([cos, cos], axis=-1), F32),
            jnp.asarray(np.concatenate([-sin, sin], axis=-1), F32))


def _swap_halves(w, axis=-1):
    a, b = jnp.split(w, 2, axis=axis)
    return jnp.concatenate([b, a], axis=axis)


def _mla_in_kernel(x_ref, nw_ref, w_ref, kvw_ref, cos_ref, sin_ref, cq_ref, ckv_ref, kr_ref):
    h = _rms(x_ref[...], nw_ref[...])
    y = _bdot(h, w_ref[...])
    cq_ref[...] = y[:, :MLA_Q_LORA]
    ckv_ref[...] = _rms(y[:, MLA_Q_LORA:MLA_Q_LORA + MLA_KV_LORA], kvw_ref[...])
    o = MLA_Q_LORA + MLA_KV_LORA
    kr_ref[...] = (y[:, o:o + MLA_ROPE] * cos_ref[...]
                   + y[:, o + LANES:o + LANES + MLA_ROPE] * sin_ref[...])


def _mla_in(x, nw, w_in, kv_norm_w, cos, sin, *, tm):
    t, d = x.shape
    o = MLA_Q_LORA + MLA_KV_LORA
    kr_w = w_in[:, o:o + MLA_ROPE]
    w_aug = jnp.concatenate([w_in[:, :o], _pad_cols(kr_w, LANES), _pad_cols(_swap_halves(kr_w), LANES)],
                            axis=1)
    wid = w_aug.shape[1]
    row = lambda i: (i, 0)
    fix = lambda i: (0, 0)
    return pl.pallas_call(
        _mla_in_kernel,
        grid=(t // tm,),
        in_specs=[pl.BlockSpec((tm, d), row), pl.BlockSpec((1, d), fix), pl.BlockSpec((d, wid), fix),
                  pl.BlockSpec((1, MLA_KV_LORA), fix), pl.BlockSpec((tm, MLA_ROPE), row),
                  pl.BlockSpec((tm, MLA_ROPE), row)],
        out_specs=[pl.BlockSpec((tm, MLA_Q_LORA), row), pl.BlockSpec((tm, MLA_KV_LORA), row),
                   pl.BlockSpec((tm, MLA_ROPE), row)],
        out_shape=[jax.ShapeDtypeStruct((t, MLA_Q_LORA), F32), jax.ShapeDtypeStruct((t, MLA_KV_LORA), F32),
                   jax.ShapeDtypeStruct((t, MLA_ROPE), F32)],
        compiler_params=_cparams("parallel"),
        name="mla_in",
    )(x, nw.reshape(1, d), w_aug, kv_norm_w.reshape(1, -1), cos, sin)


def _mla_q_kernel(cq_ref, qw_ref, wn_ref, wr_ref, ws_ref, wk_ref, cos_ref, sin_ref, ql_ref, qr_ref):
    cq = _rms(cq_ref[...], qw_ref[...]).astype(BF16)
    qn = jnp.dot(cq, wn_ref[...].astype(BF16), preferred_element_type=F32).astype(BF16)
    for j in range(MLA_HEADS // 2):
        ql = jnp.dot(qn[:, j * LANES:(j + 1) * LANES], wk_ref[j].astype(BF16), preferred_element_type=F32)
        ql_ref[2 * j] = ql[:, :MLA_KV_LORA]
        ql_ref[2 * j + 1] = ql[:, MLA_KV_LORA:]
    qr = jnp.dot(cq, wr_ref[...].astype(BF16), preferred_element_type=F32)
    qs = jnp.dot(cq, ws_ref[...].astype(BF16), preferred_element_type=F32)
    per = LANES // MLA_ROPE
    cos = jnp.concatenate([cos_ref[...]] * (MLA_HEADS // per), axis=-1)
    sin = jnp.concatenate([sin_ref[...]] * (MLA_HEADS // per), axis=-1)
    rot = qr * cos + qs * sin
    for h in range(MLA_HEADS):
        qr_ref[h] = rot[:, h * MLA_ROPE:(h + 1) * MLA_ROPE]


def _mla_q(cq, q_norm_w, w_uq, w_uk, cos, sin, *, tm):
    t = cq.shape[0]
    per = LANES // MLA_ROPE
    w3 = w_uq.reshape(MLA_Q_LORA, MLA_HEADS, MLA_NOPE + MLA_ROPE)
    w_nope = w3[:, :, :MLA_NOPE].reshape(MLA_Q_LORA, MLA_HEADS * MLA_NOPE)
    w_rope = w3[:, :, MLA_NOPE:].reshape(MLA_Q_LORA, MLA_HEADS * MLA_ROPE)
    w_swap = _swap_halves(w3[:, :, MLA_NOPE:]).reshape(MLA_Q_LORA, MLA_HEADS * MLA_ROPE)
    a = jnp.transpose(w_uk, (1, 2, 0))
    z = jnp.zeros_like(a[0::2])
    w_bd = jnp.concatenate([jnp.concatenate([a[0::2], z], axis=2),
                            jnp.concatenate([z, a[1::2]], axis=2)], axis=1)
    cos4 = jnp.tile(cos, (1, per))
    sin4 = jnp.tile(sin, (1, per))
    row = lambda i: (i, 0)
    fix2 = lambda i: (0, 0)
    return pl.pallas_call(
        _mla_q_kernel,
        grid=(t // tm,),
        in_specs=[pl.BlockSpec((tm, MLA_Q_LORA), row), pl.BlockSpec((1, MLA_Q_LORA), fix2),
                  pl.BlockSpec(w_nope.shape, fix2), pl.BlockSpec(w_rope.shape, fix2),
                  pl.BlockSpec(w_swap.shape, fix2), pl.BlockSpec(w_bd.shape, lambda i: (0, 0, 0)),
                  pl.BlockSpec((tm, LANES), row), pl.BlockSpec((tm, LANES), row)],
        out_specs=[pl.BlockSpec((MLA_HEADS, tm, MLA_KV_LORA), lambda i: (0, i, 0)),
                   pl.BlockSpec((MLA_HEADS, tm, MLA_ROPE), lambda i: (0, i, 0))],
        out_shape=[jax.ShapeDtypeStruct((MLA_HEADS, t, MLA_KV_LORA), F32),
                   jax.ShapeDtypeStruct((MLA_HEADS, t, MLA_ROPE), F32)],
        compiler_params=_cparams("parallel"),
        name="mla_q",
    )(cq, q_norm_w.reshape(1, -1), w_nope, w_rope, w_swap, w_bd, cos4, sin4)


MLA_QSCALE = (MLA_NOPE + MLA_ROPE) ** -0.5 * math.log2(math.e)


def _lane_repeat(x, width):
    return jnp.concatenate([x] * (width // LANES), axis=1)


def _flash_chunk(s, cb, m_ref, l_ref, acc_ref, rs):
    m_prev = m_ref[rs]
    m_new = jnp.maximum(m_prev, jnp.max(s, axis=-1, keepdims=True))
    alpha = jnp.exp2(m_prev - m_new)
    p = jnp.exp2(s - _lane_repeat(m_new, s.shape[1]))
    l_ref[rs] = alpha * l_ref[rs] + jnp.sum(p, axis=-1, keepdims=True)
    acc_ref[rs] = (_lane_repeat(alpha, acc_ref.shape[1]) * acc_ref[rs]
                   + jnp.dot(p.astype(BF16), cb, preferred_element_type=F32))
    m_ref[rs] = m_new


def _flash_init(ql_ref, qr_ref, qlb_ref, qrb_ref, m_ref, l_ref, acc_ref):
    rows = qlb_ref.shape[0]
    qlb_ref[...] = (ql_ref[...].reshape(rows, MLA_KV_LORA) * MLA_QSCALE).astype(BF16)
    qrb_ref[...] = (qr_ref[...].reshape(rows, MLA_ROPE) * MLA_QSCALE).astype(BF16)
    m_ref[...] = jnp.full(m_ref.shape, NEG_INF, F32)
    l_ref[...] = jnp.zeros(l_ref.shape, F32)
    acc_ref[...] = jnp.zeros(acc_ref.shape, F32)


def _mla_finish(acc_ref, l_ref, wv_ref, o_ref, rows_per_head):
    inv = _lane_repeat(1.0 / l_ref[...], acc_ref.shape[1])
    outs = []
    for j in range(MLA_HEADS // 2):
        pair = None
        for h in (2 * j, 2 * j + 1):
            rs = slice(h * rows_per_head, (h + 1) * rows_per_head)
            part = _bdot(acc_ref[rs] * inv[rs], wv_ref[h])
            pair = part if pair is None else pair + part
        outs.append(pair)
    o_ref[...] = jnp.concatenate(outs, axis=-1)


def _mla_prompt_kernel(wq_ref, wk_ref, ql_ref, qr_ref, c_ref, r_ref, wv_ref, o_ref, qlb_ref, qrb_ref, m_ref,
                       l_ref, acc_ref, *, tq, tk, rc):
    qi = wq_ref[pl.program_id(1)]
    kj = wk_ref[pl.program_id(1)]
    last = (qi * tq + tq - 1) // tk
    rows = MLA_HEADS * tq

    @pl.when(kj == 0)
    def _():
        _flash_init(ql_ref, qr_ref, qlb_ref, qrb_ref, m_ref, l_ref, acc_ref)

    def step(masked):
        cb = c_ref[...].astype(BF16)
        rb = r_ref[...].astype(BF16)
        chunks = [slice(ch * rc, (ch + 1) * rc) for ch in range(rows // rc)]
        scores = [_bdot_nt(qlb_ref[rs], cb) + _bdot_nt(qrb_ref[rs], rb) for rs in chunks]
        if masked:
            q_pos = qi * tq + lax.broadcasted_iota(jnp.int32, (rc, tk), 0) % tq
            k_pos = kj * tk + lax.broadcasted_iota(jnp.int32, (rc, tk), 1)
            scores = [jnp.where(k_pos <= q_pos, s, NEG_INF) for s in scores]
        probs, alphas = [], []
        for rs, s in zip(chunks, scores):
            m_prev = m_ref[rs]
            m_new = jnp.maximum(m_prev, jnp.max(s, axis=-1, keepdims=True))
            alpha = jnp.exp2(m_prev - m_new)
            p = jnp.exp2(s - _lane_repeat(m_new, tk))
            l_ref[rs] = alpha * l_ref[rs] + jnp.sum(p, axis=-1, keepdims=True)
            m_ref[rs] = m_new
            probs.append(p.astype(BF16))
            alphas.append(alpha)
        for rs, p, alpha in zip(chunks, probs, alphas):
            acc_ref[rs] = (_lane_repeat(alpha, MLA_KV_LORA) * acc_ref[rs]
                           + jnp.dot(p, cb, preferred_element_type=F32))

    @pl.when(kj < last)
    def _():
        step(False)

    @pl.when(kj == last)
    def _():
        step(True)
        _mla_finish(acc_ref, l_ref, wv_ref, o_ref, tq)


def _pad_uv(w_uv):
    a = jnp.transpose(w_uv, (1, 0, 2))
    z = jnp.zeros_like(a)
    even = (jnp.arange(a.shape[0]) % 2 == 0)[:, None, None]
    return jnp.concatenate([jnp.where(even, a, z), jnp.where(even, z, a)], axis=2)


def _mla_prompt_attn(q_lat, q_rope, c_kv, k_r, w_uv_pad, *, n_seq, seq_len, tq, tk):
    t = c_kv.shape[0]
    nq, nk = seq_len // tq, seq_len // tk
    assert tk % tq == 0
    pairs = [(i, j) for i in range(nq) for j in range((i * tq + tq - 1) // tk + 1)]
    work_q = jnp.asarray([p[0] for p in pairs], jnp.int32)
    work_k = jnp.asarray([p[1] for p in pairs], jnp.int32)

    def qmap(b, w, wq, wk):
        return (0, b * nq + wq[w], 0)

    def kmap(b, w, wq, wk):
        return (b * nk + wk[w], 0)

    rows = MLA_HEADS * tq
    return pl.pallas_call(
        functools.partial(_mla_prompt_kernel, tq=tq, tk=tk, rc=4 * tq),
        grid_spec=pltpu.PrefetchScalarGridSpec(
            num_scalar_prefetch=2,
            grid=(n_seq, len(pairs)),
            in_specs=[pl.BlockSpec((MLA_HEADS, tq, MLA_KV_LORA), qmap),
                      pl.BlockSpec((MLA_HEADS, tq, MLA_ROPE), qmap),
                      pl.BlockSpec((tk, MLA_KV_LORA), kmap), pl.BlockSpec((tk, MLA_ROPE), kmap),
                      pl.BlockSpec(w_uv_pad.shape, lambda b, w, wq, wk: (0, 0, 0))],
            out_specs=pl.BlockSpec((tq, MLA_HEADS * MLA_V), lambda b, w, wq, wk: (b * nq + wq[w], 0)),
            scratch_shapes=_flash_scratch(rows)),
        out_shape=jax.ShapeDtypeStruct((t, MLA_HEADS * MLA_V), F32),
        compiler_params=_cparams("parallel", "arbitrary"),
        name="mla_prompt_attn",
    )(work_q, work_k, q_lat, q_rope, c_kv, k_r, w_uv_pad)


def _flash_scratch(rows):
    return [pltpu.VMEM((rows, MLA_KV_LORA), BF16), pltpu.VMEM((rows, MLA_ROPE), BF16),
            pltpu.VMEM((rows, LANES), F32), pltpu.VMEM((rows, LANES), F32),
            pltpu.VMEM((rows, MLA_KV_LORA), F32)]


def _mla_sample_kernel(pt_ref, ql_ref, qr_ref, lat_hbm, kr_hbm, cn_ref, rn_ref, wv_ref, alias_ref, o_ref,
                       lat_buf, kr_buf, sem, qlb_ref, qrb_ref, m_ref, l_ref, acc_ref, *, n_pg, t_s):
    del alias_ref
    b = pl.program_id(0)
    n_groups = pt_ref.shape[1] // n_pg
    rows = MLA_HEADS * t_s
    everything = slice(0, rows)
    _flash_init(ql_ref, qr_ref, qlb_ref, qrb_ref, m_ref, l_ref, acc_ref)

    def copies(seq, g, slot):
        out = []
        for p in range(n_pg):
            page = pt_ref[seq, g * n_pg + p]
            out.append(pltpu.make_async_copy(lat_hbm.at[0, page], lat_buf.at[slot, p], sem.at[0, slot]))
            out.append(pltpu.make_async_copy(kr_hbm.at[0, page], kr_buf.at[slot, p], sem.at[1, slot]))
        return out

    @pl.when(b == 0)
    def _():
        for c in copies(0, 0, 0):
            c.start()

    def body(g, carry):
        slot = g % 2

        @pl.when(g + 1 < n_groups)
        def _():
            for c in copies(b, g + 1, 1 - slot):
                c.start()

        @pl.when((g + 1 == n_groups) & (b + 1 < pl.num_programs(0)))
        def _():
            for c in copies(b + 1, 0, 0):
                c.start()

        for c in copies(b, 0, slot):
            c.wait()
        page = lat_buf.shape[2]
        cb = lat_buf[slot].reshape(n_pg * page, MLA_KV_LORA).astype(BF16)
        rbt = jnp.concatenate([kr_buf[slot, p] for p in range(n_pg)], axis=1).astype(BF16)
        s = _bdot_nt(qlb_ref[...], cb) + jnp.dot(qrb_ref[...], rbt, preferred_element_type=F32)
        _flash_chunk(s, cb, m_ref, l_ref, acc_ref, everything)
        return carry

    lax.fori_loop(0, n_groups, body, 0)
    cb = _pad_rows(cn_ref[...], LANES).astype(BF16)
    rb = _pad_rows(rn_ref[...], LANES).astype(BF16)
    s = _bdot_nt(qlb_ref[...], cb) + _bdot_nt(qrb_ref[...], rb)
    q_t = lax.broadcasted_iota(jnp.int32, (rows, LANES), 0) % t_s
    k_t = lax.broadcasted_iota(jnp.int32, (rows, LANES), 1)
    s = jnp.where(k_t <= q_t, s, NEG_INF)
    _flash_chunk(s, cb, m_ref, l_ref, acc_ref, everything)
    _mla_finish(acc_ref, l_ref, wv_ref, o_ref, t_s)


def _mla_sample_attn(q_lat, q_rope, c_kv, k_r, cache_lat, cache_kr_t, page_table, w_uv_pad, o_full, *,
                     n_seq, t_s, row0, n_pg):
    n_pages = page_table.shape[1]
    page = cache_lat.shape[2]
    assert n_pages % (2 * n_pg) == 0 and row0 % t_s == 0
    base = row0 // t_s
    qmap = lambda b, pt: (0, base + b, 0)
    newmap = lambda b, pt: (base + b, 0)
    hbm = pl.BlockSpec(memory_space=pl.ANY)
    in_specs = [pl.BlockSpec((MLA_HEADS, t_s, MLA_KV_LORA), qmap), pl.BlockSpec((MLA_HEADS, t_s, MLA_ROPE), qmap),
                hbm, hbm,
                pl.BlockSpec((t_s, MLA_KV_LORA), newmap), pl.BlockSpec((t_s, MLA_ROPE), newmap),
                pl.BlockSpec(w_uv_pad.shape, lambda b, pt: (0, 0, 0)), hbm]
    scratch = [pltpu.VMEM((2, n_pg, page, MLA_KV_LORA), F32), pltpu.VMEM((2, n_pg, MLA_ROPE, page), F32),
               pltpu.SemaphoreType.DMA((2, 2))]
    return pl.pallas_call(
        functools.partial(_mla_sample_kernel, n_pg=n_pg, t_s=t_s),
        grid_spec=pltpu.PrefetchScalarGridSpec(
            num_scalar_prefetch=1,
            grid=(n_seq,),
            in_specs=in_specs,
            out_specs=pl.BlockSpec((t_s, MLA_HEADS * MLA_V), newmap),
            scratch_shapes=scratch + _flash_scratch(MLA_HEADS * t_s)),
        out_shape=jax.ShapeDtypeStruct(o_full.shape, F32),
        input_output_aliases={len(in_specs): 0},
        compiler_params=_cparams("arbitrary"),
        name="mla_sample_attn",
    )(page_table, q_lat, q_rope, cache_lat, cache_kr_t, c_kv, k_r, w_uv_pad, o_full)


def _positions(dims, past_len):
    n_p, t_p, n_s, t_s = dims
    return np.concatenate([np.tile(np.arange(t_p), n_p), np.tile(past_len + np.arange(t_s), n_s)])


def _mla_layer(x, nw, cache_lat, cache_kr, page_table, w_in, q_norm_w, w_uq, kv_norm_w, w_uk, w_uv, w_out,
               dims):
    n_p, t_p, n_s, t_s = dims
    t = x.shape[0]
    n_pages = page_table.shape[1]
    past_len = n_pages * cache_lat.shape[2]
    cos, sin = _rope_tables(_positions(dims, past_len), MLA_ROPE // 2)
    cq, ckv, kr = _mla_in(x, nw, w_in, kv_norm_w, cos, sin, tm=_tile(t, 640))
    q_lat, q_rope = _mla_q(cq, q_norm_w, w_uq, w_uk, cos, sin, tm=_tile(t, 256))
    wv = _pad_uv(w_uv)
    o = _mla_prompt_attn(q_lat, q_rope, ckv, kr, wv, n_seq=n_p, seq_len=t_p, tq=min(t_p, 128),
                         tk=min(t_p, 512))
    o = _mla_sample_attn(q_lat, q_rope, ckv, kr, cache_lat, jnp.swapaxes(cache_kr, 2, 3), page_table, wv, o,
                         n_seq=n_s, t_s=t_s, row0=n_p * t_p, n_pg=math.gcd(n_pages // 2, 32))
    x = _linear(o, w_out, res=x, tm=_tile(t, 1280), tn=512, name="mla_out")
    return x, ckv, kr


def _lane_halves(a):
    half = LANES // 2
    low = lax.broadcasted_iota(jnp.int32, a.shape, 1) < half
    rolled = pltpu.roll(a, half, axis=1)
    head0 = (jnp.where(low, a, 0.0), jnp.where(low, 0.0, rolled))
    head1 = (jnp.where(low, rolled, 0.0), jnp.where(low, 0.0, a))
    return head0, head1


def _swa_heads(q, k_all, v_all, sink_ref, mask, o_ref):
    rq = q.shape[0]
    scale = SWA_HD ** -0.5
    top = lax.broadcasted_iota(jnp.int32, (2 * rq, 1), 0) < rq
    for cg in range(SWA_KV_HEADS // 2):
        k_heads = _lane_halves(k_all[:, cg * LANES:(cg + 1) * LANES])
        v_heads = _lane_halves(v_all[:, cg * LANES:(cg + 1) * LANES])
        for sub in range(2):
            kh = 2 * cg + sub
            (k_lo, k_hi), (v_lo, v_hi) = k_heads[sub], v_heads[sub]
            qs = jnp.concatenate([q[:, (2 * kh) * LANES:(2 * kh + 1) * LANES],
                                  q[:, (2 * kh + 1) * LANES:(2 * kh + 2) * LANES]], axis=0)
            acc = None
            for which, (kk, vv) in enumerate(((k_lo, v_lo), (k_hi, v_hi))):
                s = jnp.where(mask, _bdot_nt(qs, kk) * scale, NEG_INF)
                sink = jnp.where(top, sink_ref[4 * kh + which], sink_ref[4 * kh + 2 + which])
                m = jnp.maximum(jnp.max(s, axis=-1, keepdims=True), sink)
                e = jnp.exp(s - m)
                p = e / (jnp.sum(e, axis=-1, keepdims=True) + jnp.exp(sink - m))
                part = _bdot(p, vv)
                acc = part if acc is None else acc + part
            o_ref[:, (2 * kh) * LANES:(2 * kh + 1) * LANES] = acc[:rq]
            o_ref[:, (2 * kh + 1) * LANES:(2 * kh + 2) * LANES] = acc[rq:]


def _swa_prompt_kernel(sink_ref, q_ref, kp_ref, kc_ref, vp_ref, vc_ref, o_ref):
    n = pl.program_id(1)
    w = q_ref.shape[0]
    k_all = jnp.concatenate([kp_ref[...], kc_ref[...]], axis=0)
    v_all = jnp.concatenate([vp_ref[...], vc_ref[...]], axis=0)
    r = lax.broadcasted_iota(jnp.int32, (2 * w, 2 * w), 0) % w
    c = lax.broadcasted_iota(jnp.int32, (2 * w, 2 * w), 1)
    mask = (c >= r) & (c <= r + w) & ((n > 0) | (c >= w))
    _swa_heads(q_ref[...], k_all, v_all, sink_ref, mask, o_ref)


def _swa_prompt_attn(qkv, sinks, *, n_seq, seq_len):
    t = qkv.shape[0]
    w = WINDOW
    nb = seq_len // w
    dq = SWA_HEADS * SWA_HD
    dkv = SWA_KV_HEADS * SWA_HD
    kcol = dq // dkv
    cur = lambda col: (lambda b, n: (b * nb + n, col))
    prev = lambda col: (lambda b, n: (b * nb + jnp.maximum(n - 1, 0), col))
    return pl.pallas_call(
        _swa_prompt_kernel,
        grid=(n_seq, nb),
        in_specs=[pl.BlockSpec(memory_space=pltpu.SMEM),
                  pl.BlockSpec((w, dq), cur(0)),
                  pl.BlockSpec((w, dkv), prev(kcol)), pl.BlockSpec((w, dkv), cur(kcol)),
                  pl.BlockSpec((w, dkv), prev(kcol + 1)), pl.BlockSpec((w, dkv), cur(kcol + 1))],
        out_specs=pl.BlockSpec((w, dq), cur(0)),
        out_shape=jax.ShapeDtypeStruct((t, dq), F32),
        compiler_params=_cparams("parallel", "parallel"),
        name="swa_prompt_attn",
    )(sinks, qkv, qkv, qkv, qkv, qkv)


def _swa_sample_kernel(sink_ref, q_ref, kn_ref, vn_ref, kc_ref, vc_ref, alias_ref, o_ref, ko_ref, vo_ref):
    del alias_ref
    t_s = q_ref.shape[0]
    w = kc_ref.shape[0]
    k_all = jnp.concatenate([kc_ref[...], kn_ref[...]], axis=0)
    v_all = jnp.concatenate([vc_ref[...], vn_ref[...]], axis=0)
    r = lax.broadcasted_iota(jnp.int32, (2 * t_s, w + t_s), 0) % t_s
    c = lax.broadcasted_iota(jnp.int32, (2 * t_s, w + t_s), 1)
    mask = (c <= w + r) & (c >= r)
    _swa_heads(q_ref[...], k_all, v_all, sink_ref, mask, o_ref)
    ko_ref[...] = k_all[t_s:]
    vo_ref[...] = v_all[t_s:]


def _swa_sample_attn(qkv, cache_k, cache_v, sinks, o_full, *, n_seq, t_s, row0):
    w = cache_k.shape[1]
    dq = SWA_HEADS * SWA_HD
    dkv = SWA_KV_HEADS * SWA_HD
    kcol = dq // dkv
    base = row0 // t_s
    new = lambda col: (lambda b: (base + b, col))
    seq = lambda b: (b, 0, 0)
    return pl.pallas_call(
        _swa_sample_kernel,
        grid=(n_seq,),
        in_specs=[pl.BlockSpec(memory_space=pltpu.SMEM),
                  pl.BlockSpec((t_s, dq), new(0)),
                  pl.BlockSpec((t_s, dkv), new(kcol)), pl.BlockSpec((t_s, dkv), new(kcol + 1)),
                  pl.BlockSpec((None, w, dkv), seq), pl.BlockSpec((None, w, dkv), seq),
                  pl.BlockSpec(memory_space=pl.ANY)],
        out_specs=[pl.BlockSpec((t_s, dq), new(0)),
                   pl.BlockSpec((None, w, dkv), seq), pl.BlockSpec((None, w, dkv), seq)],
        out_shape=[jax.ShapeDtypeStruct(o_full.shape, F32),
                   jax.ShapeDtypeStruct(cache_k.shape, F32), jax.ShapeDtypeStruct(cache_v.shape, F32)],
        input_output_aliases={6: 0},
        compiler_params=_cparams("parallel"),
        name="swa_sample_attn",
    )(sinks, qkv, qkv, qkv, cache_k, cache_v, o_full)


def _swa_layer(x, nw, cache_k, cache_v, w_qkv, b_qkv, sinks, w_out, b_out, dims):
    n_p, t_p, n_s, t_s = dims
    t = x.shape[0]
    tm = _tile(t, 1280)
    dq = SWA_HEADS * SWA_HD
    dkv = SWA_KV_HEADS * SWA_HD
    qkv = _linear(x, w_qkv, norm_w=nw, bias=b_qkv, tm=tm, tn=512, name="swa_in")
    o = _swa_prompt_attn(qkv, sinks, n_seq=n_p, seq_len=t_p)
    o, k_s, v_s = _swa_sample_attn(qkv, cache_k.reshape(n_s, WINDOW, dkv), cache_v.reshape(n_s, WINDOW, dkv),
                                   sinks, o, n_seq=n_s, t_s=t_s, row0=n_p * t_p)
    x = _linear(o, w_out, bias=b_out, res=x, tm=tm, tn=512, name="swa_out")
    kv_p = jnp.stack([lax.slice(qkv, ((b + 1) * t_p - WINDOW, dq), ((b + 1) * t_p, dq + 2 * dkv))
                      for b in range(n_p)])
    kv_shape = (n_p, WINDOW, SWA_KV_HEADS, SWA_HD)
    k_p = kv_p[:, :, :dkv].reshape(kv_shape)
    v_p = kv_p[:, :, dkv:].reshape(kv_shape)
    return x, k_p, v_p, k_s.reshape(cache_k.shape), v_s.reshape(cache_v.shape)


def _l2norm(x):
    return x * lax.rsqrt(jnp.sum(x * x, axis=-1, keepdims=True) + 1e-6)


def _split2(x):
    h1 = x.astype(BF16)
    return h1, (x - h1.astype(F32)).astype(BF16)


def _bmm3_parts(a_parts, b_parts):
    (a1, a2), (b1, b2) = a_parts, b_parts
    dot = lambda x, y: jnp.einsum("bij,bjk->bik", x, y, preferred_element_type=F32)
    return dot(a1, b1) + dot(a1, b2) + dot(a2, b1)


def _bmm3(a, b):
    return _bmm3_parts(_split2(a), _split2(b))


def _unit_lower_inverse(low):
    n = low.shape[-1]
    eye = (lax.broadcasted_iota(jnp.int32, (n, n), 0) == lax.broadcasted_iota(jnp.int32, (n, n), 1))
    eye = eye.astype(F32)[None]
    power = -low
    inv = eye + power
    parts = _split2(power)
    for _ in range(int(math.log2(n)) - 1):
        parts = _split2(_bmm3_parts(parts, parts))
        inv = inv + _bmm3_parts(_split2(inv), parts)
    return inv


def _dn_kernel(*refs, rb, zero_init, has_alias):
    it = iter(refs)
    x_ref, z_ref, ab_ref = next(it), next(it), next(it)
    s0_ref, c0_ref = (None, None) if zero_init else (next(it), next(it))
    cw_ref, al_ref, dt_ref, nw_ref = next(it), next(it), next(it), next(it)
    if has_alias:
        next(it)
    o_ref, s_ref, co_ref = next(it), next(it), next(it)
    st_ref, xp_ref = next(it), next(it)
    r = pl.program_id(1)
    rows = max(rb, CHUNK)
    n_chunks = rows // CHUNK
    halo = 8
    dqk = DN_HEADS * DN_HK

    @pl.when(r == 0)
    def _():
        if zero_init:
            st_ref[...] = jnp.zeros(st_ref.shape, F32)
            xp_ref[0:halo] = jnp.zeros((halo, xp_ref.shape[1]), F32)
        else:
            st_ref[...] = s0_ref[0]
            xp_ref[0:halo] = c0_ref[0]

    xp_ref[halo:halo + rb] = x_ref[...]
    full = xp_ref[...]
    conv = full[halo:] * cw_ref[DN_CONV - 1:DN_CONV]
    for w in range(DN_CONV - 1):
        conv = conv + pltpu.roll(full, DN_CONV - 1 - w, axis=0)[halo:] * cw_ref[w:w + 1]
    tail = xp_ref[rb:rb + halo]
    co_ref[0] = tail
    xp_ref[0:halo] = tail
    qkv = _pad_rows(_silu(conv), rows)
    ab = ab_ref[...]
    g_all = _pad_rows(-jnp.exp(al_ref[...]) * _softplus(ab + dt_ref[...]), rows)
    beta_all = _pad_rows(1.0 / (1.0 + jnp.exp(-ab)), rows)

    ri = lax.broadcasted_iota(jnp.int32, (rows, rows), 0)
    ci = lax.broadcasted_iota(jnp.int32, (rows, rows), 1)
    same = (ri // CHUNK) == (ci // CHUNK)
    causal = same & (ci <= ri)
    strict = same & (ci < ri)
    upper = (same & (ri <= ci)).astype(BF16)
    gc_col = _dot_exact_lhs(causal, g_all)
    gt_col = _dot_exact_lhs(same, g_all)
    g1, g2, g3 = _split3(g_all)
    tn = lambda a: lax.dot_general(a, upper, (((0,), (0,)), ((), ())), preferred_element_type=F32)
    gc_row = tn(g1) + tn(g2) + tn(g3)

    lows, rhss, attns, qds, kos, gls = [], [], [], [], [], []
    for h in range(DN_HEADS):
        hs = slice(h * DN_HK, (h + 1) * DN_HK)
        q = _l2norm(qkv[:, hs]) * DN_HK ** -0.5
        k = _l2norm(qkv[:, dqk + h * DN_HK:dqk + (h + 1) * DN_HK])
        v = qkv[:, 2 * dqk + h * DN_HV:2 * dqk + (h + 1) * DN_HV]
        beta = beta_all[:, DN_HEADS + h:DN_HEADS + h + 1]
        gc = gc_col[:, h:h + 1]
        gt = gt_col[:, h:h + 1]
        decay = jnp.where(causal, jnp.exp(jnp.where(causal, gc - gc_row[h:h + 1, :], 0.0)), 0.0)
        kb = k * beta
        low = jnp.where(strict, _bdot_nt(kb, k) * decay, 0.0)
        attn = _bdot_nt(q, k) * decay
        rhs = jnp.concatenate([v * beta, kb * jnp.exp(gc)], axis=-1)
        q_dec = q * jnp.exp(gc)
        k_out = k * jnp.exp(gt - gc)
        g_last = jnp.exp(gt)
        for c in range(n_chunks):
            rs = slice(c * CHUNK, (c + 1) * CHUNK)
            lows.append(low[rs, rs])
            attns.append(attn[rs, rs])
            rhss.append(rhs[rs])
            qds.append(q_dec[rs])
            kos.append(k_out[rs])
            gls.append(g_last[c * CHUNK:c * CHUNK + 1])
    sol = _bmm3(_unit_lower_inverse(jnp.stack(lows)), jnp.stack(rhss))
    n_out = min(rb, CHUNK)
    for h in range(DN_HEADS):
        s = st_ref[h]
        for c in range(n_chunks):
            i = h * n_chunks + c
            u, wm = sol[i, :, :DN_HV], sol[i, :, DN_HV:]
            v_new = u - _bdot(wm, s)
            o = _bdot(qds[i], s) + _bdot(attns[i], v_new)
            s = s * gls[i] + _bdot_tn(kos[i], v_new)
            zs = z_ref[c * CHUNK:c * CHUNK + n_out, h * DN_HV:(h + 1) * DN_HV]
            o_ref[c * CHUNK:c * CHUNK + n_out, h * DN_HV:(h + 1) * DN_HV] = (
                _rms(o[:n_out], nw_ref[...]) * _silu(zs))
        st_ref[h] = s

    @pl.when(r == pl.num_programs(1) - 1)
    def _():
        s_ref[0] = st_ref[...]


def _dn_core(proj, s0, conv0, conv_w, a_log, dt_bias, norm_w, o_full, *, n_seq, seq_len, row0, rb):
    t = proj.shape[0]
    dconv = conv_w.shape[1]
    dz = DN_HEADS * DN_HV
    nblk = seq_len // rb
    base = row0 // rb
    assert row0 % rb == 0 and seq_len % rb == 0 and rb % 8 == 0

    def rowmap(col):
        return lambda b, r: (base + b * nblk + r, col)

    seq4 = lambda b, r: (b, 0, 0, 0)
    seq3 = lambda b, r: (b, 0, 0)
    fix = lambda b, r: (0, 0)
    in_specs = [pl.BlockSpec((rb, dconv), rowmap(0)), pl.BlockSpec((rb, dz), rowmap(dconv // dz)),
                pl.BlockSpec((rb, LANES), rowmap((dconv + dz) // LANES))]
    args = [proj, proj, proj]
    if s0 is not None:
        in_specs += [pl.BlockSpec((1, DN_HEADS, DN_HK, DN_HV), seq4), pl.BlockSpec((1, 8, dconv), seq3)]
        args += [s0, conv0]
    in_specs += [pl.BlockSpec((DN_CONV, dconv), fix), pl.BlockSpec((1, LANES), fix),
                 pl.BlockSpec((1, LANES), fix), pl.BlockSpec((1, DN_HV), fix)]
    args += [conv_w, _pad_cols(a_log.reshape(1, -1), LANES), _pad_cols(dt_bias.reshape(1, -1), LANES),
             norm_w.reshape(1, -1)]
    aliases = {}
    if o_full is not None:
        in_specs.append(pl.BlockSpec(memory_space=pl.ANY))
        aliases = {len(args): 0}
        args.append(o_full)
    kern = functools.partial(_dn_kernel, rb=rb, zero_init=s0 is None, has_alias=o_full is not None)
    return pl.pallas_call(
        kern,
        grid=(n_seq, nblk),
        in_specs=in_specs,
        out_specs=[pl.BlockSpec((rb, dz), rowmap(0)),
                   pl.BlockSpec((1, DN_HEADS, DN_HK, DN_HV), seq4),
                   pl.BlockSpec((1, 8, dconv), seq3)],
        out_shape=[jax.ShapeDtypeStruct((t, dz), F32),
                   jax.ShapeDtypeStruct((n_seq, DN_HEADS, DN_HK, DN_HV), F32),
                   jax.ShapeDtypeStruct((n_seq, 8, dconv), F32)],
        scratch_shapes=[pltpu.VMEM((DN_HEADS, DN_HK, DN_HV), F32), pltpu.VMEM((rb + 8, dconv), F32)],
        input_output_aliases=aliases,
        compiler_params=_cparams("parallel", "arbitrary"),
        name="dn_core",
    )(*args)


def _dn_layer(x, nw, s0_s, conv0_s, w_in, conv_w, a_log, dt_bias, norm_w, w_out, dims):
    n_p, t_p, n_s, t_s = dims
    t = x.shape[0]
    tm = _tile(t, 1280)
    dconv = conv_w.shape[1]
    dz = DN_HEADS * DN_HV
    width = dconv + dz + LANES
    proj = _linear(x, _pad_cols(w_in, width), norm_w=nw, tm=tm, tn=_tile(width, 1408), name="dn_in")
    o, sp, cp = _dn_core(proj, None, None, conv_w, a_log, dt_bias, norm_w, None, n_seq=n_p, seq_len=t_p,
                         row0=0, rb=min(t_p, 128))
    conv0 = jnp.pad(conv0_s, ((0, 0), (8 - conv0_s.shape[1], 0), (0, 0)))
    o, ss, cs = _dn_core(proj, s0_s, conv0, conv_w, a_log, dt_bias, norm_w, o, n_seq=n_s, seq_len=t_s,
                         row0=n_p * t_p, rb=t_s)
    x = _linear(o, w_out, res=x, tm=tm, tn=512, name="dn_out")
    keep = DN_CONV - 1
    return x, sp, ss, cp[:, 8 - keep:], cs[:, 8 - keep:]


def kernel(x_prompt, x_sample, state_gla, cache_mla_latent, cache_mla_krope, cache_swa_k, cache_swa_v, state_delta, state_delta_conv, page_table, norm_w, final_norm_w, gla_w_in, gla_w_gk2, gla_b_gk2, gla_norm_w, gla_w_out, mla_w_in, mla_q_norm_w, mla_w_uq, mla_kv_norm_w, mla_w_uk, mla_w_uv, mla_w_out, swa_w_qkv, swa_b_qkv, swa_sinks, swa_w_out, swa_b_out, dn_w_in, dn_conv_w, dn_a_log, dn_dt_bias, dn_norm_w, dn_w_out, ffn_w_gate, ffn_w_up, ffn_w_down, moe_w_router, moe_w_gate, moe_w_up, moe_w_down):
    n_p, t_p, d = x_prompt.shape
    n_s, t_s, _ = x_sample.shape
    dims = (n_p, t_p, n_s, t_s)
    x = jnp.concatenate([x_prompt.reshape(n_p * t_p, d), x_sample.reshape(n_s * t_s, d)], axis=0)
    t = x.shape[0]
    tm = _tile(t, 1280)
    n_tp = n_p * t_p

    x, gla_p, gla_s = _gla_layer(x, norm_w[0, 0], state_gla[0], gla_w_in[0], gla_w_gk2[0], gla_b_gk2[0],
                                 gla_norm_w[0], gla_w_out[0], dims)
    x = _ffn(x, norm_w[0, 1], ffn_w_gate, ffn_w_up, ffn_w_down, 0, tm=tm, tf=512)
    moe_tiles = dict(tm=_tile(t, 640), tp=1280, tf=512)

    x, ckv, kr = _mla_layer(x, norm_w[1, 0], cache_mla_latent[0:1], cache_mla_krope[0:1], page_table,
                            mla_w_in[0], mla_q_norm_w[0], mla_w_uq[0], mla_kv_norm_w[0], mla_w_uk[0],
                            mla_w_uv[0], mla_w_out[0], dims)
    x = _moe(x, norm_w[1, 1], moe_w_router[0], moe_w_gate, moe_w_up, moe_w_down, 0, **moe_tiles)

    x, swk_p, swv_p, swk_s, swv_s = _swa_layer(x, norm_w[2, 0], cache_swa_k[0], cache_swa_v[0],
                                               swa_w_qkv[0], swa_b_qkv[0], swa_sinks[0], swa_w_out[0],
                                               swa_b_out[0], dims)
    x = _ffn(x, norm_w[2, 1], ffn_w_gate, ffn_w_up, ffn_w_down, 1, tm=tm, tf=512)

    x, dn_p, dn_s, cv_p, cv_s = _dn_layer(x, norm_w[3, 0], state_delta[0], state_delta_conv[0], dn_w_in[0],
                                          dn_conv_w[0], dn_a_log[0], dn_dt_bias[0], dn_norm_w[0],
                                          dn_w_out[0], dims)
    y = _moe(x, norm_w[3, 1], moe_w_router[1], moe_w_gate, moe_w_up, moe_w_down, 1, final_norm_w,
             **moe_tiles)

    lead = lambda a: a[None]
    return (y[:n_tp].reshape(n_p, t_p, d), y[n_tp:].reshape(n_s, t_s, d),
            lead(gla_p), lead(gla_s),
            lead(ckv[:n_tp].reshape(n_p, t_p, -1)), lead(ckv[n_tp:].reshape(n_s, t_s, -1)),
            lead(kr[:n_tp].reshape(n_p, t_p, -1)), lead(kr[n_tp:].reshape(n_s, t_s, -1)),
            lead(swk_p), lead(swk_s), lead(swv_p), lead(swv_s),
            lead(dn_p), lead(dn_s), lead(cv_p), lead(cv_s))
```
